```python
import math
import jax, jax.numpy as jnp
from jax import lax
import numpy as np

D_MODEL = 1024
BATCH = 8
SEQ = 2048
DEPTH = 1
DEC_BATCH = 128
DEC_SEQ = 1
PAST_LEN = 16384
PAGE_SIZE = 128

N_MEM = 256
POOL_WIDTH = D_MODEL // 2
POOL_WINDOWS = (2, 4, 8, 16)
N_POOL_GROUPS = len(POOL_WINDOWS)
POOL_GROUP = POOL_WIDTH // N_POOL_GROUPS
POOL_BUF = max(POOL_WINDOWS) - 1
MLSTM_HEADS = 4
MLSTM_WIDTH = D_MODEL
MLSTM_DH = MLSTM_WIDTH // MLSTM_HEADS
MLSTM_CHUNK = 64
XATTN_HEADS = 4
XATTN_WIDTH = D_MODEL // 2
XATTN_DH = XATTN_WIDTH // XATTN_HEADS
N_BRANCH = 3
D_FF = 4 * D_MODEL
N_IN = POOL_WIDTH + 4 * MLSTM_WIDTH + 2 * MLSTM_HEADS + XATTN_WIDTH + N_BRANCH * D_MODEL
EPS = 1e-6
NEG_INF = -1e30

kernel_name = 'hybrid_pool_mlstm_memxattn_decode_step'


def rmsnorm(x, g):
    xf = x.astype(jnp.float32)
    y = xf * lax.rsqrt(jnp.mean(xf * xf, axis=-1, keepdims=True) + EPS)
    return (y * g.astype(jnp.float32)).astype(x.dtype)


def pool_mix(u, buf, pos0, w_pool, pool_scale):
    B, T, P = u.shape
    z = jnp.concatenate([buf.astype(jnp.float32), u.astype(jnp.float32)], axis=1)
    cs = jnp.concatenate([jnp.zeros((B, 1, P), jnp.float32), jnp.cumsum(z, axis=1)], axis=1)
    t = jnp.arange(T)
    means = []
    for g, w in enumerate(POOL_WINDOWS):
        sl = slice(g * POOL_GROUP, (g + 1) * POOL_GROUP)
        hi = cs[:, POOL_BUF + 1:POOL_BUF + 1 + T, sl]
        lo = cs[:, POOL_BUF + 1 - w:POOL_BUF + 1 - w + T, sl]
        cnt = jnp.minimum(w, pos0 + t + 1).astype(jnp.float32)
        means.append((hi - lo) / cnt[None, :, None])
    d = (jnp.concatenate(means, axis=-1) - u.astype(jnp.float32)).astype(u.dtype)
    d = d.reshape(B, T, N_POOL_GROUPS, POOL_GROUP)
    y = jnp.einsum('btgc,gcd->btgd', d, w_pool).reshape(B, T, P) * pool_scale
    new_buf = z[:, -POOL_BUF:].astype(buf.dtype)
    return y, new_buf


def mlstm_chunkwise(q, k, v, ig, lf, C0, n0, m0):
    B, T, H, Dh = q.shape
    L = MLSTM_CHUNK if T % MLSTM_CHUNK == 0 else T
    nc = T // L

    def chunks(a):
        return jnp.moveaxis(a.reshape((B, nc, L) + a.shape[2:]), 1, 0)

    causal = jnp.tril(jnp.ones((L, L), dtype=bool))

    def step(carry, inp):
        C, n, m = carry
        qc, kc, vc, ic, fc = inp
        b = jnp.cumsum(fc, axis=1).transpose(0, 2, 1)
        ic = ic.transpose(0, 2, 1)
        logw = b[..., :, None] - b[..., None, :] + ic[..., None, :]
        logw = jnp.where(causal, logw, NEG_INF)
        log_inter = b + m[..., None]
        m_t = jnp.maximum(log_inter, jnp.max(logw, axis=-1))
        w_intra = jnp.exp(logw - m_t[..., None])
        w_inter = jnp.exp(log_inter - m_t)
        s = jnp.einsum('blhd,bshd->bhls', qc, kc) * w_intra
        num = jnp.einsum('bhls,bshd->bhld', s, vc) + w_inter[..., None] * jnp.einsum('bhvk,blhk->bhlv', C, qc)
        den = jnp.sum(s, axis=-1) + w_inter * jnp.einsum('bhk,blhk->bhl', n, qc)
        h = num / jnp.maximum(jnp.abs(den), jnp.exp(-m_t))[..., None]
        m_new = m_t[..., -1]
        decay = jnp.exp(b[..., -1] + m - m_new)
        w_end = jnp.exp(b[..., -1:] - b + ic - m_new[..., None])
        C_new = decay[..., None, None] * C + jnp.einsum('bhs,bshv,bshk->bhvk', w_end, vc, kc)
        n_new = decay[..., None] * n + jnp.einsum('bhs,bshk->bhk', w_end, kc)
        return (C_new, n_new, m_new), h.transpose(0, 2, 1, 3)

    xs = (chunks(q), chunks(k), chunks(v), chunks(ig), chunks(lf))
    (C, n, m), hs = lax.scan(step, (C0, n0, m0), xs)
    h = jnp.moveaxis(hs, 0, 1).reshape(B, T, H, Dh)
    return h, C, n, m


def memory_kv(mem, g_mem, w_mem_kv):
    B, M, _ = mem.shape
    kv = rmsnorm(mem, g_mem) @ w_mem_kv
    k = kv[..., :XATTN_WIDTH].reshape(B, M, XATTN_HEADS, XATTN_DH)
    v = kv[..., XATTN_WIDTH:].reshape(B, M, XATTN_HEADS, XATTN_DH)
    return k, v


def memory_attention(qx, mk, mv):
    s = jnp.einsum('bthd,bmhd->bhtm', qx, mk).astype(jnp.float32) * (XATTN_DH ** -0.5)
    p = jax.nn.softmax(s, axis=-1).astype(mv.dtype)
    return jnp.einsum('bhtm,bmhd->bthd', p, mv)


def layer_forward(x, pool_buf, C0, n0, m0, mk, mv, pos0, g_pre_mix, w_in, b_if, w_pool, pool_scale,
                  w_br_pool, w_br_mlstm, w_br_xattn, w_out, g_post_mix, g_pre_mlp, w_ff1, w_ff2, g_post_mlp):
    B, T, _ = x.shape
    h = rmsnorm(x, g_pre_mix)
    proj = h @ w_in
    o0 = POOL_WIDTH
    o1 = o0 + 4 * MLSTM_WIDTH
    o2 = o1 + 2 * MLSTM_HEADS
    o3 = o2 + XATTN_WIDTH
    u = proj[..., :o0]
    qkvo = proj[..., o0:o1].reshape(B, T, 4, MLSTM_HEADS, MLSTM_DH)
    gif = proj[..., o1:o2].astype(jnp.float32) + b_if.astype(jnp.float32)
    xq = proj[..., o2:o3].reshape(B, T, XATTN_HEADS, XATTN_DH)
    gates = jax.nn.sigmoid(proj[..., o3:].reshape(B, T, N_BRANCH, D_MODEL))

    y_pool, new_buf = pool_mix(u, pool_buf, pos0, w_pool, pool_scale)

    q = qkvo[:, :, 0].astype(jnp.float32)
    k = qkvo[:, :, 1].astype(jnp.float32) * (MLSTM_DH ** -0.5)
    v = qkvo[:, :, 2].astype(jnp.float32)
    o_gate = jax.nn.sigmoid(qkvo[:, :, 3].astype(jnp.float32))
    ig = gif[..., :MLSTM_HEADS]
    lf = jax.nn.log_sigmoid(gif[..., MLSTM_HEADS:])
    hm, C, n, m = mlstm_chunkwise(q, k, v, ig, lf, C0.astype(jnp.float32), n0.astype(jnp.float32),
                                  m0.astype(jnp.float32))
    y_mlstm = (o_gate * hm).reshape(B, T, MLSTM_WIDTH).astype(x.dtype)

    y_x = memory_attention(xq, mk, mv).reshape(B, T, XATTN_WIDTH)

    merged = (gates[:, :, 0] * (y_pool @ w_br_pool) + gates[:, :, 1] * (y_mlstm @ w_br_mlstm)
              + gates[:, :, 2] * (y_x @ w_br_xattn))
    x = x + rmsnorm(merged @ w_out, g_post_mix)

    h2 = rmsnorm(x, g_pre_mlp)
    f = jnp.square(jax.nn.relu(h2 @ w_ff1)) @ w_ff2
    x = x + rmsnorm(f, g_post_mlp)
    return x, new_buf, C.astype(C0.dtype), n.astype(n0.dtype), m.astype(m0.dtype)


def setup_inputs(seed: int = 0) -> dict:
    key = jax.random.key(seed)
    ks = jax.random.split(key, 32)
    f32 = jnp.float32

    def nrm(k, shape, scale):
        return jax.random.normal(k, shape, f32) * scale

    def gain(k, shape):
        return 1.0 + 0.05 * jax.random.normal(k, shape, f32)

    ig_bias = nrm(ks[14], (DEPTH, MLSTM_HEADS), 0.1)
    fg_bias = 3.0 + jnp.linspace(0.0, 3.0, MLSTM_HEADS, dtype=f32)[None, :] + nrm(ks[15], (DEPTH, MLSTM_HEADS), 0.1)
    return {
        'x_prompt': nrm(ks[0], (BATCH, SEQ, D_MODEL), 1.0),
        'x_sample': nrm(ks[1], (DEC_BATCH, DEC_SEQ, D_MODEL), 1.0),
        'mem_prompt': nrm(ks[2], (BATCH, N_MEM, D_MODEL), 1.0),
        'state_pool_buf': nrm(ks[3], (DEPTH, DEC_BATCH, POOL_BUF, POOL_WIDTH), 1.0),
        'state_mlstm_C': nrm(ks[4], (DEPTH, DEC_BATCH, MLSTM_HEADS, MLSTM_DH, MLSTM_DH), 0.1),
        'state_mlstm_n': nrm(ks[5], (DEPTH, DEC_BATCH, MLSTM_HEADS, MLSTM_DH), 0.1),
        'state_mlstm_m': nrm(ks[6], (DEPTH, DEC_BATCH, MLSTM_HEADS), 0.5),
        'cache_mem_k': nrm(ks[7], (DEPTH, DEC_BATCH, N_MEM, XATTN_HEADS, XATTN_DH), 1.0),
        'cache_mem_v': nrm(ks[8], (DEPTH, DEC_BATCH, N_MEM, XATTN_HEADS, XATTN_DH), 1.0),
        'g_pre_mix': gain(ks[9], (DEPTH, D_MODEL)),
        'w_in': nrm(ks[10], (DEPTH, D_MODEL, N_IN), D_MODEL ** -0.5),
        'b_if': jnp.concatenate([ig_bias, fg_bias], axis=-1),
        'w_pool': nrm(ks[11], (DEPTH, N_POOL_GROUPS, POOL_GROUP, POOL_GROUP), POOL_GROUP ** -0.5),
        'pool_scale': gain(ks[12], (DEPTH, POOL_WIDTH)),
        'g_mem': gain(ks[13], (DEPTH, D_MODEL)),
        'w_mem_kv': nrm(ks[16], (DEPTH, D_MODEL, 2 * XATTN_WIDTH), D_MODEL ** -0.5),
        'w_br_pool': nrm(ks[17], (DEPTH, POOL_WIDTH, D_MODEL), POOL_WIDTH ** -0.5),
        'w_br_mlstm': nrm(ks[18], (DEPTH, MLSTM_WIDTH, D_MODEL), MLSTM_WIDTH ** -0.5),
        'w_br_xattn': nrm(ks[19], (DEPTH, XATTN_WIDTH, D_MODEL), XATTN_WIDTH ** -0.5),
        'w_out': nrm(ks[20], (DEPTH, D_MODEL, D_MODEL), D_MODEL ** -0.5),
        'g_post_mix': gain(ks[21], (DEPTH, D_MODEL)),
        'g_pre_mlp': gain(ks[22], (DEPTH, D_MODEL)),
        'w_ff1': nrm(ks[23], (DEPTH, D_MODEL, D_FF), D_MODEL ** -0.5),
        'w_ff2': nrm(ks[24], (DEPTH, D_FF, D_MODEL), D_FF ** -0.5),
        'g_post_mlp': gain(ks[25], (DEPTH, D_MODEL)),
    }


def reference(x_prompt, x_sample, mem_prompt, state_pool_buf, state_mlstm_C, state_mlstm_n, state_mlstm_m,
              cache_mem_k, cache_mem_v, g_pre_mix, w_in, b_if, w_pool, pool_scale, g_mem, w_mem_kv,
              w_br_pool, w_br_mlstm, w_br_xattn, w_out, g_post_mix, g_pre_mlp, w_ff1, w_ff2, g_post_mlp):
    xp = x_prompt
    xs = x_sample
    sdt = state_mlstm_C.dtype
    buf_p_l, C_p_l, n_p_l, m_p_l, mk_p_l, mv_p_l = [], [], [], [], [], []
    buf_s_l, C_s_l, n_s_l, m_s_l = [], [], [], []
    for l in range(DEPTH):
        w = (g_pre_mix[l], w_in[l], b_if[l], w_pool[l], pool_scale[l], w_br_pool[l], w_br_mlstm[l],
             w_br_xattn[l], w_out[l], g_post_mix[l], g_pre_mlp[l], w_ff1[l], w_ff2[l], g_post_mlp[l])
        mk_p, mv_p = memory_kv(mem_prompt, g_mem[l], w_mem_kv[l])
        zbuf = jnp.zeros((BATCH, POOL_BUF, POOL_WIDTH), xp.dtype)
        zC = jnp.zeros((BATCH, MLSTM_HEADS, MLSTM_DH, MLSTM_DH), sdt)
        zn = jnp.zeros((BATCH, MLSTM_HEADS, MLSTM_DH), sdt)
        zm = jnp.zeros((BATCH, MLSTM_HEADS), sdt)
        xp, bp, Cp, npv, mp = layer_forward(xp, zbuf, zC, zn, zm, mk_p, mv_p, 0, *w)
        buf_p_l.append(bp); C_p_l.append(Cp); n_p_l.append(npv); m_p_l.append(mp)
        mk_p_l.append(mk_p); mv_p_l.append(mv_p)
        xs, bs, Cs, nsv, ms = layer_forward(xs, state_pool_buf[l], state_mlstm_C[l], state_mlstm_n[l],
                                            state_mlstm_m[l], cache_mem_k[l], cache_mem_v[l], PAST_LEN, *w)
        buf_s_l.append(bs); C_s_l.append(Cs); n_s_l.append(nsv); m_s_l.append(ms)
    return (xp, xs,
            jnp.stack(buf_p_l), jnp.stack(C_p_l), jnp.stack(n_p_l), jnp.stack(m_p_l),
            jnp.stack(mk_p_l), jnp.stack(mv_p_l),
            jnp.stack(buf_s_l), jnp.stack(C_s_l), jnp.stack(n_s_l), jnp.stack(m_s_l))
```

```python
import functools
import math

import jax
import jax.numpy as jnp
from jax import lax
from jax.experimental import pallas as pl
from jax.experimental.pallas import tpu as pltpu

F32 = jnp.float32
BF16 = jnp.bfloat16

D = 1024
POOL_W = 512
POOL_G = 128
POOL_WINDOWS = (2, 4, 8, 16)
POOL_BUF = 15
HEADS = 4
DH = 256
XH = 4
XDH = 128
XW = 512
N_MEM = 256
D_FF = 4096
EPS = 1e-6
NEG = -1e30
PAST_LEN = 16384
CHUNK = 128

VMEM_LIMIT = 56 * 1024 * 1024

O_U = 0
O_QKVO = POOL_W
O_GIF = O_QKVO + 4 * D
O_XQ = O_GIF + 2 * HEADS
O_GATE = O_XQ + XW
N_IN = O_GATE + 3 * D


def _params(sem):
    return pltpu.CompilerParams(dimension_semantics=sem, vmem_limit_bytes=VMEM_LIMIT)


def _const_spec(shape):
    nd = len(shape)
    return pl.BlockSpec(shape, lambda *_: (0,) * nd, pipeline_mode=pl.Buffered(1))


def _rms(x, g):
    ms = jnp.mean(x * x, axis=-1, keepdims=True)
    return x * lax.rsqrt(ms + EPS) * g


def _log_sigmoid(x):
    return jnp.minimum(x, 0.0) - jnp.log(1.0 + jnp.exp(-jnp.abs(x)))


def _sigmoid(x):
    return 1.0 / (1.0 + jnp.exp(-x))


def _dot(a, b):
    return jnp.dot(a, b, preferred_element_type=F32)


def _dot_nt(a, b):
    return lax.dot_general(a, b, (((1,), (1,)), ((), ())), preferred_element_type=F32)


def _dot_tn(a, b):
    return lax.dot_general(a, b, (((0,), (0,)), ((), ())), preferred_element_type=F32)


def _in_proj_kernel(x_ref, g_ref, w_ref, wgif_ref, u_ref, q_ref, k_ref, v_ref, og_ref, xq_ref,
                    gift_ref, gifc_ref):
    h = _rms(x_ref[...], g_ref[...]).astype(BF16)

    def seg(lo, n):
        return _dot(h, w_ref[:, lo:lo + n])

    u_ref[...] = seg(0, POOL_W)
    q_ref[...] = seg(POOL_W, D).astype(q_ref.dtype)
    k_ref[...] = (seg(POOL_W + D, D) * (DH ** -0.5)).astype(k_ref.dtype)
    v_ref[...] = seg(POOL_W + 2 * D, D).astype(v_ref.dtype)
    og_ref[...] = _sigmoid(seg(POOL_W + 3 * D, D))
    xq_ref[...] = seg(POOL_W + 4 * D, XW).astype(xq_ref.dtype)
    wg = wgif_ref[...]
    gift_ref[...] = _dot_nt(wg, h)
    gifc_ref[...] = _dot_nt(h, wg)


def _in_proj(x2, g, w_a, w_gif, tm, qkv_dtype):
    n = x2.shape[0]
    wa_cols = w_a.shape[1]
    row = lambda i: (i, 0)
    out_shape = (
        jax.ShapeDtypeStruct((n, POOL_W), F32),
        jax.ShapeDtypeStruct((n, D), qkv_dtype),
        jax.ShapeDtypeStruct((n, D), qkv_dtype),
        jax.ShapeDtypeStruct((n, D), qkv_dtype),
        jax.ShapeDtypeStruct((n, D), F32),
        jax.ShapeDtypeStruct((n, XW), qkv_dtype),
        jax.ShapeDtypeStruct((2 * HEADS, n), F32),
        jax.ShapeDtypeStruct((n, 2 * HEADS), F32),
    )
    return pl.pallas_call(
        _in_proj_kernel,
        out_shape=out_shape,
        grid=(n // tm,),
        in_specs=[
            pl.BlockSpec((tm, D), row),
            _const_spec((1, D)),
            _const_spec((D, wa_cols)),
            _const_spec((2 * HEADS, D)),
        ],
        out_specs=(
            pl.BlockSpec((tm, POOL_W), row),
            pl.BlockSpec((tm, D), row),
            pl.BlockSpec((tm, D), row),
            pl.BlockSpec((tm, D), row),
            pl.BlockSpec((tm, D), row),
            pl.BlockSpec((tm, XW), row),
            pl.BlockSpec((2 * HEADS, tm), lambda i: (0, i)),
            pl.BlockSpec((tm, 2 * HEADS), row),
        ),
        compiler_params=_params(("parallel",)),
        name="in_proj",
    )(x2, g, w_a, w_gif)


def _mem_kv_kernel(mem_ref, g_ref, w_ref, k_ref, v_ref):
    h = _rms(mem_ref[...], g_ref[...]).astype(BF16)
    kv = _dot(h, w_ref[...])
    k_ref[...] = kv[:, :XW]
    v_ref[...] = kv[:, XW:]


def _mem_kv(mem2, g, w):
    n = mem2.shape[0]
    tm = 512
    row = lambda i: (i, 0)
    return pl.pallas_call(
        _mem_kv_kernel,
        out_shape=(jax.ShapeDtypeStruct((n, XW), F32), jax.ShapeDtypeStruct((n, XW), F32)),
        grid=(n // tm,),
        in_specs=[pl.BlockSpec((tm, D), row), _const_spec((1, D)), _const_spec((D, 2 * XW))],
        out_specs=(pl.BlockSpec((tm, XW), row), pl.BlockSpec((tm, XW), row)),
        compiler_params=_params(("parallel",)),
        name="mem_kv",
    )(mem2, g, w)


def _pool_prompt_kernel(u_ref, w_ref, s_ref, y_ref, nb_ref):
    t_len = u_ref.shape[0]
    t_idx = lax.broadcasted_iota(jnp.int32, (t_len, POOL_G), 0)
    for g, win in enumerate(POOL_WINDOWS):
        cols = slice(g * POOL_G, (g + 1) * POOL_G)
        u = u_ref[:, cols]
        acc = u
        span = 1
        while span < win:
            shifted = pltpu.roll(acc, span, axis=0)
            acc = acc + jnp.where(t_idx >= span, shifted, 0.0)
            span *= 2
        cnt = jnp.minimum(t_idx + 1, win).astype(F32)
        d = (acc / cnt - u).astype(BF16)
        y = _dot(d, w_ref[g]) * s_ref[:, cols]
        y_ref[:, cols] = y.astype(y_ref.dtype)
    nb_ref[0] = u_ref[t_len - POOL_BUF:, :]


def _pool_prompt(u, w_pool, scale, batch, t_len):
    return pl.pallas_call(
        _pool_prompt_kernel,
        out_shape=(jax.ShapeDtypeStruct((batch * t_len, POOL_W), BF16),
                   jax.ShapeDtypeStruct((batch, POOL_BUF, POOL_W), F32)),
        grid=(batch,),
        in_specs=[pl.BlockSpec((t_len, POOL_W), lambda b: (b, 0)),
                  _const_spec((len(POOL_WINDOWS), POOL_G, POOL_G)),
                  _const_spec((1, POOL_W))],
        out_specs=(pl.BlockSpec((t_len, POOL_W), lambda b: (b, 0)),
                   pl.BlockSpec((1, POOL_BUF, POOL_W), lambda b: (b, 0, 0))),
        compiler_params=_params(("parallel",)),
        name="pool_prompt",
    )(u, w_pool, scale)


def _pool_decode_kernel(u_ref, buf_ref, w_ref, s_ref, y_ref, nb_ref):
    u_all = u_ref[...]
    for g, win in enumerate(POOL_WINDOWS):
        cols = slice(g * POOL_G, (g + 1) * POOL_G)
        u = u_all[:, cols]
        acc = u
        for j in range(POOL_BUF - (win - 1), POOL_BUF):
            acc = acc + buf_ref[:, j, cols]
        cnt = float(min(win, PAST_LEN + 1))
        d = (acc / cnt - u).astype(BF16)
        y = _dot(d, w_ref[g]) * s_ref[:, cols]
        y_ref[:, cols] = y.astype(y_ref.dtype)
    for j in range(POOL_BUF - 1):
        nb_ref[:, j, :] = buf_ref[:, j + 1, :]
    nb_ref[:, POOL_BUF - 1, :] = u_all


def _pool_decode(u, buf, w_pool, scale):
    n = u.shape[0]
    return pl.pallas_call(
        _pool_decode_kernel,
        out_shape=(jax.ShapeDtypeStruct((n, POOL_W), BF16),
                   jax.ShapeDtypeStruct((n, POOL_BUF, POOL_W), F32)),
        grid=(1,),
        in_specs=[_const_spec((n, POOL_W)), _const_spec((n, POOL_BUF, POOL_W)),
                  _const_spec((len(POOL_WINDOWS), POOL_G, POOL_G)), _const_spec((1, POOL_W))],
        out_specs=(pl.BlockSpec((n, POOL_W), lambda i: (0, 0)),
                   pl.BlockSpec((n, POOL_BUF, POOL_W), lambda i: (0, 0, 0))),
        compiler_params=_params(("arbitrary",)),
        name="pool_decode",
    )(u, buf, w_pool, scale)


def _scan_lanes(x, op, fill):
    t_len = x.shape[-1]
    lane = lax.broadcasted_iota(jnp.int32, x.shape, 1)
    k = 1
    while k < t_len:
        shifted = pltpu.roll(x, k, axis=1)
        x = op(x, jnp.where(lane >= k, shifted, fill))
        k *= 2
    return x


def _gates_kernel(gift_ref, bif_ref, row_ref, col_ref, mlast_ref, pack_ref):
    t_len = gift_ref.shape[1]
    g = gift_ref[...] + bif_ref[...]
    ig = g[0:HEADS, :]
    lf = _log_sigmoid(g[HEADS:2 * HEADS, :])
    b_cum = _scan_lanes(lf, jnp.add, 0.0)
    a = ig - b_cum
    m_run = jnp.maximum(_scan_lanes(a, jnp.maximum, NEG), 0.0)
    m_tot = b_cum + m_run
    row_ref[0:HEADS, :] = a
    row_ref[HEADS:2 * HEADS, :] = m_run
    mlast_ref[0] = m_tot[:, t_len - 1:t_len]
    pack_ref[...] = jnp.zeros(pack_ref.shape, F32)
    pack_ref[0:HEADS, :] = a
    pack_ref[HEADS:2 * HEADS, :] = m_run
    pack_ref[2 * HEADS:3 * HEADS, :] = m_tot
    for c in range(t_len // 128):
        col_ref[c * 128:(c + 1) * 128, :] = pack_ref[:, c * 128:(c + 1) * 128].T


def _gates(gift, bif, batch, t_len):
    return pl.pallas_call(
        _gates_kernel,
        out_shape=(jax.ShapeDtypeStruct((2 * HEADS, batch * t_len), F32),
                   jax.ShapeDtypeStruct((batch * t_len, 128), F32),
                   jax.ShapeDtypeStruct((batch, HEADS, 1), F32)),
        grid=(batch,),
        in_specs=[pl.BlockSpec((2 * HEADS, t_len), lambda b: (0, b)), _const_spec((2 * HEADS, 1))],
        out_specs=(pl.BlockSpec((2 * HEADS, t_len), lambda b: (0, b)),
                   pl.BlockSpec((t_len, 128), lambda b: (b, 0)),
                   pl.BlockSpec((1, HEADS, 1), lambda b: (b, 0, 0))),
        scratch_shapes=[pltpu.VMEM((128, t_len), F32)],
        compiler_params=_params(("parallel",)),
        name="mlstm_gates",
    )(gift, bif)


def _mlstm_prompt_kernel(q_ref, k_ref, v_ref, og_ref, row_ref, col_ref,
                         y_ref, c_out_ref, n_out_ref, c_ref, n_ref, mp_ref):
    c_idx = pl.program_id(1)
    n_chunks = pl.num_programs(1)
    L = q_ref.shape[0]

    @pl.when(c_idx == 0)
    def _():
        c_ref[...] = jnp.zeros(c_ref.shape, F32)
        n_ref[...] = jnp.zeros(n_ref.shape, F32)
        mp_ref[...] = jnp.zeros(mp_ref.shape, F32)

    t_idx = lax.broadcasted_iota(jnp.int32, (L, L), 0)
    s_idx = lax.broadcasted_iota(jnp.int32, (L, L), 1)
    causal = s_idx <= t_idx
    for h in range(HEADS):
        hs = slice(h * DH, (h + 1) * DH)
        qh = q_ref[:, hs]
        kh = k_ref[:, hs]
        vh = v_ref[:, hs]
        a_row = row_ref[h:h + 1, :]
        a_col = col_ref[:, h:h + 1]
        m_col = col_ref[:, HEADS + h:HEADS + h + 1]
        mt_col = col_ref[:, 2 * HEADS + h:2 * HEADS + h + 1]
        m_prev = mp_ref[h][0:1, 0:1]
        m_end = m_col[L - 1:L, :]

        dmat = jnp.exp(jnp.where(causal, a_row - m_col, NEG))
        sw = _dot_nt(qh, kh) * dmat
        w_inter = jnp.exp(m_prev - m_col)
        c_old = c_ref[h]
        n_old = n_ref[h]
        inter = _dot_nt(qh, c_old.astype(BF16))
        num = _dot(sw.astype(BF16), vh) + w_inter * inter
        nq = jnp.sum(qh.astype(F32) * n_old, axis=-1, keepdims=True)
        den = jnp.sum(sw, axis=-1, keepdims=True) + w_inter * nq
        hval = num / jnp.maximum(jnp.abs(den), jnp.exp(-mt_col))
        y_ref[:, hs] = (og_ref[:, hs] * hval).astype(y_ref.dtype)

        w_end = jnp.exp(a_col - m_end)
        decay = jnp.exp(m_prev - m_end)
        vw = (vh.astype(F32) * w_end).astype(BF16)
        c_ref[h] = decay * c_old + _dot_tn(vw, kh)
        n_ref[h] = decay * n_old + jnp.sum(kh.astype(F32) * w_end, axis=0, keepdims=True)
        mp_ref[h] = jnp.broadcast_to(m_end, mp_ref.shape[1:])

    @pl.when(c_idx == n_chunks - 1)
    def _():
        c_out_ref[0] = c_ref[...]
        n_out_ref[0] = n_ref[...]


def _mlstm_prompt(q, k, v, og, rowp, colp, batch, t_len):
    L = CHUNK
    nc = t_len // L
    tok = lambda b, c: (b * nc + c, 0)
    return pl.pallas_call(
        _mlstm_prompt_kernel,
        out_shape=(jax.ShapeDtypeStruct((batch * t_len, D), BF16),
                   jax.ShapeDtypeStruct((batch, HEADS, DH, DH), F32),
                   jax.ShapeDtypeStruct((batch, HEADS, 1, DH), F32)),
        grid=(batch, nc),
        in_specs=[pl.BlockSpec((L, D), tok), pl.BlockSpec((L, D), tok), pl.BlockSpec((L, D), tok),
                  pl.BlockSpec((L, D), tok),
                  pl.BlockSpec((2 * HEADS, L), lambda b, c: (0, b * nc + c)),
                  pl.BlockSpec((L, 128), tok)],
        out_specs=(pl.BlockSpec((L, D), tok),
                   pl.BlockSpec((1, HEADS, DH, DH), lambda b, c: (b, 0, 0, 0)),
                   pl.BlockSpec((1, HEADS, 1, DH), lambda b, c: (b, 0, 0, 0))),
        scratch_shapes=[pltpu.VMEM((HEADS, DH, DH), F32), pltpu.VMEM((HEADS, 1, DH), F32),
                        pltpu.VMEM((HEADS, 8, 128), F32)],
        compiler_params=_params(("parallel", "arbitrary")),
        name="mlstm_prompt",
    )(q, k, v, og, rowp, colp)


def _mlstm_decode_kernel(q_ref, k_ref, v_ref, og_ref, gif_ref, bif_ref, c0_ref, n0_ref, m0_ref,
                         y_ref, c_out_ref, n_out_ref, m_out_ref):
    bb = q_ref.shape[0]
    row0 = lax.broadcasted_iota(jnp.int32, (8, DH), 0) == 0

    def body(b, carry):
        gi = gif_ref[b] + bif_ref[...]
        m0 = m0_ref[b]
        for h in range(HEADS):
            hs = slice(h * DH, (h + 1) * DH)
            qr = q_ref[b, :, hs]
            kr = k_ref[b, :, hs]
            vr = v_ref[b, :, hs]
            ig = gi[:, h:h + 1]
            lf = _log_sigmoid(gi[:, HEADS + h:HEADS + h + 1])
            m_old = m0[:, h:h + 1]
            m_new = jnp.maximum(lf + m_old, ig)
            w_i = jnp.exp(ig - m_new)
            w_f = jnp.exp(lf + m_old - m_new)
            c_old = c0_ref[b, h]
            n_old = n0_ref[b, h:h + 1, :]
            q8 = jnp.broadcast_to(qr, (8, DH)).astype(BF16)
            cq = _dot_nt(q8, c_old.astype(BF16))[0:1, :]
            qk = jnp.sum(qr * kr, axis=-1, keepdims=True)
            nq = jnp.sum(n_old * qr, axis=-1, keepdims=True)
            num = (w_i * qk) * vr + w_f * cq
            den = w_i * qk + w_f * nq
            hval = num / jnp.maximum(jnp.abs(den), jnp.exp(-m_new))
            y_ref[b, :, hs] = (og_ref[b, :, hs] * hval).astype(y_ref.dtype)
            v8 = jnp.where(row0, jnp.broadcast_to(vr, (8, DH)), 0.0).astype(BF16)
            k8 = jnp.where(row0, jnp.broadcast_to(kr, (8, DH)), 0.0).astype(BF16)
            c_out_ref[b, h] = w_f * c_old + w_i * _dot_tn(v8, k8)
            n_out_ref[b, h:h + 1, :] = w_f * n_old + w_i * kr
            m_out_ref[b, :, h:h + 1] = m_new
        return carry

    lax.fori_loop(0, bb, body, 0)


def _mlstm_decode(q, k, v, og, gifc, bif_row, c0, n0, m0):
    n = q.shape[0]
    bb = 8
    r3 = lambda i: (i, 0, 0)
    q3, k3, v3, og3 = (a.reshape(n, 1, D) for a in (q, k, v, og))
    gif3 = gifc.reshape(n, 1, 2 * HEADS)
    m03 = m0.reshape(n, 1, HEADS)
    y, c_new, n_new, m_new = pl.pallas_call(
        _mlstm_decode_kernel,
        out_shape=(jax.ShapeDtypeStruct((n, 1, D), BF16),
                   jax.ShapeDtypeStruct((n, HEADS, DH, DH), F32),
                   jax.ShapeDtypeStruct((n, HEADS, DH), F32),
                   jax.ShapeDtypeStruct((n, 1, HEADS), F32)),
        grid=(n // bb,),
        in_specs=[pl.BlockSpec((bb, 1, D), r3), pl.BlockSpec((bb, 1, D), r3),
                  pl.BlockSpec((bb, 1, D), r3), pl.BlockSpec((bb, 1, D), r3),
                  pl.BlockSpec((bb, 1, 2 * HEADS), r3), _const_spec((1, 2 * HEADS)),
                  pl.BlockSpec((bb, HEADS, DH, DH), lambda i: (i, 0, 0, 0)),
                  pl.BlockSpec((bb, HEADS, DH), r3),
                  pl.BlockSpec((bb, 1, HEADS), r3)],
        out_specs=(pl.BlockSpec((bb, 1, D), r3),
                   pl.BlockSpec((bb, HEADS, DH, DH), lambda i: (i, 0, 0, 0)),
                   pl.BlockSpec((bb, HEADS, DH), r3),
                   pl.BlockSpec((bb, 1, HEADS), r3)),
        compiler_params=_params(("parallel",)),
        name="mlstm_decode",
    )(q3, k3, v3, og3, gif3, bif_row, c0, n0, m03)
    return y.reshape(n, D), c_new, n_new, m_new.reshape(n, HEADS)


def _xattn_prompt_kernel(xq_ref, mk_ref, mv_ref, y_ref):
    scale = XDH ** -0.5
    for h in range(XH):
        hs = slice(h * XDH, (h + 1) * XDH)
        s = _dot_nt(xq_ref[:, hs], mk_ref[:, hs].astype(BF16)) * scale
        p = jnp.exp(s - jnp.max(s, axis=-1, keepdims=True))
        l = jnp.sum(p, axis=-1, keepdims=True)
        o = _dot(p.astype(BF16), mv_ref[:, hs].astype(BF16)) / l
        y_ref[:, hs] = o.astype(y_ref.dtype)


def _xattn_prompt(xq, mk, mv, batch, t_len):
    tq = 512
    nq = t_len // tq
    return pl.pallas_call(
        _xattn_prompt_kernel,
        out_shape=jax.ShapeDtypeStruct((batch * t_len, XW), BF16),
        grid=(batch, nq),
        in_specs=[pl.BlockSpec((tq, XW), lambda b, i: (b * nq + i, 0)),
                  pl.BlockSpec((N_MEM, XW), lambda b, i: (b, 0)),
                  pl.BlockSpec((N_MEM, XW), lambda b, i: (b, 0))],
        out_specs=pl.BlockSpec((tq, XW), lambda b, i: (b * nq + i, 0)),
        compiler_params=_params(("parallel", "arbitrary")),
        name="xattn_prompt",
    )(xq, mk, mv)


def _xattn_decode_kernel(xq_ref, mk_ref, mv_ref, y_ref):
    scale = XDH ** -0.5
    q = xq_ref[...][:, None, :, :]
    s = jnp.sum(mk_ref[...] * q, axis=-1, keepdims=True) * scale
    p = jnp.exp(s - jnp.max(s, axis=1, keepdims=True))
    l = jnp.sum(p, axis=1)
    o = jnp.sum(p * mv_ref[...], axis=1) / l
    y_ref[...] = o.astype(y_ref.dtype)


def _xattn_decode(xq, mk, mv):
    n = xq.shape[0]
    bb = 8
    xq4 = xq.reshape(n, XH, XDH)
    y = pl.pallas_call(
        _xattn_decode_kernel,
        out_shape=jax.ShapeDtypeStruct((n, XH, XDH), BF16),
        grid=(n // bb,),
        in_specs=[pl.BlockSpec((bb, XH, XDH), lambda i: (i, 0, 0)),
                  pl.BlockSpec((bb, N_MEM, XH, XDH), lambda i: (i, 0, 0, 0)),
                  pl.BlockSpec((bb, N_MEM, XH, XDH), lambda i: (i, 0, 0, 0))],
        out_specs=pl.BlockSpec((bb, XH, XDH), lambda i: (i, 0, 0)),
        compiler_params=_params(("parallel",)),
        name="xattn_decode",
    )(xq4, mk, mv)
    return y.reshape(n, XW)


def _merge_kernel(x_ref, yp_ref, ym_ref, yx_ref, gpre_ref, gpost_ref, wg_ref, wbp_ref, wbm_ref,
                  wbx_ref, wo_ref, o_ref, merged_ref):
    x = x_ref[...]
    h = _rms(x, gpre_ref[...]).astype(BF16)
    yp = yp_ref[...]
    ym = ym_ref[...]
    yx = yx_ref[...]
    nchunk = 256
    for c in range(D // nchunk):
        cs = slice(c * nchunk, (c + 1) * nchunk)
        acc = _sigmoid(_dot(h, wg_ref[:, c * nchunk:(c + 1) * nchunk])) * _dot(yp, wbp_ref[:, cs])
        acc += (_sigmoid(_dot(h, wg_ref[:, D + c * nchunk:D + (c + 1) * nchunk]))
                * _dot(ym, wbm_ref[:, cs]))
        acc += (_sigmoid(_dot(h, wg_ref[:, 2 * D + c * nchunk:2 * D + (c + 1) * nchunk]))
                * _dot(yx, wbx_ref[:, cs]))
        merged_ref[:, cs] = acc.astype(BF16)
    o_ref[...] = x + _rms(_dot(merged_ref[...], wo_ref[...]), gpost_ref[...])


def _merge(x2, yp, ym, yx, gpre, gpost, wg, wbp, wbm, wbx, wo, tm):
    n = x2.shape[0]
    row = lambda i: (i, 0)
    return pl.pallas_call(
        _merge_kernel,
        out_shape=jax.ShapeDtypeStruct((n, D), F32),
        grid=(n // tm,),
        in_specs=[pl.BlockSpec((tm, D), row), pl.BlockSpec((tm, POOL_W), row),
                  pl.BlockSpec((tm, D), row), pl.BlockSpec((tm, XW), row),
                  _const_spec((1, D)), _const_spec((1, D)), _const_spec((D, 3 * D)),
                  _const_spec((POOL_W, D)), _const_spec((D, D)), _const_spec((XW, D)),
                  _const_spec((D, D))],
        out_specs=pl.BlockSpec((tm, D), row),
        scratch_shapes=[pltpu.VMEM((tm, D), BF16)],
        compiler_params=_params(("parallel",)),
        name="merge_out",
    )(x2, yp, ym, yx, gpre, gpost, wg, wbp, wbm, wbx, wo)


def _ffn_kernel(x_ref, gpre_ref, gpost_ref, w1_ref, w2_ref, o_ref):
    x = x_ref[...]
    h = _rms(x, gpre_ref[...]).astype(BF16)
    fchunk = 1024
    acc = None
    for c in range(D_FF // fchunk):
        a = _dot(h, w1_ref[:, c * fchunk:(c + 1) * fchunk])
        a = jnp.square(jnp.maximum(a, 0.0)).astype(BF16)
        part = _dot(a, w2_ref[c * fchunk:(c + 1) * fchunk, :])
        acc = part if acc is None else acc + part
    o_ref[...] = x + _rms(acc, gpost_ref[...])


def _ffn(x2, gpre, gpost, w1, w2, tm):
    n = x2.shape[0]
    row = lambda i: (i, 0)
    return pl.pallas_call(
        _ffn_kernel,
        out_shape=jax.ShapeDtypeStruct((n, D), F32),
        grid=(n // tm,),
        in_specs=[pl.BlockSpec((tm, D), row), _const_spec((1, D)), _const_spec((1, D)),
                  _const_spec((D, D_FF)), _const_spec((D_FF, D))],
        out_specs=pl.BlockSpec((tm, D), row),
        compiler_params=_params(("parallel",)),
        name="ffn",
    )(x2, gpre, gpost, w1, w2)


def kernel(x_prompt, x_sample, mem_prompt, state_pool_buf, state_mlstm_C, state_mlstm_n, state_mlstm_m, cache_mem_k, cache_mem_v, g_pre_mix, w_in, b_if, w_pool, pool_scale, g_mem, w_mem_kv, w_br_pool, w_br_mlstm, w_br_xattn, w_out, g_post_mix, g_pre_mlp, w_ff1, w_ff2, g_post_mlp):
    batch, t_len, _ = x_prompt.shape
    n_dec = x_sample.shape[0]
    assert w_in.shape[0] == 1, "single layer"

    w_in_b = w_in[0].astype(BF16)
    w_a = jnp.concatenate([w_in_b[:, :O_GIF], w_in_b[:, O_XQ:O_GATE]], axis=1)
    w_gif = w_in_b[:, O_GIF:O_XQ].T
    w_gate = w_in_b[:, O_GATE:]
    w_pool_b = w_pool[0].astype(BF16)
    w_kv_b = w_mem_kv[0].astype(BF16)
    wbp, wbm, wbx = (w[0].astype(BF16) for w in (w_br_pool, w_br_mlstm, w_br_xattn))
    wo = w_out[0].astype(BF16)
    w1 = w_ff1[0].astype(BF16)
    w2 = w_ff2[0].astype(BF16)
    g_mix, g_pm, g_mlp, g_pmlp, g_m = (g[0].reshape(1, D) for g in
                                       (g_pre_mix, g_post_mix, g_pre_mlp, g_post_mlp, g_mem))
    scale = pool_scale[0].reshape(1, POOL_W)
    bif = b_if[0]

    xp = x_prompt.reshape(batch * t_len, D)
    mk_p, mv_p = _mem_kv(mem_prompt.reshape(batch * N_MEM, D), g_m, w_kv_b)
    u, q, k, v, og, xq, gift, _ = _in_proj(xp, g_mix, w_a, w_gif, 512, BF16)
    y_pool, buf_p = _pool_prompt(u, w_pool_b, scale, batch, t_len)
    rowp, colp, m_last = _gates(gift, bif.reshape(2 * HEADS, 1), batch, t_len)
    y_ml, c_p, n_p = _mlstm_prompt(q, k, v, og, rowp, colp, batch, t_len)
    y_x = _xattn_prompt(xq, mk_p, mv_p, batch, t_len)
    x1 = _merge(xp, y_pool, y_ml, y_x, g_mix, g_pm, w_gate, wbp, wbm, wbx, wo, 512)
    yp = _ffn(x1, g_mlp, g_pmlp, w1, w2, 512)

    xs = x_sample.reshape(n_dec, D)
    u_s, q_s, k_s, v_s, og_s, xq_s, _, gifc_s = _in_proj(xs, g_mix, w_a, w_gif, n_dec, F32)
    ypool_s, buf_s = _pool_decode(u_s, state_pool_buf[0], w_pool_b, scale)
    yml_s, c_s, n_s, m_s = _mlstm_decode(q_s, k_s, v_s, og_s, gifc_s, bif.reshape(1, 2 * HEADS),
                                         state_mlstm_C[0], state_mlstm_n[0], state_mlstm_m[0])
    yx_s = _xattn_decode(xq_s, cache_mem_k[0], cache_mem_v[0])
    x1_s = _merge(xs, ypool_s, yml_s, yx_s, g_mix, g_pm, w_gate, wbp, wbm, wbx, wo, n_dec)
    ys = _ffn(x1_s, g_mlp, g_pmlp, w1, w2, n_dec)

    return (yp.reshape(batch, t_len, D), ys.reshape(n_dec, 1, D),
            buf_p[None], c_p[None], n_p.reshape(1, batch, HEADS, DH),
            m_last.reshape(1, batch, HEADS),
            mk_p.reshape(1, batch, N_MEM, XH, XDH), mv_p.reshape(1, batch, N_MEM, XH, XDH),
            buf_s[None], c_s[None], n_s[None], m_s[None])
```

```python
import jax
import jax.numpy as jnp
from jax import lax
from jax.experimental import pallas as pl
from jax.experimental.pallas import tpu as pltpu

F32 = jnp.float32
BF16 = jnp.bfloat16

D = 1024
POOL_W = 512
POOL_G = 128
POOL_WINDOWS = (2, 4, 8, 16)
POOL_BUF = 15
HEADS = 4
DH = 256
XH = 4
XDH = 128
XW = 512
N_MEM = 256
D_FF = 4096
EPS = 1e-6
NEG = -1e30
PAST_LEN = 16384

TM_PROMPT = 512
CHUNK = 256
MLSTM_GROUP = 4
DEC_BLOCK = 8

VMEM_LIMIT = 56 * 1024 * 1024

O_QKVO = POOL_W
O_GIF = O_QKVO + 4 * D
O_XQ = O_GIF + 2 * HEADS
O_GATE = O_XQ + XW


def _params(sem):
    return pltpu.CompilerParams(dimension_semantics=sem, vmem_limit_bytes=VMEM_LIMIT)


def _const_spec(shape):
    nd = len(shape)
    return pl.BlockSpec(shape, lambda *_: (0,) * nd, pipeline_mode=pl.Buffered(1))


def _rms(x, g):
    ms = jnp.mean(x * x, axis=-1, keepdims=True)
    return x * lax.rsqrt(ms + EPS) * g


def _log_sigmoid(x):
    return jnp.minimum(x, 0.0) - jnp.log(1.0 + jnp.exp(-jnp.abs(x)))


def _sigmoid(x):
    return 1.0 / (1.0 + jnp.exp(-x))


def _dot(a, b):
    return jnp.dot(a, b, preferred_element_type=F32)


def _dot_nt(a, b):
    return lax.dot_general(a, b, (((1,), (1,)), ((), ())), preferred_element_type=F32)


def _dot_tn(a, b):
    return lax.dot_general(a, b, (((0,), (0,)), ((), ())), preferred_element_type=F32)


def _in_proj_kernel(x_ref, g_ref, w_ref, wxq_ref, wgif_ref, u_ref, q_ref, k_ref, v_ref, og_ref,
                    xq_ref, gift_ref, gifc_ref):
    h = _rms(x_ref[...], g_ref[...]).astype(BF16)

    def seg(lo, n):
        return _dot(h, w_ref[:, lo:lo + n])

    u_ref[...] = seg(0, POOL_W)
    q_ref[...] = seg(POOL_W, D).astype(q_ref.dtype)
    k_ref[...] = (seg(POOL_W + D, D) * (DH ** -0.5)).astype(k_ref.dtype)
    v_ref[...] = seg(POOL_W + 2 * D, D).astype(v_ref.dtype)
    og_ref[...] = _sigmoid(seg(POOL_W + 3 * D, D))
    xq_ref[...] = _dot(h, wxq_ref[...]).astype(xq_ref.dtype)
    wg = wgif_ref[...]
    gift_ref[...] = _dot_nt(wg, h)
    gifc_ref[...] = _dot_nt(h, wg)


def _in_proj(x2, g, w_main, w_xq, w_gif, tm, qkv_dtype):
    n = x2.shape[0]
    row = lambda i: (i, 0)
    out_shape = (
        jax.ShapeDtypeStruct((n, POOL_W), F32),
        jax.ShapeDtypeStruct((n, D), qkv_dtype),
        jax.ShapeDtypeStruct((n, D), qkv_dtype),
        jax.ShapeDtypeStruct((n, D), qkv_dtype),
        jax.ShapeDtypeStruct((n, D), F32),
        jax.ShapeDtypeStruct((n, XW), qkv_dtype),
        jax.ShapeDtypeStruct((2 * HEADS, n), F32),
        jax.ShapeDtypeStruct((n, 2 * HEADS), F32),
    )
    return pl.pallas_call(
        _in_proj_kernel,
        out_shape=out_shape,
        grid=(n // tm,),
        in_specs=[
            pl.BlockSpec((tm, D), row),
            _const_spec((1, D)),
            _const_spec(w_main.shape),
            _const_spec(w_xq.shape),
            _const_spec((2 * HEADS, D)),
        ],
        out_specs=(
            pl.BlockSpec((tm, POOL_W), row),
            pl.BlockSpec((tm, D), row),
            pl.BlockSpec((tm, D), row),
            pl.BlockSpec((tm, D), row),
            pl.BlockSpec((tm, D), row),
            pl.BlockSpec((tm, XW), row),
            pl.BlockSpec((2 * HEADS, tm), lambda i: (0, i)),
            pl.BlockSpec((tm, 2 * HEADS), row),
        ),
        compiler_params=_params(("parallel",)),
        name="in_proj",
    )(x2, g, w_main, w_xq, w_gif)


def _mem_kv_kernel(mem_ref, g_ref, w_ref, k_ref, v_ref):
    h = _rms(mem_ref[...], g_ref[...]).astype(BF16)
    kv = _dot(h, w_ref[...])
    k_ref[...] = kv[:, :XW]
    v_ref[...] = kv[:, XW:]


def _mem_kv(mem2, g, w):
    n = mem2.shape[0]
    tm = TM_PROMPT
    row = lambda i: (i, 0)
    return pl.pallas_call(
        _mem_kv_kernel,
        out_shape=(jax.ShapeDtypeStruct((n, XW), F32), jax.ShapeDtypeStruct((n, XW), F32)),
        grid=(n // tm,),
        in_specs=[pl.BlockSpec((tm, D), row), _const_spec((1, D)), _const_spec((D, 2 * XW))],
        out_specs=(pl.BlockSpec((tm, XW), row), pl.BlockSpec((tm, XW), row)),
        compiler_params=_params(("parallel",)),
        name="mem_kv",
    )(mem2, g, w)


def _pool_prompt_kernel(u_ref, w_ref, s_ref, y_ref, nb_ref):
    t_len = u_ref.shape[0]
    t_idx = lax.broadcasted_iota(jnp.int32, (t_len, POOL_G), 0)
    for g, win in enumerate(POOL_WINDOWS):
        cols = slice(g * POOL_G, (g + 1) * POOL_G)
        u = u_ref[:, cols]
        acc = u
        span = 1
        while span < win:
            shifted = pltpu.roll(acc, span, axis=0)
            acc = acc + jnp.where(t_idx >= span, shifted, 0.0)
            span *= 2
        cnt = jnp.minimum(t_idx + 1, win).astype(F32)
        d = (acc / cnt - u).astype(BF16)
        y = _dot(d, w_ref[g]) * s_ref[:, cols]
        y_ref[:, cols] = y.astype(y_ref.dtype)
    nb_ref[0] = u_ref[t_len - POOL_BUF:, :]


def _pool_prompt(u, w_pool, scale, batch, t_len):
    return pl.pallas_call(
        _pool_prompt_kernel,
        out_shape=(jax.ShapeDtypeStruct((batch * t_len, POOL_W), BF16),
                   jax.ShapeDtypeStruct((batch, POOL_BUF, POOL_W), F32)),
        grid=(batch,),
        in_specs=[pl.BlockSpec((t_len, POOL_W), lambda b: (b, 0)),
                  _const_spec((len(POOL_WINDOWS), POOL_G, POOL_G)),
                  _const_spec((1, POOL_W))],
        out_specs=(pl.BlockSpec((t_len, POOL_W), lambda b: (b, 0)),
                   pl.BlockSpec((1, POOL_BUF, POOL_W), lambda b: (b, 0, 0))),
        compiler_params=_params(("parallel",)),
        name="pool_prompt",
    )(u, w_pool, scale)


def _pool_decode_kernel(u_ref, buf_ref, w_ref, s_ref, y_ref, nb_ref):
    u_all = u_ref[...]
    for g, win in enumerate(POOL_WINDOWS):
        cols = slice(g * POOL_G, (g + 1) * POOL_G)
        u = u_all[:, cols]
        acc = u
        for j in range(POOL_BUF - (win - 1), POOL_BUF):
            acc = acc + buf_ref[j, :, cols]
        cnt = float(min(win, PAST_LEN + 1))
        d = (acc / cnt - u).astype(BF16)
        y = _dot(d, w_ref[g]) * s_ref[:, cols]
        y_ref[:, cols] = y.astype(y_ref.dtype)
    for j in range(POOL_BUF - 1):
        nb_ref[j] = buf_ref[j + 1]
    nb_ref[POOL_BUF - 1] = u_all


def _pool_decode(u, buf_t, w_pool, scale):
    n = u.shape[0]
    return pl.pallas_call(
        _pool_decode_kernel,
        out_shape=(jax.ShapeDtypeStruct((n, POOL_W), BF16),
                   jax.ShapeDtypeStruct((POOL_BUF, n, POOL_W), F32)),
        grid=(1,),
        in_specs=[_const_spec((n, POOL_W)), _const_spec((POOL_BUF, n, POOL_W)),
                  _const_spec((len(POOL_WINDOWS), POOL_G, POOL_G)), _const_spec((1, POOL_W))],
        out_specs=(pl.BlockSpec((n, POOL_W), lambda i: (0, 0)),
                   pl.BlockSpec((POOL_BUF, n, POOL_W), lambda i: (0, 0, 0))),
        compiler_params=_params(("arbitrary",)),
        name="pool_decode",
    )(u, buf_t, w_pool, scale)


COL_M, COL_INTER, COL_EINV, COL_END, COL_DECAY = (i * HEADS for i in range(5))


def _scan_lanes(x, op, fill):
    t_len = x.shape[-1]
    lane = lax.broadcasted_iota(jnp.int32, x.shape, 1)
    k = 1
    while k < t_len:
        shifted = pltpu.roll(x, k, axis=1)
        x = op(x, jnp.where(lane >= k, shifted, fill))
        k *= 2
    return x


def _gates_kernel(gift_ref, bif_ref, row_ref, col_ref, mlast_ref, pack_ref):
    t_len = gift_ref.shape[1]
    L = CHUNK
    g = gift_ref[...] + bif_ref[...]
    ig = g[0:HEADS, :]
    lf = _log_sigmoid(g[HEADS:2 * HEADS, :])
    b_cum = _scan_lanes(lf, jnp.add, 0.0)
    a = ig - b_cum
    m_run = jnp.maximum(_scan_lanes(a, jnp.maximum, NEG), 0.0)
    m_tot = b_cum + m_run
    row_ref[0, 0:HEADS, :] = a
    row_ref[0, HEADS:2 * HEADS, :] = m_run
    mlast_ref[0] = m_tot[:, t_len - 1:t_len]
    pack_ref[...] = jnp.zeros(pack_ref.shape, F32)
    pack_ref[COL_M:COL_M + HEADS, :] = m_run
    pack_ref[COL_EINV:COL_EINV + HEADS, :] = jnp.exp(-m_tot)
    for c in range(t_len // L):
        lo, hi = c * L, (c + 1) * L
        m_prev = jnp.zeros((HEADS, 1), F32) if c == 0 else m_run[:, lo - 1:lo]
        m_end = m_run[:, hi - 1:hi]
        pack_ref[COL_INTER:COL_INTER + HEADS, lo:hi] = jnp.exp(m_prev - m_run[:, lo:hi])
        pack_ref[COL_END:COL_END + HEADS, lo:hi] = jnp.exp(a[:, lo:hi] - m_end)
        pack_ref[COL_DECAY:COL_DECAY + HEADS, lo:hi] = jnp.broadcast_to(
            jnp.exp(m_prev - m_end), (HEADS, L))
    for c in range(t_len // 128):
        col_ref[0, c * 128:(c + 1) * 128, :] = pack_ref[:, c * 128:(c + 1) * 128].T


def _gates(gift, bif, batch, t_len):
    return pl.pallas_call(
        _gates_kernel,
        out_shape=(jax.ShapeDtypeStruct((batch, 2 * HEADS, t_len), F32),
                   jax.ShapeDtypeStruct((batch, t_len, 128), F32),
                   jax.ShapeDtypeStruct((batch, HEADS, 1), F32)),
        grid=(batch,),
        in_specs=[pl.BlockSpec((2 * HEADS, t_len), lambda b: (0, b)), _const_spec((2 * HEADS, 1))],
        out_specs=(pl.BlockSpec((1, 2 * HEADS, t_len), lambda b: (b, 0, 0)),
                   pl.BlockSpec((1, t_len, 128), lambda b: (b, 0, 0)),
                   pl.BlockSpec((1, HEADS, 1), lambda b: (b, 0, 0))),
        scratch_shapes=[pltpu.VMEM((128, t_len), F32)],
        compiler_params=_params(("parallel",)),
        name="mlstm_gates",
    )(gift, bif)


def _mlstm_prompt_kernel(q_ref, k_ref, v_ref, og_ref, row_ref, col_ref, y_ref, c_ref, n_ref):
    c_idx = pl.program_id(1)
    L = q_ref.shape[1]

    @pl.when(c_idx == 0)
    def _():
        c_ref[...] = jnp.zeros(c_ref.shape, F32)
        n_ref[...] = jnp.zeros(n_ref.shape, F32)

    t_idx = lax.broadcasted_iota(jnp.int32, (L, L), 0)
    s_idx = lax.broadcasted_iota(jnp.int32, (L, L), 1)
    causal = s_idx <= t_idx
    for b in range(q_ref.shape[0]):
        for h in range(HEADS):
            hs = slice(h * DH, (h + 1) * DH)
            qh = q_ref[b, :, hs]
            kh = k_ref[b, :, hs]
            vh = v_ref[b, :, hs]
            a_row = row_ref[b, h:h + 1, :]
            m_col = col_ref[b, :, COL_M + h:COL_M + h + 1]
            w_inter = col_ref[b, :, COL_INTER + h:COL_INTER + h + 1]
            einv = col_ref[b, :, COL_EINV + h:COL_EINV + h + 1]
            w_end = col_ref[b, :, COL_END + h:COL_END + h + 1]
            decay = col_ref[b, 0:1, COL_DECAY + h:COL_DECAY + h + 1]

            dmat = jnp.exp(jnp.where(causal, a_row - m_col, NEG))
            sw = _dot_nt(qh, kh) * dmat
            c_old = c_ref[b, h]
            n_old = n_ref[b, h]
            inter = _dot_nt(qh, c_old.astype(BF16))
            num = _dot(sw.astype(BF16), vh) + w_inter * inter
            nq = jnp.sum(qh.astype(F32) * n_old, axis=-1, keepdims=True)
            den = jnp.sum(sw, axis=-1, keepdims=True) + w_inter * nq
            r = 1.0 / jnp.maximum(jnp.abs(den), einv)
            y_ref[b, :, hs] = (og_ref[b, :, hs] * (num * r)).astype(y_ref.dtype)

            vw = (vh.astype(F32) * w_end).astype(BF16)
            c_ref[b, h] = decay * c_old + _dot_tn(vw, kh)
            n_ref[b, h] = decay * n_old + jnp.sum(kh.astype(F32) * w_end, axis=0, keepdims=True)


def _mlstm_prompt(q, k, v, og, rowp, colp, batch, t_len):
    L = CHUNK
    nc = t_len // L
    gb = MLSTM_GROUP
    q3, k3, v3, og3 = (a.reshape(batch, t_len, D) for a in (q, k, v, og))
    tok = lambda g, c: (g, c, 0)
    y, c_fin, n_fin = pl.pallas_call(
        _mlstm_prompt_kernel,
        out_shape=(jax.ShapeDtypeStruct((batch, t_len, D), BF16),
                   jax.ShapeDtypeStruct((batch, HEADS, DH, DH), F32),
                   jax.ShapeDtypeStruct((batch, HEADS, 1, DH), F32)),
        grid=(batch // gb, nc),
        in_specs=[pl.BlockSpec((gb, L, D), tok), pl.BlockSpec((gb, L, D), tok),
                  pl.BlockSpec((gb, L, D), tok), pl.BlockSpec((gb, L, D), tok),
                  pl.BlockSpec((gb, 2 * HEADS, L), lambda g, c: (g, 0, c)),
                  pl.BlockSpec((gb, L, 128), tok)],
        out_specs=(pl.BlockSpec((gb, L, D), tok),
                   pl.BlockSpec((gb, HEADS, DH, DH), lambda g, c: (g, 0, 0, 0)),
                   pl.BlockSpec((gb, HEADS, 1, DH), lambda g, c: (g, 0, 0, 0))),
        compiler_params=_params(("parallel", "arbitrary")),
        name="mlstm_prompt",
    )(q3, k3, v3, og3, rowp, colp)
    return y.reshape(batch * t_len, D), c_fin, n_fin


def _mlstm_decode_kernel(q_ref, k_ref, v_ref, og_ref, gif_ref, bif_ref, c0_ref, n0_ref, m0_ref,
                         y_ref, c_out_ref, n_out_ref, m_out_ref):
    bb = q_ref.shape[0]
    row0 = lax.broadcasted_iota(jnp.int32, (8, DH), 0) == 0

    def body(b, carry):
        gi = gif_ref[b] + bif_ref[...]
        m0 = m0_ref[b]
        for h in range(HEADS):
            hs = slice(h * DH, (h + 1) * DH)
            qr = q_ref[b, :, hs]
            kr = k_ref[b, :, hs]
            vr = v_ref[b, :, hs]
            ig = gi[:, h:h + 1]
            lf = _log_sigmoid(gi[:, HEADS + h:HEADS + h + 1])
            m_old = m0[:, h:h + 1]
            m_new = jnp.maximum(lf + m_old, ig)
            w_i = jnp.exp(ig - m_new)
            w_f = jnp.exp(lf + m_old - m_new)
            c_old = c0_ref[b, h]
            n_old = n0_ref[b, h:h + 1, :]
            q8 = jnp.broadcast_to(qr, (8, DH)).astype(BF16)
            cq = _dot_nt(q8, c_old.astype(BF16))[0:1, :]
            qk = jnp.sum(qr * kr, axis=-1, keepdims=True)
            nq = jnp.sum(n_old * qr, axis=-1, keepdims=True)
            num = (w_i * qk) * vr + w_f * cq
            den = w_i * qk + w_f * nq
            hval = num / jnp.maximum(jnp.abs(den), jnp.exp(-m_new))
            y_ref[b, :, hs] = (og_ref[b, :, hs] * hval).astype(y_ref.dtype)
            v8 = jnp.where(row0, jnp.broadcast_to(vr, (8, DH)), 0.0).astype(BF16)
            k8 = jnp.where(row0, jnp.broadcast_to(kr, (8, DH)), 0.0).astype(BF16)
            c_out_ref[b, h] = w_f * c_old + w_i * _dot_tn(v8, k8)
            n_out_ref[b, h:h + 1, :] = w_f * n_old + w_i * kr
            m_out_ref[b, :, h:h + 1] = m_new
        return carry

    lax.fori_loop(0, bb, body, 0)


def _mlstm_decode(q, k, v, og, gifc, bif_row, c0, n0, m0):
    n = q.shape[0]
    bb = DEC_BLOCK
    r3 = lambda i: (i, 0, 0)
    q3, k3, v3, og3 = (a.reshape(n, 1, D) for a in (q, k, v, og))
    gif3 = gifc.reshape(n, 1, 2 * HEADS)
    m03 = m0.reshape(n, 1, HEADS)
    y, c_new, n_new, m_new = pl.pallas_call(
        _mlstm_decode_kernel,
        out_shape=(jax.ShapeDtypeStruct((n, 1, D), BF16),
                   jax.ShapeDtypeStruct((n, HEADS, DH, DH), F32),
                   jax.ShapeDtypeStruct((n, HEADS, DH), F32),
                   jax.ShapeDtypeStruct((n, 1, HEADS), F32)),
        grid=(n // bb,),
        in_specs=[pl.BlockSpec((bb, 1, D), r3), pl.BlockSpec((bb, 1, D), r3),
                  pl.BlockSpec((bb, 1, D), r3), pl.BlockSpec((bb, 1, D), r3),
                  pl.BlockSpec((bb, 1, 2 * HEADS), r3), _const_spec((1, 2 * HEADS)),
                  pl.BlockSpec((bb, HEADS, DH, DH), lambda i: (i, 0, 0, 0)),
                  pl.BlockSpec((bb, HEADS, DH), r3),
                  pl.BlockSpec((bb, 1, HEADS), r3)],
        out_specs=(pl.BlockSpec((bb, 1, D), r3),
                   pl.BlockSpec((bb, HEADS, DH, DH), lambda i: (i, 0, 0, 0)),
                   pl.BlockSpec((bb, HEADS, DH), r3),
                   pl.BlockSpec((bb, 1, HEADS), r3)),
        compiler_params=_params(("parallel",)),
        name="mlstm_decode",
    )(q3, k3, v3, og3, gif3, bif_row, c0, n0, m03)
    return y.reshape(n, D), c_new, n_new, m_new.reshape(n, HEADS)


def _xattn_prompt_kernel(xq_ref, mk_ref, mv_ref, y_ref):
    scale = XDH ** -0.5
    for h in range(XH):
        hs = slice(h * XDH, (h + 1) * XDH)
        s = _dot_nt(xq_ref[:, hs], mk_ref[:, hs].astype(BF16)) * scale
        p = jnp.exp(s - jnp.max(s, axis=-1, keepdims=True))
        l = jnp.sum(p, axis=-1, keepdims=True)
        o = _dot(p.astype(BF16), mv_ref[:, hs].astype(BF16)) / l
        y_ref[:, hs] = o.astype(y_ref.dtype)


def _xattn_prompt(xq, mk, mv, batch, t_len):
    tq = TM_PROMPT
    nq = t_len // tq
    return pl.pallas_call(
        _xattn_prompt_kernel,
        out_shape=jax.ShapeDtypeStruct((batch * t_len, XW), BF16),
        grid=(batch, nq),
        in_specs=[pl.BlockSpec((tq, XW), lambda b, i: (b * nq + i, 0)),
                  pl.BlockSpec((N_MEM, XW), lambda b, i: (b, 0)),
                  pl.BlockSpec((N_MEM, XW), lambda b, i: (b, 0))],
        out_specs=pl.BlockSpec((tq, XW), lambda b, i: (b * nq + i, 0)),
        compiler_params=_params(("parallel", "arbitrary")),
        name="xattn_prompt",
    )(xq, mk, mv)


def _xattn_decode_kernel(xq_ref, mk_ref, mv_ref, y_ref):
    scale = XDH ** -0.5
    q = xq_ref[...][:, None, :, :]
    s = jnp.sum(mk_ref[...] * q, axis=-1, keepdims=True) * scale
    p = jnp.exp(s - jnp.max(s, axis=1, keepdims=True))
    l = jnp.sum(p, axis=1)
    o = jnp.sum(p * mv_ref[...], axis=1) / l
    y_ref[...] = o.astype(y_ref.dtype)


def _xattn_decode(xq, mk, mv):
    n = xq.shape[0]
    bb = DEC_BLOCK
    xq4 = xq.reshape(n, XH, XDH)
    y = pl.pallas_call(
        _xattn_decode_kernel,
        out_shape=jax.ShapeDtypeStruct((n, XH, XDH), BF16),
        grid=(n // bb,),
        in_specs=[pl.BlockSpec((bb, XH, XDH), lambda i: (i, 0, 0)),
                  pl.BlockSpec((bb, N_MEM, XH, XDH), lambda i: (i, 0, 0, 0)),
                  pl.BlockSpec((bb, N_MEM, XH, XDH), lambda i: (i, 0, 0, 0))],
        out_specs=pl.BlockSpec((bb, XH, XDH), lambda i: (i, 0, 0)),
        compiler_params=_params(("parallel",)),
        name="xattn_decode",
    )(xq4, mk, mv)
    return y.reshape(n, XW)


def _merge_kernel(x_ref, yp_ref, ym_ref, yx_ref, gpre_ref, gpost_ref, wg_ref, wbp_ref, wbm_ref,
                  wbx_ref, wo_ref, o_ref, merged_ref):
    x = x_ref[...]
    h = _rms(x, gpre_ref[...]).astype(BF16)
    yp = yp_ref[...]
    ym = ym_ref[...]
    yx = yx_ref[...]
    nchunk = 256
    for c in range(D // nchunk):
        cs = slice(c * nchunk, (c + 1) * nchunk)
        acc = _sigmoid(_dot(h, wg_ref[:, c * nchunk:(c + 1) * nchunk])) * _dot(yp, wbp_ref[:, cs])
        acc += (_sigmoid(_dot(h, wg_ref[:, D + c * nchunk:D + (c + 1) * nchunk]))
                * _dot(ym, wbm_ref[:, cs]))
        acc += (_sigmoid(_dot(h, wg_ref[:, 2 * D + c * nchunk:2 * D + (c + 1) * nchunk]))
                * _dot(yx, wbx_ref[:, cs]))
        merged_ref[:, cs] = acc.astype(BF16)
    o_ref[...] = x + _rms(_dot(merged_ref[...], wo_ref[...]), gpost_ref[...])


def _merge(x2, yp, ym, yx, gpre, gpost, wg, wbp, wbm, wbx, wo, tm):
    n = x2.shape[0]
    row = lambda i: (i, 0)
    return pl.pallas_call(
        _merge_kernel,
        out_shape=jax.ShapeDtypeStruct((n, D), F32),
        grid=(n // tm,),
        in_specs=[pl.BlockSpec((tm, D), row), pl.BlockSpec((tm, POOL_W), row),
                  pl.BlockSpec((tm, D), row), pl.BlockSpec((tm, XW), row),
                  _const_spec((1, D)), _const_spec((1, D)), _const_spec((D, 3 * D)),
                  _const_spec((POOL_W, D)), _const_spec((D, D)), _const_spec((XW, D)),
                  _const_spec((D, D))],
        out_specs=pl.BlockSpec((tm, D), row),
        scratch_shapes=[pltpu.VMEM((tm, D), BF16)],
        compiler_params=_params(("parallel",)),
        name="merge_out",
    )(x2, yp, ym, yx, gpre, gpost, wg, wbp, wbm, wbx, wo)


def _ffn_kernel(x_ref, gpre_ref, gpost_ref, w1_ref, w2_ref, o_ref):
    x = x_ref[...]
    h = _rms(x, gpre_ref[...]).astype(BF16)
    fchunk = 1024
    acc = None
    for c in range(D_FF // fchunk):
        a = _dot(h, w1_ref[:, c * fchunk:(c + 1) * fchunk])
        a = jnp.square(jnp.maximum(a, 0.0)).astype(BF16)
        part = _dot(a, w2_ref[c * fchunk:(c + 1) * fchunk, :])
        acc = part if acc is None else acc + part
    o_ref[...] = x + _rms(acc, gpost_ref[...])


def _ffn(x2, gpre, gpost, w1, w2, tm):
    n = x2.shape[0]
    row = lambda i: (i, 0)
    return pl.pallas_call(
        _ffn_kernel,
        out_shape=jax.ShapeDtypeStruct((n, D), F32),
        grid=(n // tm,),
        in_specs=[pl.BlockSpec((tm, D), row), _const_spec((1, D)), _const_spec((1, D)),
                  _const_spec((D, D_FF)), _const_spec((D_FF, D))],
        out_specs=pl.BlockSpec((tm, D), row),
        compiler_params=_params(("parallel",)),
        name="ffn",
    )(x2, gpre, gpost, w1, w2)


def kernel(x_prompt, x_sample, mem_prompt, state_pool_buf, state_mlstm_C, state_mlstm_n, state_mlstm_m, cache_mem_k, cache_mem_v, g_pre_mix, w_in, b_if, w_pool, pool_scale, g_mem, w_mem_kv, w_br_pool, w_br_mlstm, w_br_xattn, w_out, g_post_mix, g_pre_mlp, w_ff1, w_ff2, g_post_mlp):
    batch, t_len, _ = x_prompt.shape
    n_dec = x_sample.shape[0]
    assert w_in.shape[0] == 1, "single layer"

    w_in0 = w_in[0]
    w_main = w_in0[:, :O_GIF].astype(BF16)
    w_gif = w_in0[:, O_GIF:O_XQ].T.astype(BF16)
    w_xq = w_in0[:, O_XQ:O_GATE].astype(BF16)
    w_gate = w_in0[:, O_GATE:].astype(BF16)
    w_pool_b = w_pool[0].astype(BF16)
    w_kv_b = w_mem_kv[0].astype(BF16)
    wbp, wbm, wbx = (w[0].astype(BF16) for w in (w_br_pool, w_br_mlstm, w_br_xattn))
    wo = w_out[0].astype(BF16)
    w1 = w_ff1[0].astype(BF16)
    w2 = w_ff2[0].astype(BF16)
    g_mix, g_pm, g_mlp, g_pmlp, g_m = (g[0].reshape(1, D) for g in
                                       (g_pre_mix, g_post_mix, g_pre_mlp, g_post_mlp, g_mem))
    scale = pool_scale[0].reshape(1, POOL_W)
    bif = b_if[0]

    xp = x_prompt.reshape(batch * t_len, D)
    mk_p, mv_p = _mem_kv(mem_prompt.reshape(batch * N_MEM, D), g_m, w_kv_b)
    u, q, k, v, og, xq, gift, _ = _in_proj(xp, g_mix, w_main, w_xq, w_gif, TM_PROMPT, BF16)
    y_pool, buf_p = _pool_prompt(u, w_pool_b, scale, batch, t_len)
    rowp, colp, m_last = _gates(gift, bif.reshape(2 * HEADS, 1), batch, t_len)
    y_ml, c_p, n_p = _mlstm_prompt(q, k, v, og, rowp, colp, batch, t_len)
    y_x = _xattn_prompt(xq, mk_p, mv_p, batch, t_len)
    x1 = _merge(xp, y_pool, y_ml, y_x, g_mix, g_pm, w_gate, wbp, wbm, wbx, wo, TM_PROMPT)
    yp = _ffn(x1, g_mlp, g_pmlp, w1, w2, TM_PROMPT)

    xs = x_sample.reshape(n_dec, D)
    u_s, q_s, k_s, v_s, og_s, xq_s, _, gifc_s = _in_proj(xs, g_mix, w_main, w_xq, w_gif, n_dec, F32)
    buf_t = jnp.transpose(state_pool_buf[0], (1, 0, 2))
    ypool_s, buf_s_t = _pool_decode(u_s, buf_t, w_pool_b, scale)
    yml_s, c_s, n_s, m_s = _mlstm_decode(q_s, k_s, v_s, og_s, gifc_s, bif.reshape(1, 2 * HEADS),
                                         state_mlstm_C[0], state_mlstm_n[0], state_mlstm_m[0])
    yx_s = _xattn_decode(xq_s, cache_mem_k[0], cache_mem_v[0])
    x1_s = _merge(xs, ypool_s, yml_s, yx_s, g_mix, g_pm, w_gate, wbp, wbm, wbx, wo, n_dec)
    ys = _ffn(x1_s, g_mlp, g_pmlp, w1, w2, n_dec)

    return (yp.reshape(batch, t_len, D), ys.reshape(n_dec, 1, D),
            buf_p[None], c_p[None], n_p.reshape(1, batch, HEADS, DH),
            m_last.reshape(1, batch, HEADS),
            mk_p.reshape(1, batch, N_MEM, XH, XDH), mv_p.reshape(1, batch, N_MEM, XH, XDH),
            jnp.transpose(buf_s_t, (1, 0, 2))[None], c_s[None], n_s[None], m_s[None])
```

```python
import jax
import jax.numpy as jnp
from jax import lax
from jax.experimental import pallas as pl
from jax.experimental.pallas import tpu as pltpu

F32 = jnp.float32
BF16 = jnp.bfloat16

D = 1024
POOL_W = 512
POOL_G = 128
POOL_WINDOWS = (2, 4, 8, 16)
POOL_BUF = 15
HEADS = 4
DH = 256
XH = 4
XDH = 128
XW = 512
N_MEM = 256
D_FF = 4096
EPS = 1e-6
NEG = -1e30
PAST_LEN = 16384

TM_PROMPT = 1024
CHUNK = 256
MLSTM_GROUP = 4
DEC_BLOCK = 8

VMEM_LIMIT = 56 * 1024 * 1024

O_QKVO = POOL_W
O_GIF = O_QKVO + 4 * D
O_XQ = O_GIF + 2 * HEADS
O_GATE = O_XQ + XW


def _params(sem):
    return pltpu.CompilerParams(dimension_semantics=sem, vmem_limit_bytes=VMEM_LIMIT)


def _const_spec(shape):
    nd = len(shape)
    return pl.BlockSpec(shape, lambda *_: (0,) * nd, pipeline_mode=pl.Buffered(1))


def _rms(x, g):
    ms = jnp.mean(x * x, axis=-1, keepdims=True)
    return x * lax.rsqrt(ms + EPS) * g


def _log_sigmoid(x):
    return jnp.minimum(x, 0.0) - jnp.log(1.0 + jnp.exp(-jnp.abs(x)))


def _sigmoid(x):
    return 1.0 / (1.0 + jnp.exp(-x))


def _dot(a, b):
    return jnp.dot(a, b, preferred_element_type=F32)


def _dot_nt(a, b):
    return lax.dot_general(a, b, (((1,), (1,)), ((), ())), preferred_element_type=F32)


def _dot_tn(a, b):
    return lax.dot_general(a, b, (((0,), (0,)), ((), ())), preferred_element_type=F32)


def _in_proj_kernel(x_ref, g_ref, w_ref, wxq_ref, wgif_ref, u_ref, q_ref, k_ref, v_ref, og_ref,
                    xq_ref, gift_ref, gifc_ref):
    h = _rms(x_ref[...], g_ref[...]).astype(BF16)

    def seg(lo, n):
        return _dot(h, w_ref[:, lo:lo + n])

    u_ref[...] = seg(0, POOL_W)
    q_ref[...] = seg(POOL_W, D).astype(q_ref.dtype)
    k_ref[...] = (seg(POOL_W + D, D) * (DH ** -0.5)).astype(k_ref.dtype)
    v_ref[...] = seg(POOL_W + 2 * D, D).astype(v_ref.dtype)
    og_ref[...] = _sigmoid(seg(POOL_W + 3 * D, D))
    xq_lo = O_XQ - O_GIF
    xq_ref[...] = _dot(h, wxq_ref[:, xq_lo:xq_lo + XW]).astype(xq_ref.dtype)
    wg = wgif_ref[...]
    gift_ref[...] = _dot_nt(wg, h)
    gifc_ref[...] = _dot_nt(h, wg)


XQ_WINDOW = 768


def _in_proj(x2, g, w_in_b, w_gif, tm, qkv_dtype):
    n = x2.shape[0]
    row = lambda i: (i, 0)
    assert O_GIF % XQ_WINDOW == 0 and O_GATE - O_GIF <= XQ_WINDOW
    out_shape = (
        jax.ShapeDtypeStruct((n, POOL_W), F32),
        jax.ShapeDtypeStruct((n, D), qkv_dtype),
        jax.ShapeDtypeStruct((n, D), qkv_dtype),
        jax.ShapeDtypeStruct((n, D), qkv_dtype),
        jax.ShapeDtypeStruct((n, D), F32),
        jax.ShapeDtypeStruct((n, XW), qkv_dtype),
        jax.ShapeDtypeStruct((2 * HEADS, n), F32),
        jax.ShapeDtypeStruct((n, 2 * HEADS), F32),
    )
    return pl.pallas_call(
        _in_proj_kernel,
        out_shape=out_shape,
        grid=(n // tm,),
        in_specs=[
            pl.BlockSpec((tm, D), row),
            _const_spec((1, D)),
            _const_spec((D, O_GIF)),
            pl.BlockSpec((D, XQ_WINDOW), lambda i: (0, O_GIF // XQ_WINDOW),
                         pipeline_mode=pl.Buffered(1)),
            _const_spec((2 * HEADS, D)),
        ],
        out_specs=(
            pl.BlockSpec((tm, POOL_W), row),
            pl.BlockSpec((tm, D), row),
            pl.BlockSpec((tm, D), row),
            pl.BlockSpec((tm, D), row),
            pl.BlockSpec((tm, D), row),
            pl.BlockSpec((tm, XW), row),
            pl.BlockSpec((2 * HEADS, tm), lambda i: (0, i)),
            pl.BlockSpec((tm, 2 * HEADS), row),
        ),
        compiler_params=_params(("parallel",)),
        name="in_proj",
    )(x2, g, w_in_b, w_in_b, w_gif)


def _mem_kv_kernel(mem_ref, g_ref, w_ref, k_ref, v_ref):
    h = _rms(mem_ref[...], g_ref[...]).astype(BF16)
    kv = _dot(h, w_ref[...])
    k_ref[...] = kv[:, :XW]
    v_ref[...] = kv[:, XW:]


def _mem_kv(mem2, g, w):
    n = mem2.shape[0]
    tm = TM_PROMPT
    row = lambda i: (i, 0)
    return pl.pallas_call(
        _mem_kv_kernel,
        out_shape=(jax.ShapeDtypeStruct((n, XW), F32), jax.ShapeDtypeStruct((n, XW), F32)),
        grid=(n // tm,),
        in_specs=[pl.BlockSpec((tm, D), row), _const_spec((1, D)), _const_spec((D, 2 * XW))],
        out_specs=(pl.BlockSpec((tm, XW), row), pl.BlockSpec((tm, XW), row)),
        compiler_params=_params(("parallel",)),
        name="mem_kv",
    )(mem2, g, w)


def _pool_prompt_kernel(u_ref, w_ref, s_ref, y_ref, nb_ref):
    t_len = u_ref.shape[0]
    t_idx = lax.broadcasted_iota(jnp.int32, (t_len, POOL_G), 0)
    for g, win in enumerate(POOL_WINDOWS):
        cols = slice(g * POOL_G, (g + 1) * POOL_G)
        u = u_ref[:, cols]
        acc = u
        span = 1
        while span < win:
            shifted = pltpu.roll(acc, span, axis=0)
            acc = acc + jnp.where(t_idx >= span, shifted, 0.0)
            span *= 2
        cnt = jnp.minimum(t_idx + 1, win).astype(F32)
        d = (acc / cnt - u).astype(BF16)
        y = _dot(d, w_ref[g]) * s_ref[:, cols]
        y_ref[:, cols] = y.astype(y_ref.dtype)
    nb_ref[0] = u_ref[t_len - POOL_BUF:, :]


def _pool_prompt(u, w_pool, scale, batch, t_len):
    return pl.pallas_call(
        _pool_prompt_kernel,
        out_shape=(jax.ShapeDtypeStruct((batch * t_len, POOL_W), BF16),
                   jax.ShapeDtypeStruct((batch, POOL_BUF, POOL_W), F32)),
        grid=(batch,),
        in_specs=[pl.BlockSpec((t_len, POOL_W), lambda b: (b, 0)),
                  _const_spec((len(POOL_WINDOWS), POOL_G, POOL_G)),
                  _const_spec((1, POOL_W))],
        out_specs=(pl.BlockSpec((t_len, POOL_W), lambda b: (b, 0)),
                   pl.BlockSpec((1, POOL_BUF, POOL_W), lambda b: (b, 0, 0))),
        compiler_params=_params(("parallel",)),
        name="pool_prompt",
    )(u, w_pool, scale)


def _pool_decode_kernel(u_ref, buf_ref, w_ref, s_ref, y_ref, nb_ref):
    u_all = u_ref[...]
    for g, win in enumerate(POOL_WINDOWS):
        cols = slice(g * POOL_G, (g + 1) * POOL_G)
        u = u_all[:, cols]
        acc = u
        for j in range(POOL_BUF - (win - 1), POOL_BUF):
            acc = acc + buf_ref[j, :, cols]
        cnt = float(min(win, PAST_LEN + 1))
        d = (acc / cnt - u).astype(BF16)
        y = _dot(d, w_ref[g]) * s_ref[:, cols]
        y_ref[:, cols] = y.astype(y_ref.dtype)
    for j in range(POOL_BUF - 1):
        nb_ref[j] = buf_ref[j + 1]
    nb_ref[POOL_BUF - 1] = u_all


def _pool_decode(u, buf_t, w_pool, scale):
    n = u.shape[0]
    return pl.pallas_call(
        _pool_decode_kernel,
        out_shape=(jax.ShapeDtypeStruct((n, POOL_W), BF16),
                   jax.ShapeDtypeStruct((POOL_BUF, n, POOL_W), F32)),
        grid=(1,),
        in_specs=[_const_spec((n, POOL_W)), _const_spec((POOL_BUF, n, POOL_W)),
                  _const_spec((len(POOL_WINDOWS), POOL_G, POOL_G)), _const_spec((1, POOL_W))],
        out_specs=(pl.BlockSpec((n, POOL_W), lambda i: (0, 0)),
                   pl.BlockSpec((POOL_BUF, n, POOL_W), lambda i: (0, 0, 0))),
        compiler_params=_params(("arbitrary",)),
        name="pool_decode",
    )(u, buf_t, w_pool, scale)


COL_M, COL_INTER, COL_EINV, COL_END, COL_DECAY = (i * HEADS for i in range(5))


def _scan_lanes(x, op, fill):
    t_len = x.shape[-1]
    lane = lax.broadcasted_iota(jnp.int32, x.shape, 1)
    k = 1
    while k < t_len:
        shifted = pltpu.roll(x, k, axis=1)
        x = op(x, jnp.where(lane >= k, shifted, fill))
        k *= 2
    return x


def _gates_kernel(gift_ref, bif_ref, row_ref, col_ref, mlast_ref, pack_ref):
    t_len = gift_ref.shape[1]
    L = CHUNK
    g = gift_ref[...] + bif_ref[...]
    ig = g[0:HEADS, :]
    lf = _log_sigmoid(g[HEADS:2 * HEADS, :])
    b_cum = _scan_lanes(lf, jnp.add, 0.0)
    a = ig - b_cum
    m_run = jnp.maximum(_scan_lanes(a, jnp.maximum, NEG), 0.0)
    m_tot = b_cum + m_run
    row_ref[0, 0:HEADS, :] = a
    row_ref[0, HEADS:2 * HEADS, :] = m_run
    mlast_ref[0] = m_tot[:, t_len - 1:t_len]
    pack_ref[...] = jnp.zeros(pack_ref.shape, F32)
    pack_ref[COL_M:COL_M + HEADS, :] = m_run
    pack_ref[COL_EINV:COL_EINV + HEADS, :] = jnp.exp(-m_tot)
    for c in range(t_len // L):
        lo, hi = c * L, (c + 1) * L
        m_prev = jnp.zeros((HEADS, 1), F32) if c == 0 else m_run[:, lo - 1:lo]
        m_end = m_run[:, hi - 1:hi]
        pack_ref[COL_INTER:COL_INTER + HEADS, lo:hi] = jnp.exp(m_prev - m_run[:, lo:hi])
        pack_ref[COL_END:COL_END + HEADS, lo:hi] = jnp.exp(a[:, lo:hi] - m_end)
        pack_ref[COL_DECAY:COL_DECAY + HEADS, lo:hi] = jnp.broadcast_to(
            jnp.exp(m_prev - m_end), (HEADS, L))
    for c in range(t_len // 128):
        col_ref[0, c * 128:(c + 1) * 128, :] = pack_ref[:, c * 128:(c + 1) * 128].T


def _gates(gift, bif, batch, t_len):
    return pl.pallas_call(
        _gates_kernel,
        out_shape=(jax.ShapeDtypeStruct((batch, 2 * HEADS, t_len), F32),
                   jax.ShapeDtypeStruct((batch, t_len, 128), F32),
                   jax.ShapeDtypeStruct((batch, HEADS, 1), F32)),
        grid=(batch,),
        in_specs=[pl.BlockSpec((2 * HEADS, t_len), lambda b: (0, b)), _const_spec((2 * HEADS, 1))],
        out_specs=(pl.BlockSpec((1, 2 * HEADS, t_len), lambda b: (b, 0, 0)),
                   pl.BlockSpec((1, t_len, 128), lambda b: (b, 0, 0)),
                   pl.BlockSpec((1, HEADS, 1), lambda b: (b, 0, 0))),
        scratch_shapes=[pltpu.VMEM((128, t_len), F32)],
        compiler_params=_params(("parallel",)),
        name="mlstm_gates",
    )(gift, bif)


def _mlstm_prompt_kernel(q_ref, k_ref, v_ref, og_ref, row_ref, col_ref, y_ref, c_ref, n_ref):
    c_idx = pl.program_id(1)
    L = q_ref.shape[1]

    @pl.when(c_idx == 0)
    def _():
        c_ref[...] = jnp.zeros(c_ref.shape, F32)
        n_ref[...] = jnp.zeros(n_ref.shape, F32)

    t_idx = lax.broadcasted_iota(jnp.int32, (L, L), 0)
    s_idx = lax.broadcasted_iota(jnp.int32, (L, L), 1)
    causal = s_idx <= t_idx
    for b in range(q_ref.shape[0]):
        for h in range(HEADS):
            hs = slice(h * DH, (h + 1) * DH)
            qh = q_ref[b, :, hs]
            kh = k_ref[b, :, hs]
            vh = v_ref[b, :, hs]
            a_row = row_ref[b, h:h + 1, :]
            m_col = col_ref[b, :, COL_M + h:COL_M + h + 1]
            w_inter = col_ref[b, :, COL_INTER + h:COL_INTER + h + 1]
            einv = col_ref[b, :, COL_EINV + h:COL_EINV + h + 1]
            w_end = col_ref[b, :, COL_END + h:COL_END + h + 1]
            decay = col_ref[b, 0:1, COL_DECAY + h:COL_DECAY + h + 1]

            dmat = jnp.exp(jnp.where(causal, a_row - m_col, NEG))
            sw = _dot_nt(qh, kh) * dmat
            c_old = c_ref[b, h]
            n_old = n_ref[b, h]
            inter = _dot_nt(qh, c_old.astype(BF16))
            num = _dot(sw.astype(BF16), vh) + w_inter * inter
            nq = jnp.sum(qh.astype(F32) * n_old, axis=-1, keepdims=True)
            den = jnp.sum(sw, axis=-1, keepdims=True) + w_inter * nq
            r = 1.0 / jnp.maximum(jnp.abs(den), einv)
            y_ref[b, :, hs] = (og_ref[b, :, hs] * (num * r)).astype(y_ref.dtype)

            vw = (vh.astype(F32) * w_end).astype(BF16)
            c_ref[b, h] = decay * c_old + _dot_tn(vw, kh)
            n_ref[b, h] = decay * n_old + jnp.sum(kh.astype(F32) * w_end, axis=0, keepdims=True)


def _mlstm_prompt(q, k, v, og, rowp, colp, batch, t_len):
    L = CHUNK
    nc = t_len // L
    gb = MLSTM_GROUP
    q3, k3, v3, og3 = (a.reshape(batch, t_len, D) for a in (q, k, v, og))
    tok = lambda g, c: (g, c, 0)
    y, c_fin, n_fin = pl.pallas_call(
        _mlstm_prompt_kernel,
        out_shape=(jax.ShapeDtypeStruct((batch, t_len, D), BF16),
                   jax.ShapeDtypeStruct((batch, HEADS, DH, DH), F32),
                   jax.ShapeDtypeStruct((batch, HEADS, 1, DH), F32)),
        grid=(batch // gb, nc),
        in_specs=[pl.BlockSpec((gb, L, D), tok), pl.BlockSpec((gb, L, D), tok),
                  pl.BlockSpec((gb, L, D), tok), pl.BlockSpec((gb, L, D), tok),
                  pl.BlockSpec((gb, 2 * HEADS, L), lambda g, c: (g, 0, c)),
                  pl.BlockSpec((gb, L, 128), tok)],
        out_specs=(pl.BlockSpec((gb, L, D), tok),
                   pl.BlockSpec((gb, HEADS, DH, DH), lambda g, c: (g, 0, 0, 0)),
                   pl.BlockSpec((gb, HEADS, 1, DH), lambda g, c: (g, 0, 0, 0))),
        compiler_params=_params(("parallel", "arbitrary")),
        name="mlstm_prompt",
    )(q3, k3, v3, og3, rowp, colp)
    return y.reshape(batch * t_len, D), c_fin, n_fin


def _mlstm_decode_kernel(q_ref, k_ref, v_ref, og_ref, gif_ref, bif_ref, c0_ref, n0_ref, m0_ref,
                         y_ref, c_out_ref, n_out_ref, m_out_ref):
    row0 = lax.broadcasted_iota(jnp.int32, (8, DH), 0) == 0
    for j in range(q_ref.shape[0]):
        gi = gif_ref[j] + bif_ref[...]
        m0 = m0_ref[j]
        for h in range(HEADS):
            hs = slice(h * DH, (h + 1) * DH)
            qr = q_ref[j, :, hs]
            kr = k_ref[j, :, hs]
            vr = v_ref[j, :, hs]
            ig = gi[:, h:h + 1]
            lf = _log_sigmoid(gi[:, HEADS + h:HEADS + h + 1])
            m_old = m0[:, h:h + 1]
            m_new = jnp.maximum(lf + m_old, ig)
            w_i = jnp.exp(ig - m_new)
            w_f = jnp.exp(lf + m_old - m_new)
            c_old = c0_ref[j, h]
            n_old = n0_ref[j, h:h + 1, :]
            q8 = jnp.broadcast_to(qr, (8, DH)).astype(BF16)
            cq = _dot_nt(q8, c_old.astype(BF16))[0:1, :]
            qk = jnp.sum(qr * kr, axis=-1, keepdims=True)
            nq = jnp.sum(n_old * qr, axis=-1, keepdims=True)
            num = (w_i * qk) * vr + w_f * cq
            den = w_i * qk + w_f * nq
            hval = num / jnp.maximum(jnp.abs(den), jnp.exp(-m_new))
            y_ref[j, :, hs] = (og_ref[j, :, hs] * hval).astype(y_ref.dtype)
            v8 = jnp.where(row0, jnp.broadcast_to(vr, (8, DH)), 0.0).astype(BF16)
            k8 = jnp.where(row0, jnp.broadcast_to(kr, (8, DH)), 0.0).astype(BF16)
            c_out_ref[j, h] = w_f * c_old + w_i * _dot_tn(v8, k8)
            n_out_ref[j, h:h + 1, :] = w_f * n_old + w_i * kr
            m_out_ref[j, :, h:h + 1] = m_new


def _mlstm_decode(q, k, v, og, gifc, bif_row, c0, n0, m0):
    n = q.shape[0]
    bb = DEC_BLOCK
    r3 = lambda i: (i, 0, 0)
    q3, k3, v3, og3 = (a.reshape(n, 1, D) for a in (q, k, v, og))
    gif3 = gifc.reshape(n, 1, 2 * HEADS)
    m03 = m0.reshape(n, 1, HEADS)
    y, c_new, n_new, m_new = pl.pallas_call(
        _mlstm_decode_kernel,
        out_shape=(jax.ShapeDtypeStruct((n, 1, D), BF16),
                   jax.ShapeDtypeStruct((n, HEADS, DH, DH), F32),
                   jax.ShapeDtypeStruct((n, HEADS, DH), F32),
                   jax.ShapeDtypeStruct((n, 1, HEADS), F32)),
        grid=(n // bb,),
        in_specs=[pl.BlockSpec((bb, 1, D), r3), pl.BlockSpec((bb, 1, D), r3),
                  pl.BlockSpec((bb, 1, D), r3), pl.BlockSpec((bb, 1, D), r3),
                  pl.BlockSpec((bb, 1, 2 * HEADS), r3), _const_spec((1, 2 * HEADS)),
                  pl.BlockSpec((bb, HEADS, DH, DH), lambda i: (i, 0, 0, 0)),
                  pl.BlockSpec((bb, HEADS, DH), r3),
                  pl.BlockSpec((bb, 1, HEADS), r3)],
        out_specs=(pl.BlockSpec((bb, 1, D), r3),
                   pl.BlockSpec((bb, HEADS, DH, DH), lambda i: (i, 0, 0, 0)),
                   pl.BlockSpec((bb, HEADS, DH), r3),
                   pl.BlockSpec((bb, 1, HEADS), r3)),
        compiler_params=_params(("parallel",)),
        name="mlstm_decode",
    )(q3, k3, v3, og3, gif3, bif_row, c0, n0, m03)
    return y.reshape(n, D), c_new, n_new, m_new.reshape(n, HEADS)


def _xattn_prompt_kernel(xq_ref, mk_ref, mv_ref, y_ref):
    scale = XDH ** -0.5
    for h in range(XH):
        hs = slice(h * XDH, (h + 1) * XDH)
        s = _dot_nt(xq_ref[:, hs], mk_ref[:, hs].astype(BF16)) * scale
        p = jnp.exp(s - jnp.max(s, axis=-1, keepdims=True))
        l = jnp.sum(p, axis=-1, keepdims=True)
        o = _dot(p.astype(BF16), mv_ref[:, hs].astype(BF16)) / l
        y_ref[:, hs] = o.astype(y_ref.dtype)


def _xattn_prompt(xq, mk, mv, batch, t_len):
    tq = TM_PROMPT
    nq = t_len // tq
    return pl.pallas_call(
        _xattn_prompt_kernel,
        out_shape=jax.ShapeDtypeStruct((batch * t_len, XW), BF16),
        grid=(batch, nq),
        in_specs=[pl.BlockSpec((tq, XW), lambda b, i: (b * nq + i, 0)),
                  pl.BlockSpec((N_MEM, XW), lambda b, i: (b, 0)),
                  pl.BlockSpec((N_MEM, XW), lambda b, i: (b, 0))],
        out_specs=pl.BlockSpec((tq, XW), lambda b, i: (b * nq + i, 0)),
        compiler_params=_params(("parallel", "arbitrary")),
        name="xattn_prompt",
    )(xq, mk, mv)


def _xattn_decode_kernel(q_ref, mk_ref, mv_ref, y_ref):
    scale = XDH ** -0.5
    q = q_ref[...][:, None, :, :]
    s = jnp.sum(mk_ref[...] * q, axis=-1, keepdims=True) * scale
    mx = jnp.max(s, axis=1, keepdims=True)
    mx = jnp.maximum(mx, pltpu.roll(mx, XH, axis=2))
    p = jnp.exp(s - mx)
    l = jnp.sum(p, axis=1, keepdims=True)
    l = l + pltpu.roll(l, XH, axis=2)
    o = jnp.sum(p * mv_ref[...], axis=1, keepdims=True)
    o = (o + pltpu.roll(o, XH, axis=2)) / l
    y_ref[...] = o[:, 0, 0:XH, :].astype(y_ref.dtype)


def _xattn_decode(xq, mk, mv):
    n = xq.shape[0]
    bb = DEC_BLOCK
    xq4 = xq.reshape(n, XH, XDH)
    q2 = jnp.concatenate([xq4, xq4], axis=1)
    mk2, mv2 = (a.reshape(n, N_MEM // 2, 2 * XH, XDH) for a in (mk, mv))
    kv_spec = pl.BlockSpec((bb, N_MEM // 2, 2 * XH, XDH), lambda i: (i, 0, 0, 0))
    y = pl.pallas_call(
        _xattn_decode_kernel,
        out_shape=jax.ShapeDtypeStruct((n, XH, XDH), BF16),
        grid=(n // bb,),
        in_specs=[pl.BlockSpec((bb, 2 * XH, XDH), lambda i: (i, 0, 0)), kv_spec, kv_spec],
        out_specs=pl.BlockSpec((bb, XH, XDH), lambda i: (i, 0, 0)),
        compiler_params=_params(("parallel",)),
        name="xattn_decode",
    )(q2, mk2, mv2)
    return y.reshape(n, XW)


def _merge_kernel(x_ref, yp_ref, ym_ref, yx_ref, gpre_ref, gpost_ref, wg_ref, wbp_ref, wbm_ref,
                  wbx_ref, wo_ref, o_ref, merged_ref):
    x = x_ref[...]
    h = _rms(x, gpre_ref[...]).astype(BF16)
    yp = yp_ref[...]
    ym = ym_ref[...]
    yx = yx_ref[...]
    nchunk = 256
    for c in range(D // nchunk):
        cs = slice(c * nchunk, (c + 1) * nchunk)
        acc = _sigmoid(_dot(h, wg_ref[:, c * nchunk:(c + 1) * nchunk])) * _dot(yp, wbp_ref[:, cs])
        acc += (_sigmoid(_dot(h, wg_ref[:, D + c * nchunk:D + (c + 1) * nchunk]))
                * _dot(ym, wbm_ref[:, cs]))
        acc += (_sigmoid(_dot(h, wg_ref[:, 2 * D + c * nchunk:2 * D + (c + 1) * nchunk]))
                * _dot(yx, wbx_ref[:, cs]))
        merged_ref[:, cs] = acc.astype(BF16)
    o_ref[...] = x + _rms(_dot(merged_ref[...], wo_ref[...]), gpost_ref[...])


def _merge(x2, yp, ym, yx, gpre, gpost, wg, wbp, wbm, wbx, wo, tm):
    n = x2.shape[0]
    row = lambda i: (i, 0)
    return pl.pallas_call(
        _merge_kernel,
        out_shape=jax.ShapeDtypeStruct((n, D), F32),
        grid=(n // tm,),
        in_specs=[pl.BlockSpec((tm, D), row), pl.BlockSpec((tm, POOL_W), row),
                  pl.BlockSpec((tm, D), row), pl.BlockSpec((tm, XW), row),
                  _const_spec((1, D)), _const_spec((1, D)), _const_spec((D, 3 * D)),
                  _const_spec((POOL_W, D)), _const_spec((D, D)), _const_spec((XW, D)),
                  _const_spec((D, D))],
        out_specs=pl.BlockSpec((tm, D), row),
        scratch_shapes=[pltpu.VMEM((tm, D), BF16)],
        compiler_params=_params(("parallel",)),
        name="merge_out",
    )(x2, yp, ym, yx, gpre, gpost, wg, wbp, wbm, wbx, wo)


def _ffn_kernel(x_ref, gpre_ref, gpost_ref, w1_ref, w2_ref, o_ref):
    x = x_ref[...]
    h = _rms(x, gpre_ref[...]).astype(BF16)
    fchunk = 1024
    acc = None
    for c in range(D_FF // fchunk):
        a = _dot(h, w1_ref[:, c * fchunk:(c + 1) * fchunk])
        a = jnp.square(jnp.maximum(a, 0.0)).astype(BF16)
        part = _dot(a, w2_ref[c * fchunk:(c + 1) * fchunk, :])
        acc = part if acc is None else acc + part
    o_ref[...] = x + _rms(acc, gpost_ref[...])


def _ffn(x2, gpre, gpost, w1, w2, tm):
    n = x2.shape[0]
    row = lambda i: (i, 0)
    return pl.pallas_call(
        _ffn_kernel,
        out_shape=jax.ShapeDtypeStruct((n, D), F32),
        grid=(n // tm,),
        in_specs=[pl.BlockSpec((tm, D), row), _const_spec((1, D)), _const_spec((1, D)),
                  _const_spec((D, D_FF)), _const_spec((D_FF, D))],
        out_specs=pl.BlockSpec((tm, D), row),
        compiler_params=_params(("parallel",)),
        name="ffn",
    )(x2, gpre, gpost, w1, w2)


def kernel(x_prompt, x_sample, mem_prompt, state_pool_buf, state_mlstm_C, state_mlstm_n, state_mlstm_m, cache_mem_k, cache_mem_v, g_pre_mix, w_in, b_if, w_pool, pool_scale, g_mem, w_mem_kv, w_br_pool, w_br_mlstm, w_br_xattn, w_out, g_post_mix, g_pre_mlp, w_ff1, w_ff2, g_post_mlp):
    batch, t_len, _ = x_prompt.shape
    n_dec = x_sample.shape[0]
    assert w_in.shape[0] == 1, "single layer"

    w_in0 = w_in[0]
    w_in_b = w_in0.astype(BF16)
    w_gif = w_in0[:, O_GIF:O_XQ].T.astype(BF16)
    w_gate = w_in_b[:, O_GATE:]
    w_pool_b = w_pool[0].astype(BF16)
    w_kv_b = w_mem_kv[0].astype(BF16)
    wbp, wbm, wbx = (w[0].astype(BF16) for w in (w_br_pool, w_br_mlstm, w_br_xattn))
    wo = w_out[0].astype(BF16)
    w1 = w_ff1[0].astype(BF16)
    w2 = w_ff2[0].astype(BF16)
    g_mix, g_pm, g_mlp, g_pmlp, g_m = (g[0].reshape(1, D) for g in
                                       (g_pre_mix, g_post_mix, g_pre_mlp, g_post_mlp, g_mem))
    scale = pool_scale[0].reshape(1, POOL_W)
    bif = b_if[0]

    xp = x_prompt.reshape(batch * t_len, D)
    mk_p, mv_p = _mem_kv(mem_prompt.reshape(batch * N_MEM, D), g_m, w_kv_b)
    u, q, k, v, og, xq, gift, _ = _in_proj(xp, g_mix, w_in_b, w_gif, TM_PROMPT, BF16)
    y_pool, buf_p = _pool_prompt(u, w_pool_b, scale, batch, t_len)
    rowp, colp, m_last = _gates(gift, bif.reshape(2 * HEADS, 1), batch, t_len)
    y_ml, c_p, n_p = _mlstm_prompt(q, k, v, og, rowp, colp, batch, t_len)
    y_x = _xattn_prompt(xq, mk_p, mv_p, batch, t_len)
    x1 = _merge(xp, y_pool, y_ml, y_x, g_mix, g_pm, w_gate, wbp, wbm, wbx, wo, TM_PROMPT)
    yp = _ffn(x1, g_mlp, g_pmlp, w1, w2, TM_PROMPT)

    xs = x_sample.reshape(n_dec, D)
    u_s, q_s, k_s, v_s, og_s, xq_s, _, gifc_s = _in_proj(xs, g_mix, w_in_b, w_gif, n_dec, F32)
    buf_t = jnp.transpose(state_pool_buf[0], (1, 0, 2))
    ypool_s, buf_s_t = _pool_decode(u_s, buf_t, w_pool_b, scale)
    yml_s, c_s, n_s, m_s = _mlstm_decode(q_s, k_s, v_s, og_s, gifc_s, bif.reshape(1, 2 * HEADS),
                                         state_mlstm_C[0], state_mlstm_n[0], state_mlstm_m[0])
    yx_s = _xattn_decode(xq_s, cache_mem_k[0], cache_mem_v[0])
    x1_s = _merge(xs, ypool_s, yml_s, yx_s, g_mix, g_pm, w_gate, wbp, wbm, wbx, wo, n_dec)
    ys = _ffn(x1_s, g_mlp, g_pmlp, w1, w2, n_dec)

    return (yp.reshape(batch, t_len, D), ys.reshape(n_dec, 1, D),
            buf_p[None], c_p[None], n_p.reshape(1, batch, HEADS, DH),
            m_last.reshape(1, batch, HEADS),
            mk_p.reshape(1, batch, N_MEM, XH, XDH), mv_p.reshape(1, batch, N_MEM, XH, XDH),
            jnp.transpose(buf_s_t, (1, 0, 2))[None], c_s[None], n_s[None], m_s[None])
```

```python
import jax
import jax.numpy as jnp
from jax import lax
from jax.experimental import pallas as pl
from jax.experimental.pallas import tpu as pltpu

F32 = jnp.float32
BF16 = jnp.bfloat16

D = 1024
POOL_W = 512
POOL_G = 128
POOL_WINDOWS = (2, 4, 8, 16)
POOL_BUF = 15
HEADS = 4
DH = 256
XH = 4
XDH = 128
XW = 512
N_MEM = 256
D_FF = 4096
EPS = 1e-6
NEG = -1e30
PAST_LEN = 16384

TM_PROMPT = 1024
TM_MERGE = 512
CHUNK = 256
MLSTM_GROUP = 4
DEC_BLOCK = 8

VMEM_LIMIT = 56 * 1024 * 1024

O_QKVO = POOL_W
O_GIF = O_QKVO + 4 * D
O_XQ = O_GIF + 2 * HEADS
O_GATE = O_XQ + XW


def _params(sem):
    return pltpu.CompilerParams(dimension_semantics=sem, vmem_limit_bytes=VMEM_LIMIT)


def _const_spec(shape):
    nd = len(shape)
    return pl.BlockSpec(shape, lambda *_: (0,) * nd, pipeline_mode=pl.Buffered(1))


def _rms(x, g):
    ms = jnp.mean(x * x, axis=-1, keepdims=True)
    return x * lax.rsqrt(ms + EPS) * g


def _log_sigmoid(x):
    return jnp.minimum(x, 0.0) - jnp.log(1.0 + jnp.exp(-jnp.abs(x)))


def _sigmoid(x):
    return 1.0 / (1.0 + jnp.exp(-x))


def _dot(a, b):
    return jnp.dot(a, b, preferred_element_type=F32)


def _dot_nt(a, b):
    return lax.dot_general(a, b, (((1,), (1,)), ((), ())), preferred_element_type=F32)


def _dot_tn(a, b):
    return lax.dot_general(a, b, (((0,), (0,)), ((), ())), preferred_element_type=F32)


def _in_proj_kernel(x_ref, g_ref, w_ref, wxq_ref, wgif_ref, u_ref, q_ref, k_ref, v_ref, og_ref,
                    xq_ref, gift_ref, gifc_ref):
    h = _rms(x_ref[...], g_ref[...]).astype(BF16)

    def seg(lo, n):
        return _dot(h, w_ref[:, lo:lo + n])

    u_ref[...] = seg(0, POOL_W)
    q_ref[...] = seg(POOL_W, D).astype(q_ref.dtype)
    k_ref[...] = (seg(POOL_W + D, D) * (DH ** -0.5)).astype(k_ref.dtype)
    v_ref[...] = seg(POOL_W + 2 * D, D).astype(v_ref.dtype)
    og_ref[...] = _sigmoid(seg(POOL_W + 3 * D, D))
    xq_lo = O_XQ - O_GIF
    xq_ref[...] = _dot(h, wxq_ref[:, xq_lo:xq_lo + XW]).astype(xq_ref.dtype)
    wg = wgif_ref[...]
    gift_ref[...] = _dot_nt(wg, h)
    gifc_ref[...] = _dot_nt(h, wg)


XQ_WINDOW = 768


def _in_proj(x2, g, w_in_b, w_gif, tm, qkv_dtype):
    n = x2.shape[0]
    row = lambda i: (i, 0)
    assert O_GIF % XQ_WINDOW == 0 and O_GATE - O_GIF <= XQ_WINDOW
    out_shape = (
        jax.ShapeDtypeStruct((n, POOL_W), F32),
        jax.ShapeDtypeStruct((n, D), qkv_dtype),
        jax.ShapeDtypeStruct((n, D), qkv_dtype),
        jax.ShapeDtypeStruct((n, D), qkv_dtype),
        jax.ShapeDtypeStruct((n, D), F32),
        jax.ShapeDtypeStruct((n, XW), qkv_dtype),
        jax.ShapeDtypeStruct((2 * HEADS, n), F32),
        jax.ShapeDtypeStruct((n, 2 * HEADS), F32),
    )
    return pl.pallas_call(
        _in_proj_kernel,
        out_shape=out_shape,
        grid=(n // tm,),
        in_specs=[
            pl.BlockSpec((tm, D), row),
            _const_spec((1, D)),
            _const_spec((D, O_GIF)),
            pl.BlockSpec((D, XQ_WINDOW), lambda i: (0, O_GIF // XQ_WINDOW),
                         pipeline_mode=pl.Buffered(1)),
            _const_spec((2 * HEADS, D)),
        ],
        out_specs=(
            pl.BlockSpec((tm, POOL_W), row),
            pl.BlockSpec((tm, D), row),
            pl.BlockSpec((tm, D), row),
            pl.BlockSpec((tm, D), row),
            pl.BlockSpec((tm, D), row),
            pl.BlockSpec((tm, XW), row),
            pl.BlockSpec((2 * HEADS, tm), lambda i: (0, i)),
            pl.BlockSpec((tm, 2 * HEADS), row),
        ),
        compiler_params=_params(("parallel",)),
        name="in_proj",
    )(x2, g, w_in_b, w_in_b, w_gif)


def _mem_kv_kernel(mem_ref, g_ref, w_ref, k_ref, v_ref):
    h = _rms(mem_ref[...], g_ref[...]).astype(BF16)
    kv = _dot(h, w_ref[...])
    k_ref[...] = kv[:, :XW]
    v_ref[...] = kv[:, XW:]


def _mem_kv(mem2, g, w):
    n = mem2.shape[0]
    tm = TM_PROMPT
    row = lambda i: (i, 0)
    return pl.pallas_call(
        _mem_kv_kernel,
        out_shape=(jax.ShapeDtypeStruct((n, XW), F32), jax.ShapeDtypeStruct((n, XW), F32)),
        grid=(n // tm,),
        in_specs=[pl.BlockSpec((tm, D), row), _const_spec((1, D)), _const_spec((D, 2 * XW))],
        out_specs=(pl.BlockSpec((tm, XW), row), pl.BlockSpec((tm, XW), row)),
        compiler_params=_params(("parallel",)),
        name="mem_kv",
    )(mem2, g, w)


def _pool_prompt_kernel(u_ref, w_ref, s_ref, y_ref, nb_ref):
    t_len = u_ref.shape[0]
    t_idx = lax.broadcasted_iota(jnp.int32, (t_len, POOL_G), 0)
    for g, win in enumerate(POOL_WINDOWS):
        cols = slice(g * POOL_G, (g + 1) * POOL_G)
        u = u_ref[:, cols]
        acc = u
        span = 1
        while span < win:
            shifted = pltpu.roll(acc, span, axis=0)
            acc = acc + jnp.where(t_idx >= span, shifted, 0.0)
            span *= 2
        cnt = jnp.minimum(t_idx + 1, win).astype(F32)
        d = (acc / cnt - u).astype(BF16)
        y = _dot(d, w_ref[g]) * s_ref[:, cols]
        y_ref[:, cols] = y.astype(y_ref.dtype)
    nb_ref[0] = u_ref[t_len - POOL_BUF:, :]


def _pool_prompt(u, w_pool, scale, batch, t_len):
    return pl.pallas_call(
        _pool_prompt_kernel,
        out_shape=(jax.ShapeDtypeStruct((batch * t_len, POOL_W), BF16),
                   jax.ShapeDtypeStruct((batch, POOL_BUF, POOL_W), F32)),
        grid=(batch,),
        in_specs=[pl.BlockSpec((t_len, POOL_W), lambda b: (b, 0)),
                  _const_spec((len(POOL_WINDOWS), POOL_G, POOL_G)),
                  _const_spec((1, POOL_W))],
        out_specs=(pl.BlockSpec((t_len, POOL_W), lambda b: (b, 0)),
                   pl.BlockSpec((1, POOL_BUF, POOL_W), lambda b: (b, 0, 0))),
        compiler_params=_params(("parallel",)),
        name="pool_prompt",
    )(u, w_pool, scale)


def _pool_decode_kernel(u_ref, buf_ref, w_ref, s_ref, y_ref, nb_ref):
    u_all = u_ref[...]
    for g, win in enumerate(POOL_WINDOWS):
        cols = slice(g * POOL_G, (g + 1) * POOL_G)
        u = u_all[:, cols]
        acc = u
        for j in range(POOL_BUF - (win - 1), POOL_BUF):
            acc = acc + buf_ref[j, :, cols]
        cnt = float(min(win, PAST_LEN + 1))
        d = (acc / cnt - u).astype(BF16)
        y = _dot(d, w_ref[g]) * s_ref[:, cols]
        y_ref[:, cols] = y.astype(y_ref.dtype)
    for j in range(POOL_BUF - 1):
        nb_ref[j] = buf_ref[j + 1]
    nb_ref[POOL_BUF - 1] = u_all


def _pool_decode(u, buf_t, w_pool, scale):
    n = u.shape[0]
    return pl.pallas_call(
        _pool_decode_kernel,
        out_shape=(jax.ShapeDtypeStruct((n, POOL_W), BF16),
                   jax.ShapeDtypeStruct((POOL_BUF, n, POOL_W), F32)),
        grid=(1,),
        in_specs=[_const_spec((n, POOL_W)), _const_spec((POOL_BUF, n, POOL_W)),
                  _const_spec((len(POOL_WINDOWS), POOL_G, POOL_G)), _const_spec((1, POOL_W))],
        out_specs=(pl.BlockSpec((n, POOL_W), lambda i: (0, 0)),
                   pl.BlockSpec((POOL_BUF, n, POOL_W), lambda i: (0, 0, 0))),
        compiler_params=_params(("arbitrary",)),
        name="pool_decode",
    )(u, buf_t, w_pool, scale)


COL_M, COL_INTER, COL_EINV, COL_END, COL_DECAY = (i * HEADS for i in range(5))


def _scan_lanes(x, op, fill):
    t_len = x.shape[-1]
    lane = lax.broadcasted_iota(jnp.int32, x.shape, 1)
    k = 1
    while k < t_len:
        shifted = pltpu.roll(x, k, axis=1)
        x = op(x, jnp.where(lane >= k, shifted, fill))
        k *= 2
    return x


def _gates_kernel(gift_ref, bif_ref, row_ref, col_ref, mlast_ref, pack_ref):
    t_len = gift_ref.shape[1]
    L = CHUNK
    g = gift_ref[...] + bif_ref[...]
    ig = g[0:HEADS, :]
    lf = _log_sigmoid(g[HEADS:2 * HEADS, :])
    b_cum = _scan_lanes(lf, jnp.add, 0.0)
    a = ig - b_cum
    m_run = jnp.maximum(_scan_lanes(a, jnp.maximum, NEG), 0.0)
    m_tot = b_cum + m_run
    row_ref[0, 0:HEADS, :] = a
    row_ref[0, HEADS:2 * HEADS, :] = m_run
    mlast_ref[0] = m_tot[:, t_len - 1:t_len]
    pack_ref[...] = jnp.zeros(pack_ref.shape, F32)
    pack_ref[COL_M:COL_M + HEADS, :] = m_run
    pack_ref[COL_EINV:COL_EINV + HEADS, :] = jnp.exp(-m_tot)
    for c in range(t_len // L):
        lo, hi = c * L, (c + 1) * L
        m_prev = jnp.zeros((HEADS, 1), F32) if c == 0 else m_run[:, lo - 1:lo]
        m_end = m_run[:, hi - 1:hi]
        pack_ref[COL_INTER:COL_INTER + HEADS, lo:hi] = jnp.exp(m_prev - m_run[:, lo:hi])
        pack_ref[COL_END:COL_END + HEADS, lo:hi] = jnp.exp(a[:, lo:hi] - m_end)
        pack_ref[COL_DECAY:COL_DECAY + HEADS, lo:hi] = jnp.broadcast_to(
            jnp.exp(m_prev - m_end), (HEADS, L))
    for c in range(t_len // 128):
        col_ref[0, c * 128:(c + 1) * 128, :] = pack_ref[:, c * 128:(c + 1) * 128].T


def _gates(gift, bif, batch, t_len):
    return pl.pallas_call(
        _gates_kernel,
        out_shape=(jax.ShapeDtypeStruct((batch, 2 * HEADS, t_len), F32),
                   jax.ShapeDtypeStruct((batch, t_len, 128), F32),
                   jax.ShapeDtypeStruct((batch, HEADS, 1), F32)),
        grid=(batch,),
        in_specs=[pl.BlockSpec((2 * HEADS, t_len), lambda b: (0, b)), _const_spec((2 * HEADS, 1))],
        out_specs=(pl.BlockSpec((1, 2 * HEADS, t_len), lambda b: (b, 0, 0)),
                   pl.BlockSpec((1, t_len, 128), lambda b: (b, 0, 0)),
                   pl.BlockSpec((1, HEADS, 1), lambda b: (b, 0, 0))),
        scratch_shapes=[pltpu.VMEM((128, t_len), F32)],
        compiler_params=_params(("parallel",)),
        name="mlstm_gates",
    )(gift, bif)


def _mlstm_prompt_kernel(q_ref, k_ref, v_ref, og_ref, row_ref, col_ref, y_ref, c_ref, n_ref):
    c_idx = pl.program_id(1)
    L = q_ref.shape[1]

    @pl.when(c_idx == 0)
    def _():
        c_ref[...] = jnp.zeros(c_ref.shape, F32)
        n_ref[...] = jnp.zeros(n_ref.shape, F32)

    t_idx = lax.broadcasted_iota(jnp.int32, (L, L), 0)
    s_idx = lax.broadcasted_iota(jnp.int32, (L, L), 1)
    causal = s_idx <= t_idx
    for b in range(q_ref.shape[0]):
        for h in range(HEADS):
            hs = slice(h * DH, (h + 1) * DH)
            qh = q_ref[b, :, hs]
            kh = k_ref[b, :, hs]
            vh = v_ref[b, :, hs]
            a_row = row_ref[b, h:h + 1, :]
            m_col = col_ref[b, :, COL_M + h:COL_M + h + 1]
            w_inter = col_ref[b, :, COL_INTER + h:COL_INTER + h + 1]
            einv = col_ref[b, :, COL_EINV + h:COL_EINV + h + 1]
            w_end = col_ref[b, :, COL_END + h:COL_END + h + 1]
            decay = col_ref[b, 0:1, COL_DECAY + h:COL_DECAY + h + 1]

            dmat = jnp.exp(jnp.where(causal, a_row - m_col, NEG))
            sw = _dot_nt(qh, kh) * dmat
            c_old = c_ref[b, h]
            n_old = n_ref[b, h]
            inter = _dot_nt(qh, c_old.astype(BF16))
            num = _dot(sw.astype(BF16), vh) + w_inter * inter
            nq = jnp.sum(qh.astype(F32) * n_old, axis=-1, keepdims=True)
            den = jnp.sum(sw, axis=-1, keepdims=True) + w_inter * nq
            r = 1.0 / jnp.maximum(jnp.abs(den), einv)
            y_ref[b, :, hs] = (og_ref[b, :, hs] * (num * r)).astype(y_ref.dtype)

            vw = (vh.astype(F32) * w_end).astype(BF16)
            c_ref[b, h] = decay * c_old + _dot_tn(vw, kh)
            n_ref[b, h] = decay * n_old + jnp.sum(kh.astype(F32) * w_end, axis=0, keepdims=True)


def _mlstm_prompt(q, k, v, og, rowp, colp, batch, t_len):
    L = CHUNK
    nc = t_len // L
    gb = MLSTM_GROUP
    q3, k3, v3, og3 = (a.reshape(batch, t_len, D) for a in (q, k, v, og))
    tok = lambda g, c: (g, c, 0)
    y, c_fin, n_fin = pl.pallas_call(
        _mlstm_prompt_kernel,
        out_shape=(jax.ShapeDtypeStruct((batch, t_len, D), BF16),
                   jax.ShapeDtypeStruct((batch, HEADS, DH, DH), F32),
                   jax.ShapeDtypeStruct((batch, HEADS, 1, DH), F32)),
        grid=(batch // gb, nc),
        in_specs=[pl.BlockSpec((gb, L, D), tok), pl.BlockSpec((gb, L, D), tok),
                  pl.BlockSpec((gb, L, D), tok), pl.BlockSpec((gb, L, D), tok),
                  pl.BlockSpec((gb, 2 * HEADS, L), lambda g, c: (g, 0, c)),
                  pl.BlockSpec((gb, L, 128), tok)],
        out_specs=(pl.BlockSpec((gb, L, D), tok),
                   pl.BlockSpec((gb, HEADS, DH, DH), lambda g, c: (g, 0, 0, 0)),
                   pl.BlockSpec((gb, HEADS, 1, DH), lambda g, c: (g, 0, 0, 0))),
        compiler_params=_params(("parallel", "arbitrary")),
        name="mlstm_prompt",
    )(q3, k3, v3, og3, rowp, colp)
    return y.reshape(batch * t_len, D), c_fin, n_fin


def _mlstm_decode_body(q_ref, k_ref, vt_ref, ogt_ref, gif_ref, bif_ref, c0_ref, n0_ref, m0_ref,
                       ht_ref, c_out_ref, n_out_ref, m_out_ref):
    for j in range(q_ref.shape[0]):
        gi = gif_ref[j] + bif_ref[...]
        m0 = m0_ref[j]
        for h in range(HEADS):
            hs = slice(h * DH, (h + 1) * DH)
            qr = q_ref[j, :, hs]
            kr = k_ref[j, :, hs]
            vc = vt_ref[0, hs, j:j + 1]
            ig = gi[:, h:h + 1]
            lf = _log_sigmoid(gi[:, HEADS + h:HEADS + h + 1])
            m_old = m0[:, h:h + 1]
            m_new = jnp.maximum(lf + m_old, ig)
            w_i = jnp.exp(ig - m_new)
            w_f = jnp.exp(lf + m_old - m_new)
            c_old = c0_ref[j, h]
            n_old = n0_ref[j, h:h + 1, :]
            cq = jnp.sum(c_old * qr, axis=-1, keepdims=True)
            qk = jnp.sum(qr * kr, axis=-1, keepdims=True)
            nq = jnp.sum(n_old * qr, axis=-1, keepdims=True)
            num = (w_i * qk) * vc + w_f * cq
            den = w_i * qk + w_f * nq
            hcol = num / jnp.maximum(jnp.abs(den), jnp.exp(-m_new))
            ht_ref[0, hs, j:j + 1] = ogt_ref[0, hs, j:j + 1] * hcol
            c_out_ref[j, h] = w_f * c_old + (w_i * vc) * kr
            n_out_ref[j, h:h + 1, :] = w_f * n_old + w_i * kr
            m_out_ref[j, :, h:h + 1] = m_new


def _mlstm_decode_operands(q, k, v, og, gifc, bif_row, c0, n0, m0, bb):
    n = q.shape[0]
    r3 = lambda i: (i, 0, 0)
    r4 = lambda i: (i, 0, 0, 0)
    q3, k3 = (a.reshape(n, 1, D) for a in (q, k))
    vt, ogt = (jnp.transpose(a.reshape(n // bb, bb, D), (0, 2, 1)) for a in (v, og))
    args = (q3, k3, vt, ogt, gifc.reshape(n, 1, 2 * HEADS), bif_row, c0, n0,
            m0.reshape(n, 1, HEADS))
    in_specs = [pl.BlockSpec((bb, 1, D), r3), pl.BlockSpec((bb, 1, D), r3),
                pl.BlockSpec((1, D, bb), r3), pl.BlockSpec((1, D, bb), r3),
                pl.BlockSpec((bb, 1, 2 * HEADS), r3), _const_spec((1, 2 * HEADS)),
                pl.BlockSpec((bb, HEADS, DH, DH), r4), pl.BlockSpec((bb, HEADS, DH), r3),
                pl.BlockSpec((bb, 1, HEADS), r3)]
    out_shape = (jax.ShapeDtypeStruct((n // bb, D, bb), F32),
                 jax.ShapeDtypeStruct((n, HEADS, DH, DH), F32),
                 jax.ShapeDtypeStruct((n, HEADS, DH), F32),
                 jax.ShapeDtypeStruct((n, 1, HEADS), F32))
    out_specs = (pl.BlockSpec((1, D, bb), r3), pl.BlockSpec((bb, HEADS, DH, DH), r4),
                 pl.BlockSpec((bb, HEADS, DH), r3), pl.BlockSpec((bb, 1, HEADS), r3))
    return args, in_specs, out_shape, out_specs


def _mlstm_decode_results(ht, c_new, n_new, m_new):
    nblk, _, bb = ht.shape
    y = jnp.transpose(ht, (0, 2, 1)).reshape(nblk * bb, D).astype(BF16)
    return y, c_new, n_new, m_new.reshape(nblk * bb, HEADS)


def _xattn_prompt_kernel(xq_ref, mk_ref, mv_ref, y_ref):
    scale = XDH ** -0.5
    for h in range(XH):
        hs = slice(h * XDH, (h + 1) * XDH)
        s = _dot_nt(xq_ref[:, hs], mk_ref[:, hs].astype(BF16)) * scale
        p = jnp.exp(s - jnp.max(s, axis=-1, keepdims=True))
        l = jnp.sum(p, axis=-1, keepdims=True)
        o = _dot(p.astype(BF16), mv_ref[:, hs].astype(BF16)) / l
        y_ref[:, hs] = o.astype(y_ref.dtype)


def _xattn_prompt(xq, mk, mv, batch, t_len):
    tq = TM_PROMPT
    nq = t_len // tq
    return pl.pallas_call(
        _xattn_prompt_kernel,
        out_shape=jax.ShapeDtypeStruct((batch * t_len, XW), BF16),
        grid=(batch, nq),
        in_specs=[pl.BlockSpec((tq, XW), lambda b, i: (b * nq + i, 0)),
                  pl.BlockSpec((N_MEM, XW), lambda b, i: (b, 0)),
                  pl.BlockSpec((N_MEM, XW), lambda b, i: (b, 0))],
        out_specs=pl.BlockSpec((tq, XW), lambda b, i: (b * nq + i, 0)),
        compiler_params=_params(("parallel", "arbitrary")),
        name="xattn_prompt",
    )(xq, mk, mv)


def _xattn_decode_kernel(q_ref, mk_ref, mv_ref, y_ref):
    scale = XDH ** -0.5
    q = q_ref[...][:, None, :, :]
    s = jnp.sum(mk_ref[...] * q, axis=-1, keepdims=True) * scale
    mx = jnp.max(s, axis=1, keepdims=True)
    mx = jnp.maximum(mx, pltpu.roll(mx, XH, axis=2))
    p = jnp.exp(s - mx)
    l = jnp.sum(p, axis=1, keepdims=True)
    l = l + pltpu.roll(l, XH, axis=2)
    o = jnp.sum(p * mv_ref[...], axis=1, keepdims=True)
    o = (o + pltpu.roll(o, XH, axis=2)) / l
    y_ref[...] = o[:, 0, 0:XH, :].astype(y_ref.dtype)


def _xattn_decode(xq, mk, mv):
    n = xq.shape[0]
    bb = DEC_BLOCK
    xq4 = xq.reshape(n, XH, XDH)
    q2 = jnp.concatenate([xq4, xq4], axis=1)
    mk2, mv2 = (a.reshape(n, N_MEM // 2, 2 * XH, XDH) for a in (mk, mv))
    kv_spec = pl.BlockSpec((bb, N_MEM // 2, 2 * XH, XDH), lambda i: (i, 0, 0, 0))
    y = pl.pallas_call(
        _xattn_decode_kernel,
        out_shape=jax.ShapeDtypeStruct((n, XH, XDH), BF16),
        grid=(n // bb,),
        in_specs=[pl.BlockSpec((bb, 2 * XH, XDH), lambda i: (i, 0, 0)), kv_spec, kv_spec],
        out_specs=pl.BlockSpec((bb, XH, XDH), lambda i: (i, 0, 0)),
        compiler_params=_params(("parallel",)),
        name="xattn_decode",
    )(q2, mk2, mv2)
    return y.reshape(n, XW)


N_MERGE_IN = 11
N_DEC_IN = 9


def _merge_dec_kernel(*refs):
    host_in = refs[:N_MERGE_IN]
    dec_in = refs[N_MERGE_IN:N_MERGE_IN + N_DEC_IN]
    o_ref = refs[N_MERGE_IN + N_DEC_IN]
    dec_out = refs[N_MERGE_IN + N_DEC_IN + 1:-1]
    _mlstm_decode_body(*dec_in, *dec_out)
    _merge_kernel(*host_in, o_ref, refs[-1])


def _merge_kernel(x_ref, yp_ref, ym_ref, yx_ref, gpre_ref, gpost_ref, wg_ref, wbp_ref, wbm_ref,
                  wbx_ref, wo_ref, o_ref, merged_ref):
    x = x_ref[...]
    h = _rms(x, gpre_ref[...]).astype(BF16)
    yp = yp_ref[...]
    ym = ym_ref[...]
    yx = yx_ref[...]
    nchunk = 256
    for c in range(D // nchunk):
        cs = slice(c * nchunk, (c + 1) * nchunk)
        acc = _sigmoid(_dot(h, wg_ref[:, c * nchunk:(c + 1) * nchunk])) * _dot(yp, wbp_ref[:, cs])
        acc += (_sigmoid(_dot(h, wg_ref[:, D + c * nchunk:D + (c + 1) * nchunk]))
                * _dot(ym, wbm_ref[:, cs]))
        acc += (_sigmoid(_dot(h, wg_ref[:, 2 * D + c * nchunk:2 * D + (c + 1) * nchunk]))
                * _dot(yx, wbx_ref[:, cs]))
        merged_ref[:, cs] = acc.astype(BF16)
    o_ref[...] = x + _rms(_dot(merged_ref[...], wo_ref[...]), gpost_ref[...])


def _merge(x2, yp, ym, yx, gpre, gpost, wg, wbp, wbm, wbx, wo, tm, dec=None):
    n = x2.shape[0]
    steps = n // tm
    row = lambda i: (i, 0)
    args = (x2, yp, ym, yx, gpre, gpost, wg, wbp, wbm, wbx, wo)
    in_specs = [pl.BlockSpec((tm, D), row), pl.BlockSpec((tm, POOL_W), row),
                pl.BlockSpec((tm, D), row), pl.BlockSpec((tm, XW), row),
                _const_spec((1, D)), _const_spec((1, D)), _const_spec((D, 3 * D)),
                _const_spec((POOL_W, D)), _const_spec((D, D)), _const_spec((XW, D)),
                _const_spec((D, D))]
    assert len(args) == N_MERGE_IN
    out_shape = jax.ShapeDtypeStruct((n, D), F32)
    out_specs = pl.BlockSpec((tm, D), row)
    body = _merge_kernel
    if dec is not None:
        n_dec = dec[0].shape[0]
        d_args, d_in, d_shape, d_out = _mlstm_decode_operands(*dec, n_dec // steps)
        assert len(d_args) == N_DEC_IN
        args, in_specs = args + d_args, in_specs + d_in
        out_shape, out_specs = (out_shape,) + d_shape, (out_specs,) + d_out
        body = _merge_dec_kernel
    return pl.pallas_call(
        body,
        out_shape=out_shape,
        grid=(steps,),
        in_specs=in_specs,
        out_specs=out_specs,
        scratch_shapes=[pltpu.VMEM((tm, D), BF16)],
        compiler_params=_params(("parallel",)),
        name="merge_out",
    )(*args)


def _ffn_kernel(x_ref, gpre_ref, gpost_ref, w1_ref, w2_ref, o_ref):
    x = x_ref[...]
    h = _rms(x, gpre_ref[...]).astype(BF16)
    fchunk = 1024
    acc = None
    for c in range(D_FF // fchunk):
        a = _dot(h, w1_ref[:, c * fchunk:(c + 1) * fchunk])
        a = jnp.square(jnp.maximum(a, 0.0)).astype(BF16)
        part = _dot(a, w2_ref[c * fchunk:(c + 1) * fchunk, :])
        acc = part if acc is None else acc + part
    o_ref[...] = x + _rms(acc, gpost_ref[...])


def _ffn(x2, gpre, gpost, w1, w2, tm):
    n = x2.shape[0]
    row = lambda i: (i, 0)
    return pl.pallas_call(
        _ffn_kernel,
        out_shape=jax.ShapeDtypeStruct((n, D), F32),
        grid=(n // tm,),
        in_specs=[pl.BlockSpec((tm, D), row), _const_spec((1, D)), _const_spec((1, D)),
                  _const_spec((D, D_FF)), _const_spec((D_FF, D))],
        out_specs=pl.BlockSpec((tm, D), row),
        compiler_params=_params(("parallel",)),
        name="ffn",
    )(x2, gpre, gpost, w1, w2)


def kernel(x_prompt, x_sample, mem_prompt, state_pool_buf, state_mlstm_C, state_mlstm_n, state_mlstm_m, cache_mem_k, cache_mem_v, g_pre_mix, w_in, b_if, w_pool, pool_scale, g_mem, w_mem_kv, w_br_pool, w_br_mlstm, w_br_xattn, w_out, g_post_mix, g_pre_mlp, w_ff1, w_ff2, g_post_mlp):
    batch, t_len, _ = x_prompt.shape
    n_dec = x_sample.shape[0]
    assert w_in.shape[0] == 1, "single layer"

    w_in0 = w_in[0]
    w_in_b = w_in0.astype(BF16)
    w_gif = w_in0[:, O_GIF:O_XQ].T.astype(BF16)
    w_gate = w_in_b[:, O_GATE:]
    w_pool_b = w_pool[0].astype(BF16)
    w_kv_b = w_mem_kv[0].astype(BF16)
    wbp, wbm, wbx = (w[0].astype(BF16) for w in (w_br_pool, w_br_mlstm, w_br_xattn))
    wo = w_out[0].astype(BF16)
    w1 = w_ff1[0].astype(BF16)
    w2 = w_ff2[0].astype(BF16)
    g_mix, g_pm, g_mlp, g_pmlp, g_m = (g[0].reshape(1, D) for g in
                                       (g_pre_mix, g_post_mix, g_pre_mlp, g_post_mlp, g_mem))
    scale = pool_scale[0].reshape(1, POOL_W)
    bif = b_if[0]

    xs = x_sample.reshape(n_dec, D)
    u_s, q_s, k_s, v_s, og_s, xq_s, _, gifc_s = _in_proj(xs, g_mix, w_in_b, w_gif, n_dec, F32)
    buf_t = jnp.transpose(state_pool_buf[0], (1, 0, 2))
    ypool_s, buf_s_t = _pool_decode(u_s, buf_t, w_pool_b, scale)
    yx_s = _xattn_decode(xq_s, cache_mem_k[0], cache_mem_v[0])
    dec_mlstm = (q_s, k_s, v_s, og_s, gifc_s, bif.reshape(1, 2 * HEADS),
                 state_mlstm_C[0], state_mlstm_n[0], state_mlstm_m[0])

    xp = x_prompt.reshape(batch * t_len, D)
    mk_p, mv_p = _mem_kv(mem_prompt.reshape(batch * N_MEM, D), g_m, w_kv_b)
    u, q, k, v, og, xq, gift, _ = _in_proj(xp, g_mix, w_in_b, w_gif, TM_PROMPT, BF16)
    y_pool, buf_p = _pool_prompt(u, w_pool_b, scale, batch, t_len)
    rowp, colp, m_last = _gates(gift, bif.reshape(2 * HEADS, 1), batch, t_len)
    y_ml, c_p, n_p = _mlstm_prompt(q, k, v, og, rowp, colp, batch, t_len)
    y_x = _xattn_prompt(xq, mk_p, mv_p, batch, t_len)
    x1, *dec_out = _merge(xp, y_pool, y_ml, y_x, g_mix, g_pm, w_gate, wbp, wbm, wbx, wo,
                          TM_MERGE, dec=dec_mlstm)
    yp = _ffn(x1, g_mlp, g_pmlp, w1, w2, TM_PROMPT)

    yml_s, c_s, n_s, m_s = _mlstm_decode_results(*dec_out)
    x1_s = _merge(xs, ypool_s, yml_s, yx_s, g_mix, g_pm, w_gate, wbp, wbm, wbx, wo, n_dec)
    ys = _ffn(x1_s, g_mlp, g_pmlp, w1, w2, n_dec)

    return (yp.reshape(batch, t_len, D), ys.reshape(n_dec, 1, D),
            buf_p[None], c_p[None], n_p.reshape(1, batch, HEADS, DH),
            m_last.reshape(1, batch, HEADS),
            mk_p.reshape(1, batch, N_MEM, XH, XDH), mv_p.reshape(1, batch, N_MEM, XH, XDH),
            jnp.transpose(buf_s_t, (1, 0, 2))[None], c_s[None], n_s[None], m_s[None])
```

```python
import jax
import jax.numpy as jnp
from jax import lax
from jax.experimental import pallas as pl
from jax.experimental.pallas import tpu as pltpu

F32 = jnp.float32
BF16 = jnp.bfloat16

D = 1024
POOL_W = 512
POOL_G = 128
POOL_WINDOWS = (2, 4, 8, 16)
POOL_BUF = 15
HEADS = 4
DH = 256
XH = 4
XDH = 128
XW = 512
N_MEM = 256
D_FF = 4096
EPS = 1e-6
NEG = -1e30
PAST_LEN = 16384

TM_PROMPT = 1024
TM_INPROJ = 512
TM_MERGE = 512
CHUNK = 256
MLSTM_GROUP = 4

VMEM_LIMIT = 56 * 1024 * 1024

O_QKVO = POOL_W
O_GIF = O_QKVO + 4 * D
O_XQ = O_GIF + 2 * HEADS
O_GATE = O_XQ + XW


def _params(sem):
    return pltpu.CompilerParams(dimension_semantics=sem, vmem_limit_bytes=VMEM_LIMIT)


def _const_spec(shape):
    nd = len(shape)
    return pl.BlockSpec(shape, lambda *_: (0,) * nd, pipeline_mode=pl.Buffered(1))


def _rms(x, g):
    ms = jnp.mean(x * x, axis=-1, keepdims=True)
    return x * lax.rsqrt(ms + EPS) * g


def _log_sigmoid(x):
    return jnp.minimum(x, 0.0) - jnp.log(1.0 + jnp.exp(-jnp.abs(x)))


def _sigmoid(x):
    return 1.0 / (1.0 + jnp.exp(-x))


def _dot(a, b):
    return jnp.dot(a, b, preferred_element_type=F32)


def _dot_nt(a, b):
    return lax.dot_general(a, b, (((1,), (1,)), ((), ())), preferred_element_type=F32)


def _dot_tn(a, b):
    return lax.dot_general(a, b, (((0,), (0,)), ((), ())), preferred_element_type=F32)


MID_COLS = 640
N_INPROJ_IN = 4
N_INPROJ_OUT = 8


def _in_proj_dec_kernel(*refs):
    host_in = refs[:N_INPROJ_IN]
    dec_in = refs[N_INPROJ_IN:N_INPROJ_IN + 3]
    host_out = refs[N_INPROJ_IN + 3:N_INPROJ_IN + 3 + N_INPROJ_OUT]
    _xattn_decode_kernel(*dec_in, refs[-1])
    _in_proj_kernel(*host_in, *host_out)


def _in_proj_kernel(x_ref, g_ref, w_ref, wmid_ref, u_ref, q_ref, k_ref, v_ref, og_ref,
                    xq_ref, gift_ref, gifc_ref):
    h = _rms(x_ref[...], g_ref[...]).astype(BF16)

    def seg(lo, n):
        return _dot(h, w_ref[:, lo:lo + n])

    u_ref[...] = seg(0, POOL_W)
    q_ref[...] = seg(POOL_W, D).astype(q_ref.dtype)
    k_ref[...] = (seg(POOL_W + D, D) * (DH ** -0.5)).astype(k_ref.dtype)
    v_ref[...] = seg(POOL_W + 2 * D, D).astype(v_ref.dtype)
    og_ref[...] = _sigmoid(seg(POOL_W + 3 * D, D))
    mid = _dot(h, wmid_ref[:, 0:MID_COLS])
    xq_lo = O_XQ - O_GIF
    xq_ref[...] = mid[:, xq_lo:xq_lo + XW].astype(xq_ref.dtype)
    gifc_ref[...] = mid[:, 0:2 * HEADS]
    for r in range(mid.shape[0] // 128):
        gift_ref[:, r * 128:(r + 1) * 128] = mid[r * 128:(r + 1) * 128, 0:128].T[0:2 * HEADS, :]


XQ_WINDOW = 768


def _in_proj(x2, g, w_in_b, tm, qkv_dtype, dec=None):
    n = x2.shape[0]
    steps = n // tm
    row = lambda i: (i, 0)
    assert O_GIF % XQ_WINDOW == 0 and O_XQ + XW - O_GIF <= MID_COLS <= XQ_WINDOW
    out_shape = (
        jax.ShapeDtypeStruct((n, POOL_W), F32),
        jax.ShapeDtypeStruct((n, D), qkv_dtype),
        jax.ShapeDtypeStruct((n, D), qkv_dtype),
        jax.ShapeDtypeStruct((n, D), qkv_dtype),
        jax.ShapeDtypeStruct((n, D), F32),
        jax.ShapeDtypeStruct((n, XW), qkv_dtype),
        jax.ShapeDtypeStruct((2 * HEADS, n), F32),
        jax.ShapeDtypeStruct((n, 2 * HEADS), F32),
    )
    args = (x2, g, w_in_b, w_in_b)
    in_specs = [
        pl.BlockSpec((tm, D), row),
        _const_spec((1, D)),
        _const_spec((D, O_GIF)),
        pl.BlockSpec((D, XQ_WINDOW), lambda i: (0, O_GIF // XQ_WINDOW),
                     pipeline_mode=pl.Buffered(1)),
    ]
    out_specs = (
        pl.BlockSpec((tm, POOL_W), row),
        pl.BlockSpec((tm, D), row),
        pl.BlockSpec((tm, D), row),
        pl.BlockSpec((tm, D), row),
        pl.BlockSpec((tm, D), row),
        pl.BlockSpec((tm, XW), row),
        pl.BlockSpec((2 * HEADS, tm), lambda i: (0, i)),
        pl.BlockSpec((tm, 2 * HEADS), row),
    )
    assert len(args) == N_INPROJ_IN and len(out_shape) == N_INPROJ_OUT
    body = _in_proj_kernel
    if dec is not None:
        d_args, d_in, d_shape, d_out = _xattn_decode_operands(*dec, dec[0].shape[0] // steps)
        args, in_specs = args + d_args, in_specs + d_in
        out_shape, out_specs = out_shape + (d_shape,), out_specs + (d_out,)
        body = _in_proj_dec_kernel
    return pl.pallas_call(
        body,
        out_shape=out_shape,
        grid=(steps,),
        in_specs=in_specs,
        out_specs=out_specs,
        compiler_params=_params(("parallel",)),
        name="in_proj",
    )(*args)


def _mem_kv_kernel(mem_ref, g_ref, w_ref, k_ref, v_ref):
    h = _rms(mem_ref[...], g_ref[...]).astype(BF16)
    kv = _dot(h, w_ref[...])
    k_ref[...] = kv[:, :XW]
    v_ref[...] = kv[:, XW:]


def _mem_kv(mem2, g, w):
    n = mem2.shape[0]
    tm = TM_PROMPT
    row = lambda i: (i, 0)
    return pl.pallas_call(
        _mem_kv_kernel,
        out_shape=(jax.ShapeDtypeStruct((n, XW), F32), jax.ShapeDtypeStruct((n, XW), F32)),
        grid=(n // tm,),
        in_specs=[pl.BlockSpec((tm, D), row), _const_spec((1, D)), _const_spec((D, 2 * XW))],
        out_specs=(pl.BlockSpec((tm, XW), row), pl.BlockSpec((tm, XW), row)),
        compiler_params=_params(("parallel",)),
        name="mem_kv",
    )(mem2, g, w)


def _pool_prompt_kernel(u_ref, w_ref, s_ref, y_ref, nb_ref):
    t_len = u_ref.shape[0]
    t_idx = lax.broadcasted_iota(jnp.int32, (t_len, POOL_G), 0)
    for g, win in enumerate(POOL_WINDOWS):
        cols = slice(g * POOL_G, (g + 1) * POOL_G)
        u = u_ref[:, cols]
        acc = u
        span = 1
        while span < win:
            shifted = pltpu.roll(acc, span, axis=0)
            acc = acc + jnp.where(t_idx >= span, shifted, 0.0)
            span *= 2
        cnt = jnp.minimum(t_idx + 1, win).astype(F32)
        d = (acc / cnt - u).astype(BF16)
        y = _dot(d, w_ref[g]) * s_ref[:, cols]
        y_ref[:, cols] = y.astype(y_ref.dtype)
    nb_ref[0] = u_ref[t_len - POOL_BUF:, :]


def _pool_prompt(u, w_pool, scale, batch, t_len):
    return pl.pallas_call(
        _pool_prompt_kernel,
        out_shape=(jax.ShapeDtypeStruct((batch * t_len, POOL_W), BF16),
                   jax.ShapeDtypeStruct((batch, POOL_BUF, POOL_W), F32)),
        grid=(batch,),
        in_specs=[pl.BlockSpec((t_len, POOL_W), lambda b: (b, 0)),
                  _const_spec((len(POOL_WINDOWS), POOL_G, POOL_G)),
                  _const_spec((1, POOL_W))],
        out_specs=(pl.BlockSpec((t_len, POOL_W), lambda b: (b, 0)),
                   pl.BlockSpec((1, POOL_BUF, POOL_W), lambda b: (b, 0, 0))),
        compiler_params=_params(("parallel",)),
        name="pool_prompt",
    )(u, w_pool, scale)


def _pool_decode_kernel(u_ref, buf_ref, w_ref, s_ref, y_ref, nb_ref):
    u_all = u_ref[...]
    for g, win in enumerate(POOL_WINDOWS):
        cols = slice(g * POOL_G, (g + 1) * POOL_G)
        u = u_all[:, cols]
        acc = u
        for j in range(POOL_BUF - (win - 1), POOL_BUF):
            acc = acc + buf_ref[j, :, cols]
        cnt = float(min(win, PAST_LEN + 1))
        d = (acc / cnt - u).astype(BF16)
        y = _dot(d, w_ref[g]) * s_ref[:, cols]
        y_ref[:, cols] = y.astype(y_ref.dtype)
    for j in range(POOL_BUF - 1):
        nb_ref[j] = buf_ref[j + 1]
    nb_ref[POOL_BUF - 1] = u_all


def _pool_decode(u, buf_t, w_pool, scale):
    n = u.shape[0]
    return pl.pallas_call(
        _pool_decode_kernel,
        out_shape=(jax.ShapeDtypeStruct((n, POOL_W), BF16),
                   jax.ShapeDtypeStruct((POOL_BUF, n, POOL_W), F32)),
        grid=(1,),
        in_specs=[_const_spec((n, POOL_W)), _const_spec((POOL_BUF, n, POOL_W)),
                  _const_spec((len(POOL_WINDOWS), POOL_G, POOL_G)), _const_spec((1, POOL_W))],
        out_specs=(pl.BlockSpec((n, POOL_W), lambda i: (0, 0)),
                   pl.BlockSpec((POOL_BUF, n, POOL_W), lambda i: (0, 0, 0))),
        compiler_params=_params(("arbitrary",)),
        name="pool_decode",
    )(u, buf_t, w_pool, scale)


COL_M, COL_INTER, COL_EINV, COL_END, COL_DECAY = (i * HEADS for i in range(5))


def _scan_lanes(x, op, fill):
    t_len = x.shape[-1]
    lane = lax.broadcasted_iota(jnp.int32, x.shape, 1)
    k = 1
    while k < t_len:
        shifted = pltpu.roll(x, k, axis=1)
        x = op(x, jnp.where(lane >= k, shifted, fill))
        k *= 2
    return x


def _gates_kernel(gift_ref, bif_ref, row_ref, col_ref, mlast_ref, pack_ref):
    t_len = gift_ref.shape[1]
    L = CHUNK
    g = gift_ref[...] + bif_ref[...]
    ig = g[0:HEADS, :]
    lf = _log_sigmoid(g[HEADS:2 * HEADS, :])
    b_cum = _scan_lanes(lf, jnp.add, 0.0)
    a = ig - b_cum
    m_run = jnp.maximum(_scan_lanes(a, jnp.maximum, NEG), 0.0)
    m_tot = b_cum + m_run
    row_ref[0, 0:HEADS, :] = a
    row_ref[0, HEADS:2 * HEADS, :] = m_run
    mlast_ref[0] = m_tot[:, t_len - 1:t_len]
    pack_ref[...] = jnp.zeros(pack_ref.shape, F32)
    pack_ref[COL_M:COL_M + HEADS, :] = m_run
    pack_ref[COL_EINV:COL_EINV + HEADS, :] = jnp.exp(-m_tot)
    for c in range(t_len // L):
        lo, hi = c * L, (c + 1) * L
        m_prev = jnp.zeros((HEADS, 1), F32) if c == 0 else m_run[:, lo - 1:lo]
        m_end = m_run[:, hi - 1:hi]
        pack_ref[COL_INTER:COL_INTER + HEADS, lo:hi] = jnp.exp(m_prev - m_run[:, lo:hi])
        pack_ref[COL_END:COL_END + HEADS, lo:hi] = jnp.exp(a[:, lo:hi] - m_end)
        pack_ref[COL_DECAY:COL_DECAY + HEADS, lo:hi] = jnp.broadcast_to(
            jnp.exp(m_prev - m_end), (HEADS, L))
    for c in range(t_len // 128):
        col_ref[0, c * 128:(c + 1) * 128, :] = pack_ref[:, c * 128:(c + 1) * 128].T


def _gates(gift, bif, batch, t_len):
    return pl.pallas_call(
        _gates_kernel,
        out_shape=(jax.ShapeDtypeStruct((batch, 2 * HEADS, t_len), F32),
                   jax.ShapeDtypeStruct((batch, t_len, 128), F32),
                   jax.ShapeDtypeStruct((batch, HEADS, 1), F32)),
        grid=(batch,),
        in_specs=[pl.BlockSpec((2 * HEADS, t_len), lambda b: (0, b)), _const_spec((2 * HEADS, 1))],
        out_specs=(pl.BlockSpec((1, 2 * HEADS, t_len), lambda b: (b, 0, 0)),
                   pl.BlockSpec((1, t_len, 128), lambda b: (b, 0, 0)),
                   pl.BlockSpec((1, HEADS, 1), lambda b: (b, 0, 0))),
        scratch_shapes=[pltpu.VMEM((128, t_len), F32)],
        compiler_params=_params(("parallel",)),
        name="mlstm_gates",
    )(gift, bif)


def _mlstm_prompt_kernel(q_ref, k_ref, v_ref, og_ref, row_ref, col_ref, y_ref, c_ref, n_ref):
    c_idx = pl.program_id(1)
    L = q_ref.shape[1]

    @pl.when(c_idx == 0)
    def _():
        c_ref[...] = jnp.zeros(c_ref.shape, F32)
        n_ref[...] = jnp.zeros(n_ref.shape, F32)

    t_idx = lax.broadcasted_iota(jnp.int32, (L, L), 0)
    s_idx = lax.broadcasted_iota(jnp.int32, (L, L), 1)
    causal = s_idx <= t_idx
    for b in range(q_ref.shape[0]):
        for h in range(HEADS):
            hs = slice(h * DH, (h + 1) * DH)
            qh = q_ref[b, :, hs]
            kh = k_ref[b, :, hs]
            vh = v_ref[b, :, hs]
            a_row = row_ref[b, h:h + 1, :]
            m_col = col_ref[b, :, COL_M + h:COL_M + h + 1]
            w_inter = col_ref[b, :, COL_INTER + h:COL_INTER + h + 1]
            einv = col_ref[b, :, COL_EINV + h:COL_EINV + h + 1]
            w_end = col_ref[b, :, COL_END + h:COL_END + h + 1]
            decay = col_ref[b, 0:1, COL_DECAY + h:COL_DECAY + h + 1]

            dmat = jnp.exp(jnp.where(causal, a_row - m_col, NEG))
            sw = _dot_nt(qh, kh) * dmat
            c_old = c_ref[b, h]
            n_old = n_ref[b, h]
            inter = _dot_nt(qh, c_old.astype(BF16))
            num = _dot(sw.astype(BF16), vh) + w_inter * inter
            nq = jnp.sum(qh.astype(F32) * n_old, axis=-1, keepdims=True)
            den = jnp.sum(sw, axis=-1, keepdims=True) + w_inter * nq
            r = 1.0 / jnp.maximum(jnp.abs(den), einv)
            y_ref[b, :, hs] = (og_ref[b, :, hs] * (num * r)).astype(y_ref.dtype)

            vw = (vh.astype(F32) * w_end).astype(BF16)
            c_ref[b, h] = decay * c_old + _dot_tn(vw, kh)
            n_ref[b, h] = decay * n_old + jnp.sum(kh.astype(F32) * w_end, axis=0, keepdims=True)


def _mlstm_prompt(q, k, v, og, rowp, colp, batch, t_len):
    L = CHUNK
    nc = t_len // L
    gb = MLSTM_GROUP
    q3, k3, v3, og3 = (a.reshape(batch, t_len, D) for a in (q, k, v, og))
    tok = lambda g, c: (g, c, 0)
    y, c_fin, n_fin = pl.pallas_call(
        _mlstm_prompt_kernel,
        out_shape=(jax.ShapeDtypeStruct((batch, t_len, D), BF16),
                   jax.ShapeDtypeStruct((batch, HEADS, DH, DH), F32),
                   jax.ShapeDtypeStruct((batch, HEADS, 1, DH), F32)),
        grid=(batch // gb, nc),
        in_specs=[pl.BlockSpec((gb, L, D), tok), pl.BlockSpec((gb, L, D), tok),
                  pl.BlockSpec((gb, L, D), tok), pl.BlockSpec((gb, L, D), tok),
                  pl.BlockSpec((gb, 2 * HEADS, L), lambda g, c: (g, 0, c)),
                  pl.BlockSpec((gb, L, 128), tok)],
        out_specs=(pl.BlockSpec((gb, L, D), tok),
                   pl.BlockSpec((gb, HEADS, DH, DH), lambda g, c: (g, 0, 0, 0)),
                   pl.BlockSpec((gb, HEADS, 1, DH), lambda g, c: (g, 0, 0, 0))),
        compiler_params=_params(("parallel", "arbitrary")),
        name="mlstm_prompt",
    )(q3, k3, v3, og3, rowp, colp)
    return y.reshape(batch * t_len, D), c_fin, n_fin


def _mlstm_decode_body(q_ref, k_ref, vt_ref, ogt_ref, gif_ref, bif_ref, c0_ref, n0_ref, m0_ref,
                       ht_ref, c_out_ref, n_out_ref, m_out_ref):
    for j in range(q_ref.shape[0]):
        gi = gif_ref[j] + bif_ref[...]
        m0 = m0_ref[j]
        for h in range(HEADS):
            hs = slice(h * DH, (h + 1) * DH)
            qr = q_ref[j, :, hs]
            kr = k_ref[j, :, hs]
            vc = vt_ref[0, hs, j:j + 1]
            ig = gi[:, h:h + 1]
            lf = _log_sigmoid(gi[:, HEADS + h:HEADS + h + 1])
            m_old = m0[:, h:h + 1]
            m_new = jnp.maximum(lf + m_old, ig)
            w_i = jnp.exp(ig - m_new)
            w_f = jnp.exp(lf + m_old - m_new)
            c_old = c0_ref[j, h]
            n_old = n0_ref[j, h:h + 1, :]
            cq = jnp.sum(c_old * qr, axis=-1, keepdims=True)
            qk = jnp.sum(qr * kr, axis=-1, keepdims=True)
            nq = jnp.sum(n_old * qr, axis=-1, keepdims=True)
            num = (w_i * qk) * vc + w_f * cq
            den = w_i * qk + w_f * nq
            hcol = num / jnp.maximum(jnp.abs(den), jnp.exp(-m_new))
            ht_ref[0, hs, j:j + 1] = ogt_ref[0, hs, j:j + 1] * hcol
            c_out_ref[j, h] = w_f * c_old + (w_i * vc) * kr
            n_out_ref[j, h:h + 1, :] = w_f * n_old + w_i * kr
            m_out_ref[j, :, h:h + 1] = m_new


def _mlstm_decode_operands(q, k, v, og, gifc, bif_row, c0, n0, m0, bb):
    n = q.shape[0]
    r3 = lambda i: (i, 0, 0)
    r4 = lambda i: (i, 0, 0, 0)
    q3, k3 = (a.reshape(n, 1, D) for a in (q, k))
    vt, ogt = (jnp.transpose(a.reshape(n // bb, bb, D), (0, 2, 1)) for a in (v, og))
    args = (q3, k3, vt, ogt, gifc.reshape(n, 1, 2 * HEADS), bif_row, c0, n0,
            m0.reshape(n, 1, HEADS))
    in_specs = [pl.BlockSpec((bb, 1, D), r3), pl.BlockSpec((bb, 1, D), r3),
                pl.BlockSpec((1, D, bb), r3), pl.BlockSpec((1, D, bb), r3),
                pl.BlockSpec((bb, 1, 2 * HEADS), r3), _const_spec((1, 2 * HEADS)),
                pl.BlockSpec((bb, HEADS, DH, DH), r4), pl.BlockSpec((bb, HEADS, DH), r3),
                pl.BlockSpec((bb, 1, HEADS), r3)]
    out_shape = (jax.ShapeDtypeStruct((n // bb, D, bb), F32),
                 jax.ShapeDtypeStruct((n, HEADS, DH, DH), F32),
                 jax.ShapeDtypeStruct((n, HEADS, DH), F32),
                 jax.ShapeDtypeStruct((n, 1, HEADS), F32))
    out_specs = (pl.BlockSpec((1, D, bb), r3), pl.BlockSpec((bb, HEADS, DH, DH), r4),
                 pl.BlockSpec((bb, HEADS, DH), r3), pl.BlockSpec((bb, 1, HEADS), r3))
    return args, in_specs, out_shape, out_specs


def _mlstm_decode_results(ht, c_new, n_new, m_new):
    nblk, _, bb = ht.shape
    y = jnp.transpose(ht, (0, 2, 1)).reshape(nblk * bb, D).astype(BF16)
    return y, c_new, n_new, m_new.reshape(nblk * bb, HEADS)


def _xattn_prompt_kernel(xq_ref, mk_ref, mv_ref, y_ref):
    scale = XDH ** -0.5
    for h in range(XH):
        hs = slice(h * XDH, (h + 1) * XDH)
        s = _dot_nt(xq_ref[:, hs], mk_ref[:, hs].astype(BF16)) * scale
        p = jnp.exp(s - jnp.max(s, axis=-1, keepdims=True))
        l = jnp.sum(p, axis=-1, keepdims=True)
        o = _dot(p.astype(BF16), mv_ref[:, hs].astype(BF16)) / l
        y_ref[:, hs] = o.astype(y_ref.dtype)


def _xattn_prompt(xq, mk, mv, batch, t_len):
    tq = TM_PROMPT
    nq = t_len // tq
    return pl.pallas_call(
        _xattn_prompt_kernel,
        out_shape=jax.ShapeDtypeStruct((batch * t_len, XW), BF16),
        grid=(batch, nq),
        in_specs=[pl.BlockSpec((tq, XW), lambda b, i: (b * nq + i, 0)),
                  pl.BlockSpec((N_MEM, XW), lambda b, i: (b, 0)),
                  pl.BlockSpec((N_MEM, XW), lambda b, i: (b, 0))],
        out_specs=pl.BlockSpec((tq, XW), lambda b, i: (b * nq + i, 0)),
        compiler_params=_params(("parallel", "arbitrary")),
        name="xattn_prompt",
    )(xq, mk, mv)


def _xattn_decode_kernel(q_ref, mk_ref, mv_ref, y_ref):
    scale = XDH ** -0.5
    q = q_ref[...][:, None, :, :]
    s = jnp.sum(mk_ref[...] * q, axis=-1, keepdims=True) * scale
    mx = jnp.max(s, axis=1, keepdims=True)
    mx = jnp.maximum(mx, pltpu.roll(mx, XH, axis=2))
    p = jnp.exp(s - mx)
    l = jnp.sum(p, axis=1, keepdims=True)
    l = l + pltpu.roll(l, XH, axis=2)
    o = jnp.sum(p * mv_ref[...], axis=1, keepdims=True)
    o = (o + pltpu.roll(o, XH, axis=2)) / l
    y_ref[...] = o[:, 0, 0:XH, :].astype(y_ref.dtype)


def _xattn_decode_operands(xq, mk, mv, bb):
    n = xq.shape[0]
    xq4 = xq.reshape(n, XH, XDH)
    q2 = jnp.concatenate([xq4, xq4], axis=1)
    mk2, mv2 = (a.reshape(n, N_MEM // 2, 2 * XH, XDH) for a in (mk, mv))
    kv_spec = pl.BlockSpec((bb, N_MEM // 2, 2 * XH, XDH), lambda i: (i, 0, 0, 0))
    in_specs = [pl.BlockSpec((bb, 2 * XH, XDH), lambda i: (i, 0, 0)), kv_spec, kv_spec]
    out_shape = jax.ShapeDtypeStruct((n, XH, XDH), BF16)
    out_spec = pl.BlockSpec((bb, XH, XDH), lambda i: (i, 0, 0))
    return (q2, mk2, mv2), in_specs, out_shape, out_spec


N_MERGE_IN = 11
N_DEC_IN = 9


def _merge_dec_kernel(*refs):
    host_in = refs[:N_MERGE_IN]
    dec_in = refs[N_MERGE_IN:N_MERGE_IN + N_DEC_IN]
    o_ref = refs[N_MERGE_IN + N_DEC_IN]
    dec_out = refs[N_MERGE_IN + N_DEC_IN + 1:-1]
    _mlstm_decode_body(*dec_in, *dec_out)
    _merge_kernel(*host_in, o_ref, refs[-1])


def _merge_kernel(x_ref, yp_ref, ym_ref, yx_ref, gpre_ref, gpost_ref, wg_ref, wbp_ref, wbm_ref,
                  wbx_ref, wo_ref, o_ref, merged_ref):
    x = x_ref[...]
    h = _rms(x, gpre_ref[...]).astype(BF16)
    yp = yp_ref[...]
    ym = ym_ref[...]
    yx = yx_ref[...]
    nchunk = 256
    for c in range(D // nchunk):
        cs = slice(c * nchunk, (c + 1) * nchunk)
        acc = _sigmoid(_dot(h, wg_ref[:, c * nchunk:(c + 1) * nchunk])) * _dot(yp, wbp_ref[:, cs])
        acc += (_sigmoid(_dot(h, wg_ref[:, D + c * nchunk:D + (c + 1) * nchunk]))
                * _dot(ym, wbm_ref[:, cs]))
        acc += (_sigmoid(_dot(h, wg_ref[:, 2 * D + c * nchunk:2 * D + (c + 1) * nchunk]))
                * _dot(yx, wbx_ref[:, cs]))
        merged_ref[:, cs] = acc.astype(BF16)
    o_ref[...] = x + _rms(_dot(merged_ref[...], wo_ref[...]), gpost_ref[...])


def _merge(x2, yp, ym, yx, gpre, gpost, wg, wbp, wbm, wbx, wo, tm, dec=None):
    n = x2.shape[0]
    steps = n // tm
    row = lambda i: (i, 0)
    args = (x2, yp, ym, yx, gpre, gpost, wg, wbp, wbm, wbx, wo)
    in_specs = [pl.BlockSpec((tm, D), row), pl.BlockSpec((tm, POOL_W), row),
                pl.BlockSpec((tm, D), row), pl.BlockSpec((tm, XW), row),
                _const_spec((1, D)), _const_spec((1, D)), _const_spec((D, 3 * D)),
                _const_spec((POOL_W, D)), _const_spec((D, D)), _const_spec((XW, D)),
                _const_spec((D, D))]
    assert len(args) == N_MERGE_IN
    out_shape = jax.ShapeDtypeStruct((n, D), F32)
    out_specs = pl.BlockSpec((tm, D), row)
    body = _merge_kernel
    if dec is not None:
        n_dec = dec[0].shape[0]
        d_args, d_in, d_shape, d_out = _mlstm_decode_operands(*dec, n_dec // steps)
        assert len(d_args) == N_DEC_IN
        args, in_specs = args + d_args, in_specs + d_in
        out_shape, out_specs = (out_shape,) + d_shape, (out_specs,) + d_out
        body = _merge_dec_kernel
    return pl.pallas_call(
        body,
        out_shape=out_shape,
        grid=(steps,),
        in_specs=in_specs,
        out_specs=out_specs,
        scratch_shapes=[pltpu.VMEM((tm, D), BF16)],
        compiler_params=_params(("parallel",)),
        name="merge_out",
    )(*args)


def _ffn_kernel(x_ref, gpre_ref, gpost_ref, w1_ref, w2_ref, o_ref):
    x = x_ref[...]
    h = _rms(x, gpre_ref[...]).astype(BF16)
    fchunk = 1024
    acc = None
    for c in range(D_FF // fchunk):
        a = _dot(h, w1_ref[:, c * fchunk:(c + 1) * fchunk])
        a = jnp.square(jnp.maximum(a, 0.0)).astype(BF16)
        part = _dot(a, w2_ref[c * fchunk:(c + 1) * fchunk, :])
        acc = part if acc is None else acc + part
    o_ref[...] = x + _rms(acc, gpost_ref[...])


def _ffn(x2, gpre, gpost, w1, w2, tm):
    n = x2.shape[0]
    row = lambda i: (i, 0)
    return pl.pallas_call(
        _ffn_kernel,
        out_shape=jax.ShapeDtypeStruct((n, D), F32),
        grid=(n // tm,),
        in_specs=[pl.BlockSpec((tm, D), row), _const_spec((1, D)), _const_spec((1, D)),
                  _const_spec((D, D_FF)), _const_spec((D_FF, D))],
        out_specs=pl.BlockSpec((tm, D), row),
        compiler_params=_params(("parallel",)),
        name="ffn",
    )(x2, gpre, gpost, w1, w2)


def kernel(x_prompt, x_sample, mem_prompt, state_pool_buf, state_mlstm_C, state_mlstm_n, state_mlstm_m, cache_mem_k, cache_mem_v, g_pre_mix, w_in, b_if, w_pool, pool_scale, g_mem, w_mem_kv, w_br_pool, w_br_mlstm, w_br_xattn, w_out, g_post_mix, g_pre_mlp, w_ff1, w_ff2, g_post_mlp):
    batch, t_len, _ = x_prompt.shape
    n_dec = x_sample.shape[0]
    assert w_in.shape[0] == 1, "single layer"

    w_in_b = w_in[0].astype(BF16)
    w_gate = w_in_b[:, O_GATE:]
    w_pool_b = w_pool[0].astype(BF16)
    w_kv_b = w_mem_kv[0].astype(BF16)
    wbp, wbm, wbx = (w[0].astype(BF16) for w in (w_br_pool, w_br_mlstm, w_br_xattn))
    wo = w_out[0].astype(BF16)
    w1 = w_ff1[0].astype(BF16)
    w2 = w_ff2[0].astype(BF16)
    g_mix, g_pm, g_mlp, g_pmlp, g_m = (g[0].reshape(1, D) for g in
                                       (g_pre_mix, g_post_mix, g_pre_mlp, g_post_mlp, g_mem))
    scale = pool_scale[0].reshape(1, POOL_W)
    bif = b_if[0]

    xs = x_sample.reshape(n_dec, D)
    u_s, q_s, k_s, v_s, og_s, xq_s, _, gifc_s = _in_proj(xs, g_mix, w_in_b, n_dec, F32)
    buf_t = jnp.transpose(state_pool_buf[0], (1, 0, 2))
    ypool_s, buf_s_t = _pool_decode(u_s, buf_t, w_pool_b, scale)
    dec_xattn = (xq_s, cache_mem_k[0], cache_mem_v[0])
    dec_mlstm = (q_s, k_s, v_s, og_s, gifc_s, bif.reshape(1, 2 * HEADS),
                 state_mlstm_C[0], state_mlstm_n[0], state_mlstm_m[0])

    xp = x_prompt.reshape(batch * t_len, D)
    mk_p, mv_p = _mem_kv(mem_prompt.reshape(batch * N_MEM, D), g_m, w_kv_b)
    u, q, k, v, og, xq, gift, _, yx_s = _in_proj(xp, g_mix, w_in_b, TM_INPROJ, BF16, dec=dec_xattn)
    yx_s = yx_s.reshape(n_dec, XW)
    y_pool, buf_p = _pool_prompt(u, w_pool_b, scale, batch, t_len)
    rowp, colp, m_last = _gates(gift, bif.reshape(2 * HEADS, 1), batch, t_len)
    y_ml, c_p, n_p = _mlstm_prompt(q, k, v, og, rowp, colp, batch, t_len)
    y_x = _xattn_prompt(xq, mk_p, mv_p, batch, t_len)
    x1, *dec_out = _merge(xp, y_pool, y_ml, y_x, g_mix, g_pm, w_gate, wbp, wbm, wbx, wo,
                          TM_MERGE, dec=dec_mlstm)
    yp = _ffn(x1, g_mlp, g_pmlp, w1, w2, TM_PROMPT)

    yml_s, c_s, n_s, m_s = _mlstm_decode_results(*dec_out)
    x1_s = _merge(xs, ypool_s, yml_s, yx_s, g_mix, g_pm, w_gate, wbp, wbm, wbx, wo, n_dec)
    ys = _ffn(x1_s, g_mlp, g_pmlp, w1, w2, n_dec)

    return (yp.reshape(batch, t_len, D), ys.reshape(n_dec, 1, D),
            buf_p[None], c_p[None], n_p.reshape(1, batch, HEADS, DH),
            m_last.reshape(1, batch, HEADS),
            mk_p.reshape(1, batch, N_MEM, XH, XDH), mv_p.reshape(1, batch, N_MEM, XH, XDH),
            jnp.transpose(buf_s_t, (1, 0, 2))[None], c_s[None], n_s[None], m_s[None])
```

```python
import functools

import jax
import jax.numpy as jnp
from jax import lax
from jax.experimental import pallas as pl
from jax.experimental.pallas import tpu as pltpu

F32 = jnp.float32
BF16 = jnp.bfloat16

D = 1024
POOL_W = 512
POOL_G = 128
POOL_WINDOWS = (2, 4, 8, 16)
POOL_BUF = 15
HEADS = 4
DH = 256
XH = 4
XDH = 128
XW = 512
N_MEM = 256
D_FF = 4096
EPS = 1e-6
NEG = -1e30
PAST_LEN = 16384

TM_PROMPT = 1024
TM_INPROJ = 512
TM_MERGE = 512
CHUNK = 256
MLSTM_GROUP = 4

VMEM_LIMIT = 56 * 1024 * 1024

O_QKVO = POOL_W
O_GIF = O_QKVO + 4 * D
O_XQ = O_GIF + 2 * HEADS
O_GATE = O_XQ + XW


def _params(sem):
    return pltpu.CompilerParams(dimension_semantics=sem, vmem_limit_bytes=VMEM_LIMIT)


def _const_spec(shape):
    nd = len(shape)
    return pl.BlockSpec(shape, lambda *_: (0,) * nd, pipeline_mode=pl.Buffered(1))


def _rms(x, g):
    ms = jnp.mean(x * x, axis=-1, keepdims=True)
    return x * lax.rsqrt(ms + EPS) * g


def _log_sigmoid(x):
    return jnp.minimum(x, 0.0) - jnp.log(1.0 + jnp.exp(-jnp.abs(x)))


def _sigmoid(x):
    return 1.0 / (1.0 + jnp.exp(-x))


def _dot(a, b):
    return jnp.dot(a, b, preferred_element_type=F32)


def _dot_nt(a, b):
    return lax.dot_general(a, b, (((1,), (1,)), ((), ())), preferred_element_type=F32)


def _dot_tn(a, b):
    return lax.dot_general(a, b, (((0,), (0,)), ((), ())), preferred_element_type=F32)


MID_COLS = 640
N_INPROJ_IN = 4
N_INPROJ_OUT = 8


def _in_proj_dec_kernel(n_conv, *refs):
    n_in = N_INPROJ_IN + 3 + n_conv
    host_in = refs[:N_INPROJ_IN]
    dec_in = refs[N_INPROJ_IN:N_INPROJ_IN + 3]
    conv_in = refs[N_INPROJ_IN + 3:n_in]
    host_out = refs[n_in:n_in + N_INPROJ_OUT]
    dec_out = refs[n_in + N_INPROJ_OUT]
    conv_out = refs[n_in + N_INPROJ_OUT + 1:]
    for src, dst in zip(conv_in, conv_out):
        dst[...] = src[...].astype(dst.dtype)
    _xattn_decode_kernel(*dec_in, dec_out)
    _in_proj_kernel(*host_in, *host_out)


def _in_proj_kernel(x_ref, g_ref, w_ref, wmid_ref, u_ref, q_ref, k_ref, v_ref, og_ref,
                    xq_ref, gift_ref, gifc_ref):
    h = _rms(x_ref[...], g_ref[...]).astype(BF16)

    def seg(lo, n):
        return _dot(h, w_ref[:, lo:lo + n])

    u_ref[...] = seg(0, POOL_W)
    q_ref[...] = seg(POOL_W, D).astype(q_ref.dtype)
    k_ref[...] = (seg(POOL_W + D, D) * (DH ** -0.5)).astype(k_ref.dtype)
    v_ref[...] = seg(POOL_W + 2 * D, D).astype(v_ref.dtype)
    og_ref[...] = _sigmoid(seg(POOL_W + 3 * D, D))
    mid = _dot(h, wmid_ref[:, 0:MID_COLS])
    xq_lo = O_XQ - O_GIF
    xq_ref[...] = mid[:, xq_lo:xq_lo + XW].astype(xq_ref.dtype)
    gifc_ref[...] = mid[:, 0:2 * HEADS]
    for r in range(mid.shape[0] // 128):
        gift_ref[:, r * 128:(r + 1) * 128] = mid[r * 128:(r + 1) * 128, 0:128].T[0:2 * HEADS, :]


XQ_WINDOW = 768


def _in_proj(x2, g, w_in_b, tm, qkv_dtype, dec=None, convert=()):
    n = x2.shape[0]
    steps = n // tm
    row = lambda i: (i, 0)
    assert O_GIF % XQ_WINDOW == 0 and O_XQ + XW - O_GIF <= MID_COLS <= XQ_WINDOW
    out_shape = (
        jax.ShapeDtypeStruct((n, POOL_W), F32),
        jax.ShapeDtypeStruct((n, D), qkv_dtype),
        jax.ShapeDtypeStruct((n, D), qkv_dtype),
        jax.ShapeDtypeStruct((n, D), qkv_dtype),
        jax.ShapeDtypeStruct((n, D), F32),
        jax.ShapeDtypeStruct((n, XW), qkv_dtype),
        jax.ShapeDtypeStruct((2 * HEADS, n), F32),
        jax.ShapeDtypeStruct((n, 2 * HEADS), F32),
    )
    args = (x2, g, w_in_b, w_in_b)
    in_specs = [
        pl.BlockSpec((tm, D), row),
        _const_spec((1, D)),
        _const_spec((D, O_GIF)),
        pl.BlockSpec((D, XQ_WINDOW), lambda i: (0, O_GIF // XQ_WINDOW),
                     pipeline_mode=pl.Buffered(1)),
    ]
    out_specs = (
        pl.BlockSpec((tm, POOL_W), row),
        pl.BlockSpec((tm, D), row),
        pl.BlockSpec((tm, D), row),
        pl.BlockSpec((tm, D), row),
        pl.BlockSpec((tm, D), row),
        pl.BlockSpec((tm, XW), row),
        pl.BlockSpec((2 * HEADS, tm), lambda i: (0, i)),
        pl.BlockSpec((tm, 2 * HEADS), row),
    )
    assert len(args) == N_INPROJ_IN and len(out_shape) == N_INPROJ_OUT
    body = _in_proj_kernel
    if dec is not None:
        d_args, d_in, d_shape, d_out = _xattn_decode_operands(*dec, dec[0].shape[0] // steps)
        args, in_specs = args + d_args, in_specs + d_in
        out_shape, out_specs = out_shape + (d_shape,), out_specs + (d_out,)
        for w in convert:
            spec = pl.BlockSpec((w.shape[0] // steps, w.shape[1]), row)
            args, in_specs = args + (w,), in_specs + [spec]
            out_shape += (jax.ShapeDtypeStruct(w.shape, BF16),)
            out_specs += (spec,)
        body = functools.partial(_in_proj_dec_kernel, len(convert))
    return pl.pallas_call(
        body,
        out_shape=out_shape,
        grid=(steps,),
        in_specs=in_specs,
        out_specs=out_specs,
        compiler_params=_params(("parallel",)),
        name="in_proj",
    )(*args)


MEM_KV_STEPS = 16


def _mem_kv_kernel(mem_ref, g_ref, w_ref, win_ref, k_ref, v_ref, winb_ref, wgate_ref):
    h = _rms(mem_ref[...], g_ref[...]).astype(BF16)
    kv = _dot(h, w_ref[...])
    k_ref[...] = kv[:, :XW]
    v_ref[...] = kv[:, XW:]
    w = win_ref[...]
    winb_ref[...] = w.astype(BF16)
    wgate_ref[...] = w[:, O_GATE:].astype(BF16)


def _mem_kv(mem2, g, w, w_in0):
    n = mem2.shape[0]
    tm = n // MEM_KV_STEPS
    wr = D // MEM_KV_STEPS
    n_in = w_in0.shape[1]
    row = lambda i: (i, 0)
    return pl.pallas_call(
        _mem_kv_kernel,
        out_shape=(jax.ShapeDtypeStruct((n, XW), F32), jax.ShapeDtypeStruct((n, XW), F32),
                   jax.ShapeDtypeStruct((D, n_in), BF16), jax.ShapeDtypeStruct((D, 3 * D), BF16)),
        grid=(MEM_KV_STEPS,),
        in_specs=[pl.BlockSpec((tm, D), row), _const_spec((1, D)), _const_spec((D, 2 * XW)),
                  pl.BlockSpec((wr, n_in), row)],
        out_specs=(pl.BlockSpec((tm, XW), row), pl.BlockSpec((tm, XW), row),
                   pl.BlockSpec((wr, n_in), row), pl.BlockSpec((wr, 3 * D), row)),
        compiler_params=_params(("parallel",)),
        name="mem_kv",
    )(mem2, g, w, w_in0)


def _pool_prompt_kernel(u_ref, w_ref, s_ref, y_ref, nb_ref):
    t_len = u_ref.shape[0]
    t_idx = lax.broadcasted_iota(jnp.int32, (t_len, POOL_G), 0)
    for g, win in enumerate(POOL_WINDOWS):
        cols = slice(g * POOL_G, (g + 1) * POOL_G)
        u = u_ref[:, cols]
        acc = u
        span = 1
        while span < win:
            shifted = pltpu.roll(acc, span, axis=0)
            acc = acc + jnp.where(t_idx >= span, shifted, 0.0)
            span *= 2
        cnt = jnp.minimum(t_idx + 1, win).astype(F32)
        d = (acc / cnt - u).astype(BF16)
        y = _dot(d, w_ref[g]) * s_ref[:, cols]
        y_ref[:, cols] = y.astype(y_ref.dtype)
    nb_ref[0] = u_ref[t_len - POOL_BUF:, :]


def _pool_prompt(u, w_pool, scale, batch, t_len):
    return pl.pallas_call(
        _pool_prompt_kernel,
        out_shape=(jax.ShapeDtypeStruct((batch * t_len, POOL_W), BF16),
                   jax.ShapeDtypeStruct((batch, POOL_BUF, POOL_W), F32)),
        grid=(batch,),
        in_specs=[pl.BlockSpec((t_len, POOL_W), lambda b: (b, 0)),
                  _const_spec((len(POOL_WINDOWS), POOL_G, POOL_G)),
                  _const_spec((1, POOL_W))],
        out_specs=(pl.BlockSpec((t_len, POOL_W), lambda b: (b, 0)),
                   pl.BlockSpec((1, POOL_BUF, POOL_W), lambda b: (b, 0, 0))),
        compiler_params=_params(("parallel",)),
        name="pool_prompt",
    )(u, w_pool, scale)


def _pool_decode_kernel(u_ref, buf_ref, w_ref, s_ref, y_ref, nb_ref):
    u_all = u_ref[...]
    for g, win in enumerate(POOL_WINDOWS):
        cols = slice(g * POOL_G, (g + 1) * POOL_G)
        u = u_all[:, cols]
        acc = u
        for j in range(POOL_BUF - (win - 1), POOL_BUF):
            acc = acc + buf_ref[j, :, cols]
        cnt = float(min(win, PAST_LEN + 1))
        d = (acc / cnt - u).astype(BF16)
        y = _dot(d, w_ref[g]) * s_ref[:, cols]
        y_ref[:, cols] = y.astype(y_ref.dtype)
    for j in range(POOL_BUF - 1):
        nb_ref[j] = buf_ref[j + 1]
    nb_ref[POOL_BUF - 1] = u_all


def _pool_decode(u, buf_t, w_pool, scale):
    n = u.shape[0]
    return pl.pallas_call(
        _pool_decode_kernel,
        out_shape=(jax.ShapeDtypeStruct((n, POOL_W), BF16),
                   jax.ShapeDtypeStruct((POOL_BUF, n, POOL_W), F32)),
        grid=(1,),
        in_specs=[_const_spec((n, POOL_W)), _const_spec((POOL_BUF, n, POOL_W)),
                  _const_spec((len(POOL_WINDOWS), POOL_G, POOL_G)), _const_spec((1, POOL_W))],
        out_specs=(pl.BlockSpec((n, POOL_W), lambda i: (0, 0)),
                   pl.BlockSpec((POOL_BUF, n, POOL_W), lambda i: (0, 0, 0))),
        compiler_params=_params(("arbitrary",)),
        name="pool_decode",
    )(u, buf_t, w_pool, scale)


COL_M, COL_INTER, COL_EINV, COL_END, COL_DECAY = (i * HEADS for i in range(5))


def _scan_lanes(x, op, fill):
    t_len = x.shape[-1]
    lane = lax.broadcasted_iota(jnp.int32, x.shape, 1)
    k = 1
    while k < t_len:
        shifted = pltpu.roll(x, k, axis=1)
        x = op(x, jnp.where(lane >= k, shifted, fill))
        k *= 2
    return x


def _gates_kernel(gift_ref, bif_ref, row_ref, col_ref, mlast_ref, pack_ref):
    t_len = gift_ref.shape[1]
    L = CHUNK
    g = gift_ref[...] + bif_ref[...]
    ig = g[0:HEADS, :]
    lf = _log_sigmoid(g[HEADS:2 * HEADS, :])
    b_cum = _scan_lanes(lf, jnp.add, 0.0)
    a = ig - b_cum
    m_run = jnp.maximum(_scan_lanes(a, jnp.maximum, NEG), 0.0)
    m_tot = b_cum + m_run
    row_ref[0, 0:HEADS, :] = a
    row_ref[0, HEADS:2 * HEADS, :] = m_run
    mlast_ref[0] = m_tot[:, t_len - 1:t_len]
    pack_ref[...] = jnp.zeros(pack_ref.shape, F32)
    pack_ref[COL_M:COL_M + HEADS, :] = m_run
    pack_ref[COL_EINV:COL_EINV + HEADS, :] = jnp.exp(-m_tot)
    for c in range(t_len // L):
        lo, hi = c * L, (c + 1) * L
        m_prev = jnp.zeros((HEADS, 1), F32) if c == 0 else m_run[:, lo - 1:lo]
        m_end = m_run[:, hi - 1:hi]
        pack_ref[COL_INTER:COL_INTER + HEADS, lo:hi] = jnp.exp(m_prev - m_run[:, lo:hi])
        pack_ref[COL_END:COL_END + HEADS, lo:hi] = jnp.exp(a[:, lo:hi] - m_end)
        pack_ref[COL_DECAY:COL_DECAY + HEADS, lo:hi] = jnp.broadcast_to(
            jnp.exp(m_prev - m_end), (HEADS, L))
    for c in range(t_len // 128):
        col_ref[0, c * 128:(c + 1) * 128, :] = pack_ref[:, c * 128:(c + 1) * 128].T


def _gates(gift, bif, batch, t_len):
    return pl.pallas_call(
        _gates_kernel,
        out_shape=(jax.ShapeDtypeStruct((batch, 2 * HEADS, t_len), F32),
                   jax.ShapeDtypeStruct((batch, t_len, 128), F32),
                   jax.ShapeDtypeStruct((batch, HEADS, 1), F32)),
        grid=(batch,),
        in_specs=[pl.BlockSpec((2 * HEADS, t_len), lambda b: (0, b)), _const_spec((2 * HEADS, 1))],
        out_specs=(pl.BlockSpec((1, 2 * HEADS, t_len), lambda b: (b, 0, 0)),
                   pl.BlockSpec((1, t_len, 128), lambda b: (b, 0, 0)),
                   pl.BlockSpec((1, HEADS, 1), lambda b: (b, 0, 0))),
        scratch_shapes=[pltpu.VMEM((128, t_len), F32)],
        compiler_params=_params(("parallel",)),
        name="mlstm_gates",
    )(gift, bif)


def _mlstm_prompt_kernel(q_ref, k_ref, v_ref, og_ref, row_ref, col_ref, y_ref, c_ref, n_ref):
    c_idx = pl.program_id(1)
    L = q_ref.shape[1]

    @pl.when(c_idx == 0)
    def _():
        c_ref[...] = jnp.zeros(c_ref.shape, F32)
        n_ref[...] = jnp.zeros(n_ref.shape, F32)

    t_idx = lax.broadcasted_iota(jnp.int32, (L, L), 0)
    s_idx = lax.broadcasted_iota(jnp.int32, (L, L), 1)
    causal = s_idx <= t_idx
    for b in range(q_ref.shape[0]):
        for h in range(HEADS):
            hs = slice(h * DH, (h + 1) * DH)
            qh = q_ref[b, :, hs]
            kh = k_ref[b, :, hs]
            vh = v_ref[b, :, hs]
            a_row = row_ref[b, h:h + 1, :]
            m_col = col_ref[b, :, COL_M + h:COL_M + h + 1]
            w_inter = col_ref[b, :, COL_INTER + h:COL_INTER + h + 1]
            einv = col_ref[b, :, COL_EINV + h:COL_EINV + h + 1]
            w_end = col_ref[b, :, COL_END + h:COL_END + h + 1]
            decay = col_ref[b, 0:1, COL_DECAY + h:COL_DECAY + h + 1]

            dmat = jnp.exp(jnp.where(causal, a_row - m_col, NEG))
            sw = _dot_nt(qh, kh) * dmat
            c_old = c_ref[b, h]
            n_old = n_ref[b, h]
            inter = _dot_nt(qh, c_old.astype(BF16))
            num = _dot(sw.astype(BF16), vh) + w_inter * inter
            nq = jnp.sum(qh.astype(F32) * n_old, axis=-1, keepdims=True)
            den = jnp.sum(sw, axis=-1, keepdims=True) + w_inter * nq
            r = 1.0 / jnp.maximum(jnp.abs(den), einv)
            y_ref[b, :, hs] = (og_ref[b, :, hs] * (num * r)).astype(y_ref.dtype)

            vw = (vh.astype(F32) * w_end).astype(BF16)
            c_ref[b, h] = decay * c_old + _dot_tn(vw, kh)
            n_ref[b, h] = decay * n_old + jnp.sum(kh.astype(F32) * w_end, axis=0, keepdims=True)


def _mlstm_prompt(q, k, v, og, rowp, colp, batch, t_len):
    L = CHUNK
    nc = t_len // L
    gb = MLSTM_GROUP
    q3, k3, v3, og3 = (a.reshape(batch, t_len, D) for a in (q, k, v, og))
    tok = lambda g, c: (g, c, 0)
    y, c_fin, n_fin = pl.pallas_call(
        _mlstm_prompt_kernel,
        out_shape=(jax.ShapeDtypeStruct((batch, t_len, D), BF16),
                   jax.ShapeDtypeStruct((batch, HEADS, DH, DH), F32),
                   jax.ShapeDtypeStruct((batch, HEADS, 1, DH), F32)),
        grid=(batch // gb, nc),
        in_specs=[pl.BlockSpec((gb, L, D), tok), pl.BlockSpec((gb, L, D), tok),
                  pl.BlockSpec((gb, L, D), tok), pl.BlockSpec((gb, L, D), tok),
                  pl.BlockSpec((gb, 2 * HEADS, L), lambda g, c: (g, 0, c)),
                  pl.BlockSpec((gb, L, 128), tok)],
        out_specs=(pl.BlockSpec((gb, L, D), tok),
                   pl.BlockSpec((gb, HEADS, DH, DH), lambda g, c: (g, 0, 0, 0)),
                   pl.BlockSpec((gb, HEADS, 1, DH), lambda g, c: (g, 0, 0, 0))),
        compiler_params=_params(("parallel", "arbitrary")),
        name="mlstm_prompt",
    )(q3, k3, v3, og3, rowp, colp)
    return y.reshape(batch * t_len, D), c_fin, n_fin


def _mlstm_decode_body(q_ref, k_ref, vt_ref, ogt_ref, gif_ref, bif_ref, c0_ref, n0_ref, m0_ref,
                       ht_ref, c_out_ref, n_out_ref, m_out_ref):
    for j in range(q_ref.shape[0]):
        gi = gif_ref[j] + bif_ref[...]
        m0 = m0_ref[j]
        for h in range(HEADS):
            hs = slice(h * DH, (h + 1) * DH)
            qr = q_ref[j, :, hs]
            kr = k_ref[j, :, hs]
            vc = vt_ref[0, hs, j:j + 1]
            ig = gi[:, h:h + 1]
            lf = _log_sigmoid(gi[:, HEADS + h:HEADS + h + 1])
            m_old = m0[:, h:h + 1]
            m_new = jnp.maximum(lf + m_old, ig)
            w_i = jnp.exp(ig - m_new)
            w_f = jnp.exp(lf + m_old - m_new)
            c_old = c0_ref[j, h]
            n_old = n0_ref[j, h:h + 1, :]
            cq = jnp.sum(c_old * qr, axis=-1, keepdims=True)
            qk = jnp.sum(qr * kr, axis=-1, keepdims=True)
            nq = jnp.sum(n_old * qr, axis=-1, keepdims=True)
            num = (w_i * qk) * vc + w_f * cq
            den = w_i * qk + w_f * nq
            hcol = num / jnp.maximum(jnp.abs(den), jnp.exp(-m_new))
            ht_ref[0, hs, j:j + 1] = ogt_ref[0, hs, j:j + 1] * hcol
            c_out_ref[j, h] = w_f * c_old + (w_i * vc) * kr
            n_out_ref[j, h:h + 1, :] = w_f * n_old + w_i * kr
            m_out_ref[j, :, h:h + 1] = m_new


def _mlstm_decode_operands(q, k, v, og, gifc, bif_row, c0, n0, m0, bb):
    n = q.shape[0]
    r3 = lambda i: (i, 0, 0)
    r4 = lambda i: (i, 0, 0, 0)
    q3, k3 = (a.reshape(n, 1, D) for a in (q, k))
    vt, ogt = (jnp.transpose(a.reshape(n // bb, bb, D), (0, 2, 1)) for a in (v, og))
    args = (q3, k3, vt, ogt, gifc.reshape(n, 1, 2 * HEADS), bif_row, c0, n0,
            m0.reshape(n, 1, HEADS))
    in_specs = [pl.BlockSpec((bb, 1, D), r3), pl.BlockSpec((bb, 1, D), r3),
                pl.BlockSpec((1, D, bb), r3), pl.BlockSpec((1, D, bb), r3),
                pl.BlockSpec((bb, 1, 2 * HEADS), r3), _const_spec((1, 2 * HEADS)),
                pl.BlockSpec((bb, HEADS, DH, DH), r4), pl.BlockSpec((bb, HEADS, DH), r3),
                pl.BlockSpec((bb, 1, HEADS), r3)]
    out_shape = (jax.ShapeDtypeStruct((n // bb, D, bb), F32),
                 jax.ShapeDtypeStruct((n, HEADS, DH, DH), F32),
                 jax.ShapeDtypeStruct((n, HEADS, DH), F32),
                 jax.ShapeDtypeStruct((n, 1, HEADS), F32))
    out_specs = (pl.BlockSpec((1, D, bb), r3), pl.BlockSpec((bb, HEADS, DH, DH), r4),
                 pl.BlockSpec((bb, HEADS, DH), r3), pl.BlockSpec((bb, 1, HEADS), r3))
    return args, in_specs, out_shape, out_specs


def _mlstm_decode_results(ht, c_new, n_new, m_new):
    nblk, _, bb = ht.shape
    y = jnp.transpose(ht, (0, 2, 1)).reshape(nblk * bb, D).astype(BF16)
    return y, c_new, n_new, m_new.reshape(nblk * bb, HEADS)


def _xattn_prompt_kernel(xq_ref, mk_ref, mv_ref, y_ref):
    scale = XDH ** -0.5
    for h in range(XH):
        hs = slice(h * XDH, (h + 1) * XDH)
        s = _dot_nt(xq_ref[:, hs], mk_ref[:, hs].astype(BF16)) * scale
        p = jnp.exp(s - jnp.max(s, axis=-1, keepdims=True))
        l = jnp.sum(p, axis=-1, keepdims=True)
        o = _dot(p.astype(BF16), mv_ref[:, hs].astype(BF16)) / l
        y_ref[:, hs] = o.astype(y_ref.dtype)


def _xattn_prompt(xq, mk, mv, batch, t_len):
    tq = TM_PROMPT
    nq = t_len // tq
    return pl.pallas_call(
        _xattn_prompt_kernel,
        out_shape=jax.ShapeDtypeStruct((batch * t_len, XW), BF16),
        grid=(batch, nq),
        in_specs=[pl.BlockSpec((tq, XW), lambda b, i: (b * nq + i, 0)),
                  pl.BlockSpec((N_MEM, XW), lambda b, i: (b, 0)),
                  pl.BlockSpec((N_MEM, XW), lambda b, i: (b, 0))],
        out_specs=pl.BlockSpec((tq, XW), lambda b, i: (b * nq + i, 0)),
        compiler_params=_params(("parallel", "arbitrary")),
        name="xattn_prompt",
    )(xq, mk, mv)


def _xattn_decode_kernel(q_ref, mk_ref, mv_ref, y_ref):
    scale = XDH ** -0.5
    q = q_ref[...][:, None, :, :]
    s = jnp.sum(mk_ref[...] * q, axis=-1, keepdims=True) * scale
    mx = jnp.max(s, axis=1, keepdims=True)
    mx = jnp.maximum(mx, pltpu.roll(mx, XH, axis=2))
    p = jnp.exp(s - mx)
    l = jnp.sum(p, axis=1, keepdims=True)
    l = l + pltpu.roll(l, XH, axis=2)
    o = jnp.sum(p * mv_ref[...], axis=1, keepdims=True)
    o = (o + pltpu.roll(o, XH, axis=2)) / l
    y_ref[...] = o[:, 0, 0:XH, :].astype(y_ref.dtype)


def _xattn_decode_operands(xq, mk, mv, bb):
    n = xq.shape[0]
    xq4 = xq.reshape(n, XH, XDH)
    q2 = jnp.concatenate([xq4, xq4], axis=1)
    mk2, mv2 = (a.reshape(n, N_MEM // 2, 2 * XH, XDH) for a in (mk, mv))
    kv_spec = pl.BlockSpec((bb, N_MEM // 2, 2 * XH, XDH), lambda i: (i, 0, 0, 0))
    in_specs = [pl.BlockSpec((bb, 2 * XH, XDH), lambda i: (i, 0, 0)), kv_spec, kv_spec]
    out_shape = jax.ShapeDtypeStruct((n, XH, XDH), BF16)
    out_spec = pl.BlockSpec((bb, XH, XDH), lambda i: (i, 0, 0))
    return (q2, mk2, mv2), in_specs, out_shape, out_spec


N_MERGE_IN = 11
N_DEC_IN = 9


def _merge_dec_kernel(*refs):
    host_in = refs[:N_MERGE_IN]
    dec_in = refs[N_MERGE_IN:N_MERGE_IN + N_DEC_IN]
    o_ref = refs[N_MERGE_IN + N_DEC_IN]
    dec_out = refs[N_MERGE_IN + N_DEC_IN + 1:-1]
    _mlstm_decode_body(*dec_in, *dec_out)
    _merge_kernel(*host_in, o_ref, refs[-1])


def _merge_kernel(x_ref, yp_ref, ym_ref, yx_ref, gpre_ref, gpost_ref, wg_ref, wbp_ref, wbm_ref,
                  wbx_ref, wo_ref, o_ref, merged_ref):
    x = x_ref[...]
    h = _rms(x, gpre_ref[...]).astype(BF16)
    yp = yp_ref[...]
    ym = ym_ref[...]
    yx = yx_ref[...]
    nchunk = 256
    for c in range(D // nchunk):
        cs = slice(c * nchunk, (c + 1) * nchunk)
        acc = _sigmoid(_dot(h, wg_ref[:, c * nchunk:(c + 1) * nchunk])) * _dot(yp, wbp_ref[:, cs])
        acc += (_sigmoid(_dot(h, wg_ref[:, D + c * nchunk:D + (c + 1) * nchunk]))
                * _dot(ym, wbm_ref[:, cs]))
        acc += (_sigmoid(_dot(h, wg_ref[:, 2 * D + c * nchunk:2 * D + (c + 1) * nchunk]))
                * _dot(yx, wbx_ref[:, cs]))
        merged_ref[:, cs] = acc.astype(BF16)
    o_ref[...] = x + _rms(_dot(merged_ref[...], wo_ref[...]), gpost_ref[...])


def _merge(x2, yp, ym, yx, gpre, gpost, wg, wbp, wbm, wbx, wo, tm, dec=None):
    n = x2.shape[0]
    steps = n // tm
    row = lambda i: (i, 0)
    args = (x2, yp, ym, yx, gpre, gpost, wg, wbp, wbm, wbx, wo)
    in_specs = [pl.BlockSpec((tm, D), row), pl.BlockSpec((tm, POOL_W), row),
                pl.BlockSpec((tm, D), row), pl.BlockSpec((tm, XW), row),
                _const_spec((1, D)), _const_spec((1, D)), _const_spec((D, 3 * D)),
                _const_spec((POOL_W, D)), _const_spec((D, D)), _const_spec((XW, D)),
                _const_spec((D, D))]
    assert len(args) == N_MERGE_IN
    out_shape = jax.ShapeDtypeStruct((n, D), F32)
    out_specs = pl.BlockSpec((tm, D), row)
    body = _merge_kernel
    if dec is not None:
        n_dec = dec[0].shape[0]
        d_args, d_in, d_shape, d_out = _mlstm_decode_operands(*dec, n_dec // steps)
        assert len(d_args) == N_DEC_IN
        args, in_specs = args + d_args, in_specs + d_in
        out_shape, out_specs = (out_shape,) + d_shape, (out_specs,) + d_out
        body = _merge_dec_kernel
    return pl.pallas_call(
        body,
        out_shape=out_shape,
        grid=(steps,),
        in_specs=in_specs,
        out_specs=out_specs,
        scratch_shapes=[pltpu.VMEM((tm, D), BF16)],
        compiler_params=_params(("parallel",)),
        name="merge_out",
    )(*args)


def _ffn_kernel(x_ref, gpre_ref, gpost_ref, w1_ref, w2_ref, o_ref):
    x = x_ref[...]
    h = _rms(x, gpre_ref[...]).astype(BF16)
    fchunk = 1024
    acc = None
    for c in range(D_FF // fchunk):
        a = _dot(h, w1_ref[:, c * fchunk:(c + 1) * fchunk])
        a = jnp.square(jnp.maximum(a, 0.0)).astype(BF16)
        part = _dot(a, w2_ref[c * fchunk:(c + 1) * fchunk, :])
        acc = part if acc is None else acc + part
    o_ref[...] = x + _rms(acc, gpost_ref[...])


def _ffn(x2, gpre, gpost, w1, w2, tm):
    n = x2.shape[0]
    row = lambda i: (i, 0)
    return pl.pallas_call(
        _ffn_kernel,
        out_shape=jax.ShapeDtypeStruct((n, D), F32),
        grid=(n // tm,),
        in_specs=[pl.BlockSpec((tm, D), row), _const_spec((1, D)), _const_spec((1, D)),
                  _const_spec((D, D_FF)), _const_spec((D_FF, D))],
        out_specs=pl.BlockSpec((tm, D), row),
        compiler_params=_params(("parallel",)),
        name="ffn",
    )(x2, gpre, gpost, w1, w2)


def kernel(x_prompt, x_sample, mem_prompt, state_pool_buf, state_mlstm_C, state_mlstm_n, state_mlstm_m, cache_mem_k, cache_mem_v, g_pre_mix, w_in, b_if, w_pool, pool_scale, g_mem, w_mem_kv, w_br_pool, w_br_mlstm, w_br_xattn, w_out, g_post_mix, g_pre_mlp, w_ff1, w_ff2, g_post_mlp):
    batch, t_len, _ = x_prompt.shape
    n_dec = x_sample.shape[0]
    assert w_in.shape[0] == 1, "single layer"

    w_pool_b = w_pool[0].astype(BF16)
    w_kv_b = w_mem_kv[0].astype(BF16)
    g_mix, g_pm, g_mlp, g_pmlp, g_m = (g[0].reshape(1, D) for g in
                                       (g_pre_mix, g_post_mix, g_pre_mlp, g_post_mlp, g_mem))
    scale = pool_scale[0].reshape(1, POOL_W)
    bif = b_if[0]

    mk_p, mv_p, w_in_b, w_gate = _mem_kv(mem_prompt.reshape(batch * N_MEM, D), g_m, w_kv_b, w_in[0])

    xs = x_sample.reshape(n_dec, D)
    u_s, q_s, k_s, v_s, og_s, xq_s, _, gifc_s = _in_proj(xs, g_mix, w_in_b, n_dec, F32)
    buf_t = jnp.transpose(state_pool_buf[0], (1, 0, 2))
    ypool_s, buf_s_t = _pool_decode(u_s, buf_t, w_pool_b, scale)
    dec_xattn = (xq_s, cache_mem_k[0], cache_mem_v[0])
    dec_mlstm = (q_s, k_s, v_s, og_s, gifc_s, bif.reshape(1, 2 * HEADS),
                 state_mlstm_C[0], state_mlstm_n[0], state_mlstm_m[0])

    xp = x_prompt.reshape(batch * t_len, D)
    later_weights = (w_ff1[0], w_ff2[0], w_out[0], w_br_pool[0], w_br_mlstm[0], w_br_xattn[0])
    u, q, k, v, og, xq, gift, _, yx_s, w1, w2, wo, wbp, wbm, wbx = _in_proj(
        xp, g_mix, w_in_b, TM_INPROJ, BF16, dec=dec_xattn, convert=later_weights)
    yx_s = yx_s.reshape(n_dec, XW)
    y_pool, buf_p = _pool_prompt(u, w_pool_b, scale, batch, t_len)
    rowp, colp, m_last = _gates(gift, bif.reshape(2 * HEADS, 1), batch, t_len)
    y_ml, c_p, n_p = _mlstm_prompt(q, k, v, og, rowp, colp, batch, t_len)
    y_x = _xattn_prompt(xq, mk_p, mv_p, batch, t_len)
    x1, *dec_out = _merge(xp, y_pool, y_ml, y_x, g_mix, g_pm, w_gate, wbp, wbm, wbx, wo,
                          TM_MERGE, dec=dec_mlstm)
    yp = _ffn(x1, g_mlp, g_pmlp, w1, w2, TM_PROMPT)

    yml_s, c_s, n_s, m_s = _mlstm_decode_results(*dec_out)
    x1_s = _merge(xs, ypool_s, yml_s, yx_s, g_mix, g_pm, w_gate, wbp, wbm, wbx, wo, n_dec)
    ys = _ffn(x1_s, g_mlp, g_pmlp, w1, w2, n_dec)

    return (yp.reshape(batch, t_len, D), ys.reshape(n_dec, 1, D),
            buf_p[None], c_p[None], n_p.reshape(1, batch, HEADS, DH),
            m_last.reshape(1, batch, HEADS),
            mk_p.reshape(1, batch, N_MEM, XH, XDH), mv_p.reshape(1, batch, N_MEM, XH, XDH),
            jnp.transpose(buf_s_t, (1, 0, 2))[None], c_s[None], n_s[None], m_s[None])
```

```python
import functools

import jax
import jax.numpy as jnp
from jax import lax
from jax.experimental import pallas as pl
from jax.experimental.pallas import tpu as pltpu

F32 = jnp.float32
BF16 = jnp.bfloat16

D = 1024
POOL_W = 512
POOL_G = 128
POOL_WINDOWS = (2, 4, 8, 16)
POOL_BUF = 15
HEADS = 4
DH = 256
XH = 4
XDH = 128
XW = 512
N_MEM = 256
D_FF = 4096
EPS = 1e-6
NEG = -1e30
PAST_LEN = 16384

TM_PROMPT = 1024
TM_INPROJ = 512
TM_MERGE = 512
CHUNK = 256
MLSTM_GROUP = 4

VMEM_LIMIT = 56 * 1024 * 1024

O_QKVO = POOL_W
O_GIF = O_QKVO + 4 * D
O_XQ = O_GIF + 2 * HEADS
O_GATE = O_XQ + XW


def _params(sem):
    return pltpu.CompilerParams(dimension_semantics=sem, vmem_limit_bytes=VMEM_LIMIT)


def _const_spec(shape):
    nd = len(shape)
    return pl.BlockSpec(shape, lambda *_: (0,) * nd, pipeline_mode=pl.Buffered(1))


def _rms(x, g):
    ms = jnp.mean(x * x, axis=-1, keepdims=True)
    return x * lax.rsqrt(ms + EPS) * g


def _log_sigmoid(x):
    return jnp.minimum(x, 0.0) - jnp.log(1.0 + jnp.exp(-jnp.abs(x)))


def _sigmoid(x):
    return 1.0 / (1.0 + jnp.exp(-x))


def _dot(a, b):
    return jnp.dot(a, b, preferred_element_type=F32)


def _dot_nt(a, b):
    return lax.dot_general(a, b, (((1,), (1,)), ((), ())), preferred_element_type=F32)


def _dot_tn(a, b):
    return lax.dot_general(a, b, (((0,), (0,)), ((), ())), preferred_element_type=F32)


MID_COLS = 640
N_INPROJ_IN = 4
N_INPROJ_OUT = 8


def _in_proj_dec_kernel(n_conv, *refs):
    n_in = N_INPROJ_IN + 3 + n_conv
    host_in = refs[:N_INPROJ_IN]
    dec_in = refs[N_INPROJ_IN:N_INPROJ_IN + 3]
    conv_in = refs[N_INPROJ_IN + 3:n_in]
    host_out = refs[n_in:n_in + N_INPROJ_OUT]
    dec_out = refs[n_in + N_INPROJ_OUT]
    conv_out = refs[n_in + N_INPROJ_OUT + 1:]
    for src, dst in zip(conv_in, conv_out):
        dst[...] = src[...].astype(dst.dtype)
    _xattn_decode_kernel(*dec_in, dec_out)
    _in_proj_kernel(*host_in, *host_out)


def _in_proj_kernel(x_ref, g_ref, w_ref, wmid_ref, u_ref, q_ref, k_ref, v_ref, og_ref,
                    xq_ref, gift_ref, gifc_ref):
    h = _rms(x_ref[...], g_ref[...]).astype(BF16)

    def seg(lo, n):
        return _dot(h, w_ref[:, lo:lo + n])

    u_ref[...] = seg(0, POOL_W)
    q_ref[...] = seg(POOL_W, D).astype(q_ref.dtype)
    k_ref[...] = (seg(POOL_W + D, D) * (DH ** -0.5)).astype(k_ref.dtype)
    v_ref[...] = seg(POOL_W + 2 * D, D).astype(v_ref.dtype)
    og_ref[...] = _sigmoid(seg(POOL_W + 3 * D, D))
    mid = _dot(h, wmid_ref[:, 0:MID_COLS])
    xq_lo = O_XQ - O_GIF
    xq_ref[...] = mid[:, xq_lo:xq_lo + XW].astype(xq_ref.dtype)
    gifc_ref[...] = mid[:, 0:2 * HEADS]
    for r in range(mid.shape[0] // 128):
        gift_ref[:, r * 128:(r + 1) * 128] = mid[r * 128:(r + 1) * 128, 0:128].T[0:2 * HEADS, :]


XQ_WINDOW = 768


def _in_proj(x2, g, w_in_b, tm, qkv_dtype, dec=None, convert=()):
    n = x2.shape[0]
    steps = n // tm
    row = lambda i: (i, 0)
    assert O_GIF % XQ_WINDOW == 0 and O_XQ + XW - O_GIF <= MID_COLS <= XQ_WINDOW
    out_shape = (
        jax.ShapeDtypeStruct((n, POOL_W), F32),
        jax.ShapeDtypeStruct((n, D), qkv_dtype),
        jax.ShapeDtypeStruct((n, D), qkv_dtype),
        jax.ShapeDtypeStruct((n, D), qkv_dtype),
        jax.ShapeDtypeStruct((n, D), F32),
        jax.ShapeDtypeStruct((n, XW), qkv_dtype),
        jax.ShapeDtypeStruct((2 * HEADS, n), F32),
        jax.ShapeDtypeStruct((n, 2 * HEADS), F32),
    )
    args = (x2, g, w_in_b, w_in_b)
    in_specs = [
        pl.BlockSpec((tm, D), row),
        _const_spec((1, D)),
        _const_spec((D, O_GIF)),
        pl.BlockSpec((D, XQ_WINDOW), lambda i: (0, O_GIF // XQ_WINDOW),
                     pipeline_mode=pl.Buffered(1)),
    ]
    out_specs = (
        pl.BlockSpec((tm, POOL_W), row),
        pl.BlockSpec((tm, D), row),
        pl.BlockSpec((tm, D), row),
        pl.BlockSpec((tm, D), row),
        pl.BlockSpec((tm, D), row),
        pl.BlockSpec((tm, XW), row),
        pl.BlockSpec((2 * HEADS, tm), lambda i: (0, i)),
        pl.BlockSpec((tm, 2 * HEADS), row),
    )
    assert len(args) == N_INPROJ_IN and len(out_shape) == N_INPROJ_OUT
    body = _in_proj_kernel
    if dec is not None:
        d_args, d_in, d_shape, d_out = _xattn_decode_operands(*dec, dec[0].shape[0] // steps)
        args, in_specs = args + d_args, in_specs + d_in
        out_shape, out_specs = out_shape + (d_shape,), out_specs + (d_out,)
        for w in convert:
            spec = pl.BlockSpec((w.shape[0] // steps, w.shape[1]), row)
            args, in_specs = args + (w,), in_specs + [spec]
            out_shape += (jax.ShapeDtypeStruct(w.shape, BF16),)
            out_specs += (spec,)
        body = functools.partial(_in_proj_dec_kernel, len(convert))
    return pl.pallas_call(
        body,
        out_shape=out_shape,
        grid=(steps,),
        in_specs=in_specs,
        out_specs=out_specs,
        compiler_params=_params(("parallel",)),
        name="in_proj",
    )(*args)


def _mem_kv_kernel(mem_ref, g_ref, w_ref, k_ref, v_ref):
    h = _rms(mem_ref[...], g_ref[...]).astype(BF16)
    kv = _dot(h, w_ref[...])
    k_ref[...] = kv[:, :XW]
    v_ref[...] = kv[:, XW:]


def _mem_kv(mem2, g, w):
    n = mem2.shape[0]
    tm = TM_PROMPT
    row = lambda i: (i, 0)
    return pl.pallas_call(
        _mem_kv_kernel,
        out_shape=(jax.ShapeDtypeStruct((n, XW), F32), jax.ShapeDtypeStruct((n, XW), F32)),
        grid=(n // tm,),
        in_specs=[pl.BlockSpec((tm, D), row), _const_spec((1, D)), _const_spec((D, 2 * XW))],
        out_specs=(pl.BlockSpec((tm, XW), row), pl.BlockSpec((tm, XW), row)),
        compiler_params=_params(("parallel",)),
        name="mem_kv",
    )(mem2, g, w)


def _pool_prompt_kernel(u_ref, w_ref, s_ref, y_ref, nb_ref):
    t_len = u_ref.shape[0]
    t_idx = lax.broadcasted_iota(jnp.int32, (t_len, POOL_G), 0)
    for g, win in enumerate(POOL_WINDOWS):
        cols = slice(g * POOL_G, (g + 1) * POOL_G)
        u = u_ref[:, cols]
        acc = u
        span = 1
        while span < win:
            shifted = pltpu.roll(acc, span, axis=0)
            acc = acc + jnp.where(t_idx >= span, shifted, 0.0)
            span *= 2
        cnt = jnp.minimum(t_idx + 1, win).astype(F32)
        d = (acc / cnt - u).astype(BF16)
        y = _dot(d, w_ref[g]) * s_ref[:, cols]
        y_ref[:, cols] = y.astype(y_ref.dtype)
    nb_ref[0] = u_ref[t_len - POOL_BUF:, :]


def _pool_prompt(u, w_pool, scale, batch, t_len):
    return pl.pallas_call(
        _pool_prompt_kernel,
        out_shape=(jax.ShapeDtypeStruct((batch * t_len, POOL_W), BF16),
                   jax.ShapeDtypeStruct((batch, POOL_BUF, POOL_W), F32)),
        grid=(batch,),
        in_specs=[pl.BlockSpec((t_len, POOL_W), lambda b: (b, 0)),
                  _const_spec((len(POOL_WINDOWS), POOL_G, POOL_G)),
                  _const_spec((1, POOL_W))],
        out_specs=(pl.BlockSpec((t_len, POOL_W), lambda b: (b, 0)),
                   pl.BlockSpec((1, POOL_BUF, POOL_W), lambda b: (b, 0, 0))),
        compiler_params=_params(("parallel",)),
        name="pool_prompt",
    )(u, w_pool, scale)


def _pool_decode_kernel(u_ref, buf_ref, w_ref, s_ref, y_ref, nb_ref):
    u_all = u_ref[...]
    for g, win in enumerate(POOL_WINDOWS):
        cols = slice(g * POOL_G, (g + 1) * POOL_G)
        u = u_all[:, cols]
        acc = u
        for j in range(POOL_BUF - (win - 1), POOL_BUF):
            acc = acc + buf_ref[j, :, cols]
        cnt = float(min(win, PAST_LEN + 1))
        d = (acc / cnt - u).astype(BF16)
        y = _dot(d, w_ref[g]) * s_ref[:, cols]
        y_ref[:, cols] = y.astype(y_ref.dtype)
    for j in range(POOL_BUF - 1):
        nb_ref[j] = buf_ref[j + 1]
    nb_ref[POOL_BUF - 1] = u_all


def _pool_decode(u, buf_t, w_pool, scale):
    n = u.shape[0]
    return pl.pallas_call(
        _pool_decode_kernel,
        out_shape=(jax.ShapeDtypeStruct((n, POOL_W), BF16),
                   jax.ShapeDtypeStruct((POOL_BUF, n, POOL_W), F32)),
        grid=(1,),
        in_specs=[_const_spec((n, POOL_W)), _const_spec((POOL_BUF, n, POOL_W)),
                  _const_spec((len(POOL_WINDOWS), POOL_G, POOL_G)), _const_spec((1, POOL_W))],
        out_specs=(pl.BlockSpec((n, POOL_W), lambda i: (0, 0)),
                   pl.BlockSpec((POOL_BUF, n, POOL_W), lambda i: (0, 0, 0))),
        compiler_params=_params(("arbitrary",)),
        name="pool_decode",
    )(u, buf_t, w_pool, scale)


COL_M, COL_INTER, COL_EINV, COL_END, COL_DECAY = (i * HEADS for i in range(5))


def _scan_lanes(x, op, fill):
    t_len = x.shape[-1]
    lane = lax.broadcasted_iota(jnp.int32, x.shape, 1)
    k = 1
    while k < t_len:
        shifted = pltpu.roll(x, k, axis=1)
        x = op(x, jnp.where(lane >= k, shifted, fill))
        k *= 2
    return x


def _gates_kernel(gift_ref, bif_ref, row_ref, col_ref, mlast_ref, pack_ref):
    t_len = gift_ref.shape[1]
    L = CHUNK
    g = gift_ref[...] + bif_ref[...]
    ig = g[0:HEADS, :]
    lf = _log_sigmoid(g[HEADS:2 * HEADS, :])
    b_cum = _scan_lanes(lf, jnp.add, 0.0)
    a = ig - b_cum
    m_run = jnp.maximum(_scan_lanes(a, jnp.maximum, NEG), 0.0)
    m_tot = b_cum + m_run
    row_ref[0, 0:HEADS, :] = a
    row_ref[0, HEADS:2 * HEADS, :] = m_run
    mlast_ref[0] = m_tot[:, t_len - 1:t_len]
    pack_ref[...] = jnp.zeros(pack_ref.shape, F32)
    pack_ref[COL_M:COL_M + HEADS, :] = m_run
    pack_ref[COL_EINV:COL_EINV + HEADS, :] = jnp.exp(-m_tot)
    for c in range(t_len // L):
        lo, hi = c * L, (c + 1) * L
        m_prev = jnp.zeros((HEADS, 1), F32) if c == 0 else m_run[:, lo - 1:lo]
        m_end = m_run[:, hi - 1:hi]
        pack_ref[COL_INTER:COL_INTER + HEADS, lo:hi] = jnp.exp(m_prev - m_run[:, lo:hi])
        pack_ref[COL_END:COL_END + HEADS, lo:hi] = jnp.exp(a[:, lo:hi] - m_end)
        pack_ref[COL_DECAY:COL_DECAY + HEADS, lo:hi] = jnp.broadcast_to(
            jnp.exp(m_prev - m_end), (HEADS, L))
    for c in range(t_len // 128):
        col_ref[0, c * 128:(c + 1) * 128, :] = pack_ref[:, c * 128:(c + 1) * 128].T


def _gates(gift, bif, batch, t_len):
    return pl.pallas_call(
        _gates_kernel,
        out_shape=(jax.ShapeDtypeStruct((batch, 2 * HEADS, t_len), F32),
                   jax.ShapeDtypeStruct((batch, t_len, 128), F32),
                   jax.ShapeDtypeStruct((batch, HEADS, 1), F32)),
        grid=(batch,),
        in_specs=[pl.BlockSpec((2 * HEADS, t_len), lambda b: (0, b)), _const_spec((2 * HEADS, 1))],
        out_specs=(pl.BlockSpec((1, 2 * HEADS, t_len), lambda b: (b, 0, 0)),
                   pl.BlockSpec((1, t_len, 128), lambda b: (b, 0, 0)),
                   pl.BlockSpec((1, HEADS, 1), lambda b: (b, 0, 0))),
        scratch_shapes=[pltpu.VMEM((128, t_len), F32)],
        compiler_params=_params(("parallel",)),
        name="mlstm_gates",
    )(gift, bif)


def _mlstm_prompt_kernel(q_ref, k_ref, v_ref, og_ref, row_ref, col_ref, y_ref, c_ref, n_ref):
    c_idx = pl.program_id(1)
    L = q_ref.shape[1]

    @pl.when(c_idx == 0)
    def _():
        c_ref[...] = jnp.zeros(c_ref.shape, F32)
        n_ref[...] = jnp.zeros(n_ref.shape, F32)

    t_idx = lax.broadcasted_iota(jnp.int32, (L, L), 0)
    s_idx = lax.broadcasted_iota(jnp.int32, (L, L), 1)
    causal = s_idx <= t_idx
    for b in range(q_ref.shape[0]):
        for h in range(HEADS):
            hs = slice(h * DH, (h + 1) * DH)
            qh = q_ref[b, :, hs]
            kh = k_ref[b, :, hs]
            vh = v_ref[b, :, hs]
            a_row = row_ref[b, h:h + 1, :]
            m_col = col_ref[b, :, COL_M + h:COL_M + h + 1]
            w_inter = col_ref[b, :, COL_INTER + h:COL_INTER + h + 1]
            einv = col_ref[b, :, COL_EINV + h:COL_EINV + h + 1]
            w_end = col_ref[b, :, COL_END + h:COL_END + h + 1]
            decay = col_ref[b, 0:1, COL_DECAY + h:COL_DECAY + h + 1]

            dmat = jnp.exp(jnp.where(causal, a_row - m_col, NEG))
            sw = _dot_nt(qh, kh) * dmat
            c_old = c_ref[b, h]
            n_old = n_ref[b, h]
            inter = _dot_nt(qh, c_old.astype(BF16))
            num = _dot(sw.astype(BF16), vh) + w_inter * inter
            nq = jnp.sum(qh.astype(F32) * n_old, axis=-1, keepdims=True)
            den = jnp.sum(sw, axis=-1, keepdims=True) + w_inter * nq
            r = 1.0 / jnp.maximum(jnp.abs(den), einv)
            y_ref[b, :, hs] = (og_ref[b, :, hs] * (num * r)).astype(y_ref.dtype)

            vw = (vh.astype(F32) * w_end).astype(BF16)
            c_ref[b, h] = decay * c_old + _dot_tn(vw, kh)
            n_ref[b, h] = decay * n_old + jnp.sum(kh.astype(F32) * w_end, axis=0, keepdims=True)


def _mlstm_prompt(q, k, v, og, rowp, colp, batch, t_len):
    L = CHUNK
    nc = t_len // L
    gb = MLSTM_GROUP
    q3, k3, v3, og3 = (a.reshape(batch, t_len, D) for a in (q, k, v, og))
    tok = lambda g, c: (g, c, 0)
    y, c_fin, n_fin = pl.pallas_call(
        _mlstm_prompt_kernel,
        out_shape=(jax.ShapeDtypeStruct((batch, t_len, D), BF16),
                   jax.ShapeDtypeStruct((batch, HEADS, DH, DH), F32),
                   jax.ShapeDtypeStruct((batch, HEADS, 1, DH), F32)),
        grid=(batch // gb, nc),
        in_specs=[pl.BlockSpec((gb, L, D), tok), pl.BlockSpec((gb, L, D), tok),
                  pl.BlockSpec((gb, L, D), tok), pl.BlockSpec((gb, L, D), tok),
                  pl.BlockSpec((gb, 2 * HEADS, L), lambda g, c: (g, 0, c)),
                  pl.BlockSpec((gb, L, 128), tok)],
        out_specs=(pl.BlockSpec((gb, L, D), tok),
                   pl.BlockSpec((gb, HEADS, DH, DH), lambda g, c: (g, 0, 0, 0)),
                   pl.BlockSpec((gb, HEADS, 1, DH), lambda g, c: (g, 0, 0, 0))),
        compiler_params=_params(("parallel", "arbitrary")),
        name="mlstm_prompt",
    )(q3, k3, v3, og3, rowp, colp)
    return y.reshape(batch * t_len, D), c_fin, n_fin


def _mlstm_decode_body(q_ref, k_ref, v8_ref, og8_ref, gif_ref, bif_ref, c0_ref, n0_ref, m0_ref,
                       y8_ref, c_out_ref, n_out_ref, m_out_ref, vt_ref, ht_ref):
    vrep = jnp.concatenate([v8_ref[0]] * 16, axis=0)
    for r in range(D // 128):
        rs = slice(r * 128, (r + 1) * 128)
        vt_ref[rs, :] = vrep[:, rs].T
    ht_ref[...] = jnp.zeros(ht_ref.shape, F32)
    for j in range(q_ref.shape[0]):
        gi = gif_ref[j] + bif_ref[...]
        m0 = m0_ref[j]
        for h in range(HEADS):
            hs = slice(h * DH, (h + 1) * DH)
            qr = q_ref[j, :, hs]
            kr = k_ref[j, :, hs]
            vc = vt_ref[hs, j:j + 1]
            ig = gi[:, h:h + 1]
            lf = _log_sigmoid(gi[:, HEADS + h:HEADS + h + 1])
            m_old = m0[:, h:h + 1]
            m_new = jnp.maximum(lf + m_old, ig)
            w_i = jnp.exp(ig - m_new)
            w_f = jnp.exp(lf + m_old - m_new)
            c_old = c0_ref[j, h]
            n_old = n0_ref[j, h:h + 1, :]
            cq = jnp.sum(c_old * qr, axis=-1, keepdims=True)
            qk = jnp.sum(qr * kr, axis=-1, keepdims=True)
            nq = jnp.sum(n_old * qr, axis=-1, keepdims=True)
            num = (w_i * qk) * vc + w_f * cq
            den = w_i * qk + w_f * nq
            hcol = num / jnp.maximum(jnp.abs(den), jnp.exp(-m_new))
            ht_ref[hs, j:j + 1] = hcol
            c_out_ref[j, h] = w_f * c_old + (w_i * vc) * kr
            n_out_ref[j, h:h + 1, :] = w_f * n_old + w_i * kr
            m_out_ref[j, :, h:h + 1] = m_new
    for r in range(D // 128):
        rs = slice(r * 128, (r + 1) * 128)
        y8_ref[0, :, rs] = (og8_ref[0, :, rs] * ht_ref[rs, :].T[0:8, :]).astype(y8_ref.dtype)


MLSTM_DEC_SCRATCH = [pltpu.VMEM((D, 128), F32), pltpu.VMEM((D, 128), F32)]


def _mlstm_decode_operands(q, k, v, og, gifc, bif_row, c0, n0, m0, bb):
    n = q.shape[0]
    assert bb <= 8
    r3 = lambda i: (i, 0, 0)
    r4 = lambda i: (i, 0, 0, 0)
    q3, k3 = (a.reshape(n, 1, D) for a in (q, k))
    v8, og8 = (jnp.pad(a.reshape(n // bb, bb, D), ((0, 0), (0, 8 - bb), (0, 0))) for a in (v, og))
    args = (q3, k3, v8, og8, gifc.reshape(n, 1, 2 * HEADS), bif_row, c0, n0,
            m0.reshape(n, 1, HEADS))
    in_specs = [pl.BlockSpec((bb, 1, D), r3), pl.BlockSpec((bb, 1, D), r3),
                pl.BlockSpec((1, 8, D), r3), pl.BlockSpec((1, 8, D), r3),
                pl.BlockSpec((bb, 1, 2 * HEADS), r3), _const_spec((1, 2 * HEADS)),
                pl.BlockSpec((bb, HEADS, DH, DH), r4), pl.BlockSpec((bb, HEADS, DH), r3),
                pl.BlockSpec((bb, 1, HEADS), r3)]
    out_shape = (jax.ShapeDtypeStruct((n // bb, 8, D), BF16),
                 jax.ShapeDtypeStruct((n, HEADS, DH, DH), F32),
                 jax.ShapeDtypeStruct((n, HEADS, DH), F32),
                 jax.ShapeDtypeStruct((n, 1, HEADS), F32))
    out_specs = (pl.BlockSpec((1, 8, D), r3), pl.BlockSpec((bb, HEADS, DH, DH), r4),
                 pl.BlockSpec((bb, HEADS, DH), r3), pl.BlockSpec((bb, 1, HEADS), r3))
    return args, in_specs, out_shape, out_specs


def _mlstm_decode_results(y8, c_new, n_new, m_new):
    n = c_new.shape[0]
    bb = n // y8.shape[0]
    return y8[:, :bb].reshape(n, D), c_new, n_new, m_new.reshape(n, HEADS)


def _xattn_prompt_kernel(xq_ref, mk_ref, mv_ref, y_ref):
    scale = XDH ** -0.5
    for h in range(XH):
        hs = slice(h * XDH, (h + 1) * XDH)
        s = _dot_nt(xq_ref[:, hs], mk_ref[:, hs].astype(BF16)) * scale
        p = jnp.exp(s - jnp.max(s, axis=-1, keepdims=True))
        l = jnp.sum(p, axis=-1, keepdims=True)
        o = _dot(p.astype(BF16), mv_ref[:, hs].astype(BF16)) / l
        y_ref[:, hs] = o.astype(y_ref.dtype)


def _xattn_prompt(xq, mk, mv, batch, t_len):
    tq = TM_PROMPT
    nq = t_len // tq
    return pl.pallas_call(
        _xattn_prompt_kernel,
        out_shape=jax.ShapeDtypeStruct((batch * t_len, XW), BF16),
        grid=(batch, nq),
        in_specs=[pl.BlockSpec((tq, XW), lambda b, i: (b * nq + i, 0)),
                  pl.BlockSpec((N_MEM, XW), lambda b, i: (b, 0)),
                  pl.BlockSpec((N_MEM, XW), lambda b, i: (b, 0))],
        out_specs=pl.BlockSpec((tq, XW), lambda b, i: (b * nq + i, 0)),
        compiler_params=_params(("parallel", "arbitrary")),
        name="xattn_prompt",
    )(xq, mk, mv)


def _xattn_decode_kernel(q_ref, mk_ref, mv_ref, y_ref):
    scale = XDH ** -0.5
    q = q_ref[...][:, None, :, :]
    s = jnp.sum(mk_ref[...] * q, axis=-1, keepdims=True) * scale
    mx = jnp.max(s, axis=1, keepdims=True)
    mx = jnp.maximum(mx, pltpu.roll(mx, XH, axis=2))
    p = jnp.exp(s - mx)
    l = jnp.sum(p, axis=1, keepdims=True)
    l = l + pltpu.roll(l, XH, axis=2)
    o = jnp.sum(p * mv_ref[...], axis=1, keepdims=True)
    o = (o + pltpu.roll(o, XH, axis=2)) / l
    y_ref[...] = o[:, 0, 0:XH, :].astype(y_ref.dtype)


def _xattn_decode_operands(xq, mk, mv, bb):
    n = xq.shape[0]
    xq4 = xq.reshape(n, XH, XDH)
    q2 = jnp.concatenate([xq4, xq4], axis=1)
    mk2, mv2 = (a.reshape(n, N_MEM // 2, 2 * XH, XDH) for a in (mk, mv))
    kv_spec = pl.BlockSpec((bb, N_MEM // 2, 2 * XH, XDH), lambda i: (i, 0, 0, 0))
    in_specs = [pl.BlockSpec((bb, 2 * XH, XDH), lambda i: (i, 0, 0)), kv_spec, kv_spec]
    out_shape = jax.ShapeDtypeStruct((n, XH, XDH), BF16)
    out_spec = pl.BlockSpec((bb, XH, XDH), lambda i: (i, 0, 0))
    return (q2, mk2, mv2), in_specs, out_shape, out_spec


N_MERGE_IN = 11
N_DEC_IN = 9


def _merge_dec_kernel(*refs):
    host_in = refs[:N_MERGE_IN]
    dec_in = refs[N_MERGE_IN:N_MERGE_IN + N_DEC_IN]
    o_ref = refs[N_MERGE_IN + N_DEC_IN]
    n_scr = 1 + len(MLSTM_DEC_SCRATCH)
    dec_out = refs[N_MERGE_IN + N_DEC_IN + 1:-n_scr]
    merged_ref, *dec_scratch = refs[-n_scr:]
    _mlstm_decode_body(*dec_in, *dec_out, *dec_scratch)
    _merge_kernel(*host_in, o_ref, merged_ref)


def _merge_kernel(x_ref, yp_ref, ym_ref, yx_ref, gpre_ref, gpost_ref, wg_ref, wbp_ref, wbm_ref,
                  wbx_ref, wo_ref, o_ref, merged_ref):
    x = x_ref[...]
    h = _rms(x, gpre_ref[...]).astype(BF16)
    yp = yp_ref[...]
    ym = ym_ref[...]
    yx = yx_ref[...]
    nchunk = 256
    for c in range(D // nchunk):
        cs = slice(c * nchunk, (c + 1) * nchunk)
        acc = _sigmoid(_dot(h, wg_ref[:, c * nchunk:(c + 1) * nchunk])) * _dot(yp, wbp_ref[:, cs])
        acc += (_sigmoid(_dot(h, wg_ref[:, D + c * nchunk:D + (c + 1) * nchunk]))
                * _dot(ym, wbm_ref[:, cs]))
        acc += (_sigmoid(_dot(h, wg_ref[:, 2 * D + c * nchunk:2 * D + (c + 1) * nchunk]))
                * _dot(yx, wbx_ref[:, cs]))
        merged_ref[:, cs] = acc.astype(BF16)
    o_ref[...] = x + _rms(_dot(merged_ref[...], wo_ref[...]), gpost_ref[...])


def _merge(x2, yp, ym, yx, gpre, gpost, wg, wbp, wbm, wbx, wo, tm, dec=None):
    n = x2.shape[0]
    steps = n // tm
    row = lambda i: (i, 0)
    args = (x2, yp, ym, yx, gpre, gpost, wg, wbp, wbm, wbx, wo)
    in_specs = [pl.BlockSpec((tm, D), row), pl.BlockSpec((tm, POOL_W), row),
                pl.BlockSpec((tm, D), row), pl.BlockSpec((tm, XW), row),
                _const_spec((1, D)), _const_spec((1, D)), _const_spec((D, 3 * D)),
                _const_spec((POOL_W, D)), _const_spec((D, D)), _const_spec((XW, D)),
                _const_spec((D, D))]
    assert len(args) == N_MERGE_IN
    out_shape = jax.ShapeDtypeStruct((n, D), F32)
    out_specs = pl.BlockSpec((tm, D), row)
    body = _merge_kernel
    scratch = [pltpu.VMEM((tm, D), BF16)]
    if dec is not None:
        n_dec = dec[0].shape[0]
        d_args, d_in, d_shape, d_out = _mlstm_decode_operands(*dec, n_dec // steps)
        assert len(d_args) == N_DEC_IN
        args, in_specs = args + d_args, in_specs + d_in
        out_shape, out_specs = (out_shape,) + d_shape, (out_specs,) + d_out
        scratch = scratch + MLSTM_DEC_SCRATCH
        body = _merge_dec_kernel
    return pl.pallas_call(
        body,
        out_shape=out_shape,
        grid=(steps,),
        in_specs=in_specs,
        out_specs=out_specs,
        scratch_shapes=scratch,
        compiler_params=_params(("parallel",)),
        name="merge_out",
    )(*args)


def _ffn_kernel(x_ref, gpre_ref, gpost_ref, w1_ref, w2_ref, o_ref):
    x = x_ref[...]
    h = _rms(x, gpre_ref[...]).astype(BF16)
    fchunk = 1024
    acc = None
    for c in range(D_FF // fchunk):
        a = _dot(h, w1_ref[:, c * fchunk:(c + 1) * fchunk])
        a = jnp.square(jnp.maximum(a, 0.0)).astype(BF16)
        part = _dot(a, w2_ref[c * fchunk:(c + 1) * fchunk, :])
        acc = part if acc is None else acc + part
    o_ref[...] = x + _rms(acc, gpost_ref[...])


def _ffn(x2, gpre, gpost, w1, w2, tm):
    n = x2.shape[0]
    row = lambda i: (i, 0)
    return pl.pallas_call(
        _ffn_kernel,
        out_shape=jax.ShapeDtypeStruct((n, D), F32),
        grid=(n // tm,),
        in_specs=[pl.BlockSpec((tm, D), row), _const_spec((1, D)), _const_spec((1, D)),
                  _const_spec((D, D_FF)), _const_spec((D_FF, D))],
        out_specs=pl.BlockSpec((tm, D), row),
        compiler_params=_params(("parallel",)),
        name="ffn",
    )(x2, gpre, gpost, w1, w2)


def kernel(x_prompt, x_sample, mem_prompt, state_pool_buf, state_mlstm_C, state_mlstm_n, state_mlstm_m, cache_mem_k, cache_mem_v, g_pre_mix, w_in, b_if, w_pool, pool_scale, g_mem, w_mem_kv, w_br_pool, w_br_mlstm, w_br_xattn, w_out, g_post_mix, g_pre_mlp, w_ff1, w_ff2, g_post_mlp):
    batch, t_len, _ = x_prompt.shape
    n_dec = x_sample.shape[0]
    assert w_in.shape[0] == 1, "single layer"

    w_pool_b = w_pool[0].astype(BF16)
    w_kv_b = w_mem_kv[0].astype(BF16)
    g_mix, g_pm, g_mlp, g_pmlp, g_m = (g[0].reshape(1, D) for g in
                                       (g_pre_mix, g_post_mix, g_pre_mlp, g_post_mlp, g_mem))
    scale = pool_scale[0].reshape(1, POOL_W)
    bif = b_if[0]

    w_in_b = w_in[0].astype(BF16)
    w_gate = w_in_b[:, O_GATE:]
    mk_p, mv_p = _mem_kv(mem_prompt.reshape(batch * N_MEM, D), g_m, w_kv_b)

    xs = x_sample.reshape(n_dec, D)
    u_s, q_s, k_s, v_s, og_s, xq_s, _, gifc_s = _in_proj(xs, g_mix, w_in_b, n_dec, F32)
    buf_t = jnp.transpose(state_pool_buf[0], (1, 0, 2))
    ypool_s, buf_s_t = _pool_decode(u_s, buf_t, w_pool_b, scale)
    dec_xattn = (xq_s, cache_mem_k[0], cache_mem_v[0])
    dec_mlstm = (q_s, k_s, v_s, og_s, gifc_s, bif.reshape(1, 2 * HEADS),
                 state_mlstm_C[0], state_mlstm_n[0], state_mlstm_m[0])

    xp = x_prompt.reshape(batch * t_len, D)
    later_weights = (w_ff1[0], w_ff2[0], w_out[0], w_br_pool[0], w_br_mlstm[0], w_br_xattn[0])
    u, q, k, v, og, xq, gift, _, yx_s, w1, w2, wo, wbp, wbm, wbx = _in_proj(
        xp, g_mix, w_in_b, TM_INPROJ, BF16, dec=dec_xattn, convert=later_weights)
    yx_s = yx_s.reshape(n_dec, XW)
    y_pool, buf_p = _pool_prompt(u, w_pool_b, scale, batch, t_len)
    rowp, colp, m_last = _gates(gift, bif.reshape(2 * HEADS, 1), batch, t_len)
    y_ml, c_p, n_p = _mlstm_prompt(q, k, v, og, rowp, colp, batch, t_len)
    y_x = _xattn_prompt(xq, mk_p, mv_p, batch, t_len)
    x1, *dec_out = _merge(xp, y_pool, y_ml, y_x, g_mix, g_pm, w_gate, wbp, wbm, wbx, wo,
                          TM_MERGE, dec=dec_mlstm)
    yp = _ffn(x1, g_mlp, g_pmlp, w1, w2, TM_PROMPT)

    yml_s, c_s, n_s, m_s = _mlstm_decode_results(*dec_out)
    x1_s = _merge(xs, ypool_s, yml_s, yx_s, g_mix, g_pm, w_gate, wbp, wbm, wbx, wo, n_dec)
    ys = _ffn(x1_s, g_mlp, g_pmlp, w1, w2, n_dec)

    return (yp.reshape(batch, t_len, D), ys.reshape(n_dec, 1, D),
            buf_p[None], c_p[None], n_p.reshape(1, batch, HEADS, DH),
            m_last.reshape(1, batch, HEADS),
            mk_p.reshape(1, batch, N_MEM, XH, XDH), mv_p.reshape(1, batch, N_MEM, XH, XDH),
            jnp.transpose(buf_s_t, (1, 0, 2))[None], c_s[None], n_s[None], m_s[None])
```

```python
import functools

import jax
import jax.numpy as jnp
from jax import lax
from jax.experimental import pallas as pl
from jax.experimental.pallas import tpu as pltpu

F32 = jnp.float32
BF16 = jnp.bfloat16

D = 1024
POOL_W = 512
POOL_G = 128
POOL_WINDOWS = (2, 4, 8, 16)
POOL_BUF = 15
HEADS = 4
DH = 256
XH = 4
XDH = 128
XW = 512
N_MEM = 256
D_FF = 4096
EPS = 1e-6
NEG = -1e30
PAST_LEN = 16384

TM_PROMPT = 1024
TM_INPROJ = 512
TM_MERGE = 512
CHUNK = 256
MLSTM_GROUP = 4

VMEM_LIMIT = 56 * 1024 * 1024

O_QKVO = POOL_W
O_GIF = O_QKVO + 4 * D
O_XQ = O_GIF + 2 * HEADS
O_GATE = O_XQ + XW


def _params(sem):
    return pltpu.CompilerParams(dimension_semantics=sem, vmem_limit_bytes=VMEM_LIMIT)


def _const_spec(shape):
    nd = len(shape)
    return pl.BlockSpec(shape, lambda *_: (0,) * nd, pipeline_mode=pl.Buffered(1))


def _rms(x, g):
    ms = jnp.mean(x * x, axis=-1, keepdims=True)
    return x * lax.rsqrt(ms + EPS) * g


def _log_sigmoid(x):
    return jnp.minimum(x, 0.0) - jnp.log(1.0 + jnp.exp(-jnp.abs(x)))


def _sigmoid(x):
    return 1.0 / (1.0 + jnp.exp(-x))


def _dot(a, b):
    return jnp.dot(a, b, preferred_element_type=F32)


def _dot_nt(a, b):
    return lax.dot_general(a, b, (((1,), (1,)), ((), ())), preferred_element_type=F32)


def _dot_tn(a, b):
    return lax.dot_general(a, b, (((0,), (0,)), ((), ())), preferred_element_type=F32)


N_INPROJ_IN = 5
N_INPROJ_OUT = 8


def _in_proj_dec_kernel(n_conv, *refs):
    n_in = N_INPROJ_IN + 3 + n_conv
    host_in = refs[:N_INPROJ_IN]
    dec_in = refs[N_INPROJ_IN:N_INPROJ_IN + 3]
    conv_in = refs[N_INPROJ_IN + 3:n_in]
    host_out = refs[n_in:n_in + N_INPROJ_OUT]
    dec_out = refs[n_in + N_INPROJ_OUT]
    conv_out = refs[n_in + N_INPROJ_OUT + 1:]
    for src, dst in zip(conv_in, conv_out):
        dst[...] = src[...].astype(dst.dtype)
    _xattn_decode_kernel(*dec_in, dec_out)
    _in_proj_kernel(*host_in, *host_out)


def _in_proj_kernel(x_ref, g_ref, w_ref, wxq_ref, wgif_ref, u_ref, q_ref, k_ref, v_ref, og_ref,
                    xq_ref, gift_ref, gifc_ref):
    h = _rms(x_ref[...], g_ref[...]).astype(BF16)

    def seg(lo, n):
        return _dot_nt(h, w_ref[lo:lo + n, :])

    u_ref[...] = seg(0, POOL_W)
    q_ref[...] = seg(POOL_W, D).astype(q_ref.dtype)
    k_ref[...] = (seg(POOL_W + D, D) * (DH ** -0.5)).astype(k_ref.dtype)
    v_ref[...] = seg(POOL_W + 2 * D, D).astype(v_ref.dtype)
    og_ref[...] = _sigmoid(seg(POOL_W + 3 * D, D))
    xq_ref[...] = _dot_nt(h, wxq_ref[...]).astype(xq_ref.dtype)
    gif = _dot_nt(h, wgif_ref[...])
    gifc_ref[...] = gif[:, 0:2 * HEADS]
    for r in range(gif.shape[0] // 128):
        gift_ref[:, r * 128:(r + 1) * 128] = gif[r * 128:(r + 1) * 128, :].T[0:2 * HEADS, :]


def _in_proj(x2, g, w_main_t, w_xq_t, w_gif_t, tm, qkv_dtype, dec=None, convert=()):
    n = x2.shape[0]
    steps = n // tm
    row = lambda i: (i, 0)
    out_shape = (
        jax.ShapeDtypeStruct((n, POOL_W), F32),
        jax.ShapeDtypeStruct((n, D), qkv_dtype),
        jax.ShapeDtypeStruct((n, D), qkv_dtype),
        jax.ShapeDtypeStruct((n, D), qkv_dtype),
        jax.ShapeDtypeStruct((n, D), F32),
        jax.ShapeDtypeStruct((n, XW), qkv_dtype),
        jax.ShapeDtypeStruct((2 * HEADS, n), F32),
        jax.ShapeDtypeStruct((n, 2 * HEADS), F32),
    )
    args = (x2, g, w_main_t, w_xq_t, w_gif_t)
    in_specs = [
        pl.BlockSpec((tm, D), row),
        _const_spec((1, D)),
        _const_spec((O_GIF, D)),
        _const_spec(w_xq_t.shape),
        _const_spec(w_gif_t.shape),
    ]
    out_specs = (
        pl.BlockSpec((tm, POOL_W), row),
        pl.BlockSpec((tm, D), row),
        pl.BlockSpec((tm, D), row),
        pl.BlockSpec((tm, D), row),
        pl.BlockSpec((tm, D), row),
        pl.BlockSpec((tm, XW), row),
        pl.BlockSpec((2 * HEADS, tm), lambda i: (0, i)),
        pl.BlockSpec((tm, 2 * HEADS), row),
    )
    assert len(args) == N_INPROJ_IN and len(out_shape) == N_INPROJ_OUT
    body = _in_proj_kernel
    if dec is not None:
        d_args, d_in, d_shape, d_out = _xattn_decode_operands(*dec, dec[0].shape[0] // steps)
        args, in_specs = args + d_args, in_specs + d_in
        out_shape, out_specs = out_shape + (d_shape,), out_specs + (d_out,)
        for w, row0, nrows in convert:
            per = nrows // steps
            spec = pl.BlockSpec((per, w.shape[1]), row)
            src = spec if row0 == 0 and nrows == w.shape[0] else pl.BlockSpec(
                (pl.Element(per), pl.Element(w.shape[1])), lambda i, r=row0, p=per: (pl.multiple_of(r + i * p, 8), 0))
            args, in_specs = args + (w,), in_specs + [src]
            out_shape += (jax.ShapeDtypeStruct((nrows, w.shape[1]), BF16),)
            out_specs += (spec,)
        body = functools.partial(_in_proj_dec_kernel, len(convert))
    return pl.pallas_call(
        body,
        out_shape=out_shape,
        grid=(steps,),
        in_specs=in_specs,
        out_specs=out_specs,
        compiler_params=_params(("parallel",)),
        name="in_proj",
    )(*args)


def _mem_kv_kernel(mem_ref, g_ref, w_ref, k_ref, v_ref, k4_ref, v4_ref):
    h = _rms(mem_ref[...], g_ref[...]).astype(BF16)
    kv = _dot(h, w_ref[...])
    k_ref[...] = kv[:, :XW]
    v_ref[...] = kv[:, XW:]
    for hd in range(XH):
        k4_ref[:, hd, :] = kv[:, hd * XDH:(hd + 1) * XDH]
        v4_ref[:, hd, :] = kv[:, XW + hd * XDH:XW + (hd + 1) * XDH]


def _mem_kv(mem2, g, w):
    n = mem2.shape[0]
    tm = TM_PROMPT
    row = lambda i: (i, 0)
    row3 = lambda i: (i, 0, 0)
    return pl.pallas_call(
        _mem_kv_kernel,
        out_shape=(jax.ShapeDtypeStruct((n, XW), F32), jax.ShapeDtypeStruct((n, XW), F32),
                   jax.ShapeDtypeStruct((n, XH, XDH), F32), jax.ShapeDtypeStruct((n, XH, XDH), F32)),
        grid=(n // tm,),
        in_specs=[pl.BlockSpec((tm, D), row), _const_spec((1, D)), _const_spec((D, 2 * XW))],
        out_specs=(pl.BlockSpec((tm, XW), row), pl.BlockSpec((tm, XW), row),
                   pl.BlockSpec((tm, XH, XDH), row3), pl.BlockSpec((tm, XH, XDH), row3)),
        compiler_params=_params(("parallel",)),
        name="mem_kv",
    )(mem2, g, w)


def _pool_prompt_kernel(u_ref, w_ref, s_ref, y_ref, nb_ref):
    t_len = u_ref.shape[0]
    t_idx = lax.broadcasted_iota(jnp.int32, (t_len, POOL_G), 0)
    for g, win in enumerate(POOL_WINDOWS):
        cols = slice(g * POOL_G, (g + 1) * POOL_G)
        u = u_ref[:, cols]
        acc = u
        span = 1
        while span < win:
            shifted = pltpu.roll(acc, span, axis=0)
            acc = acc + jnp.where(t_idx >= span, shifted, 0.0)
            span *= 2
        cnt = jnp.minimum(t_idx + 1, win).astype(F32)
        d = (acc / cnt - u).astype(BF16)
        y = _dot(d, w_ref[g]) * s_ref[:, cols]
        y_ref[:, cols] = y.astype(y_ref.dtype)
    nb_ref[0] = u_ref[t_len - POOL_BUF:, :]


def _pool_prompt(u, w_pool, scale, batch, t_len):
    return pl.pallas_call(
        _pool_prompt_kernel,
        out_shape=(jax.ShapeDtypeStruct((batch * t_len, POOL_W), BF16),
                   jax.ShapeDtypeStruct((batch, POOL_BUF, POOL_W), F32)),
        grid=(batch,),
        in_specs=[pl.BlockSpec((t_len, POOL_W), lambda b: (b, 0)),
                  _const_spec((len(POOL_WINDOWS), POOL_G, POOL_G)),
                  _const_spec((1, POOL_W))],
        out_specs=(pl.BlockSpec((t_len, POOL_W), lambda b: (b, 0)),
                   pl.BlockSpec((1, POOL_BUF, POOL_W), lambda b: (b, 0, 0))),
        compiler_params=_params(("parallel",)),
        name="pool_prompt",
    )(u, w_pool, scale)


def _pool_decode_kernel(u_ref, buf_ref, w_ref, s_ref, y_ref, nb_ref):
    u_all = u_ref[...]
    for g, win in enumerate(POOL_WINDOWS):
        cols = slice(g * POOL_G, (g + 1) * POOL_G)
        u = u_all[:, cols]
        acc = u
        for j in range(POOL_BUF - (win - 1), POOL_BUF):
            acc = acc + buf_ref[j, :, cols]
        cnt = float(min(win, PAST_LEN + 1))
        d = (acc / cnt - u).astype(BF16)
        y = _dot(d, w_ref[g]) * s_ref[:, cols]
        y_ref[:, cols] = y.astype(y_ref.dtype)
    for j in range(POOL_BUF - 1):
        nb_ref[j] = buf_ref[j + 1]
    nb_ref[POOL_BUF - 1] = u_all


def _pool_decode(u, buf_t, w_pool, scale):
    n = u.shape[0]
    return pl.pallas_call(
        _pool_decode_kernel,
        out_shape=(jax.ShapeDtypeStruct((n, POOL_W), BF16),
                   jax.ShapeDtypeStruct((POOL_BUF, n, POOL_W), F32)),
        grid=(1,),
        in_specs=[_const_spec((n, POOL_W)), _const_spec((POOL_BUF, n, POOL_W)),
                  _const_spec((len(POOL_WINDOWS), POOL_G, POOL_G)), _const_spec((1, POOL_W))],
        out_specs=(pl.BlockSpec((n, POOL_W), lambda i: (0, 0)),
                   pl.BlockSpec((POOL_BUF, n, POOL_W), lambda i: (0, 0, 0))),
        compiler_params=_params(("arbitrary",)),
        name="pool_decode",
    )(u, buf_t, w_pool, scale)


COL_M, COL_INTER, COL_EINV, COL_END, COL_DECAY = (i * HEADS for i in range(5))


def _scan_lanes(x, op, fill):
    t_len = x.shape[-1]
    lane = lax.broadcasted_iota(jnp.int32, x.shape, 1)
    k = 1
    while k < t_len:
        shifted = pltpu.roll(x, k, axis=1)
        x = op(x, jnp.where(lane >= k, shifted, fill))
        k *= 2
    return x


def _gates_kernel(gift_ref, bif_ref, row_ref, col_ref, mlast_ref, pack_ref):
    t_len = gift_ref.shape[1]
    L = CHUNK
    g = gift_ref[...] + bif_ref[...]
    ig = g[0:HEADS, :]
    lf = _log_sigmoid(g[HEADS:2 * HEADS, :])
    b_cum = _scan_lanes(lf, jnp.add, 0.0)
    a = ig - b_cum
    m_run = jnp.maximum(_scan_lanes(a, jnp.maximum, NEG), 0.0)
    m_tot = b_cum + m_run
    row_ref[0, 0:HEADS, :] = a
    row_ref[0, HEADS:2 * HEADS, :] = m_run
    mlast_ref[0] = m_tot[:, t_len - 1:t_len]
    pack_ref[...] = jnp.zeros(pack_ref.shape, F32)
    pack_ref[COL_M:COL_M + HEADS, :] = m_run
    pack_ref[COL_EINV:COL_EINV + HEADS, :] = jnp.exp(-m_tot)
    for c in range(t_len // L):
        lo, hi = c * L, (c + 1) * L
        m_prev = jnp.zeros((HEADS, 1), F32) if c == 0 else m_run[:, lo - 1:lo]
        m_end = m_run[:, hi - 1:hi]
        pack_ref[COL_INTER:COL_INTER + HEADS, lo:hi] = jnp.exp(m_prev - m_run[:, lo:hi])
        pack_ref[COL_END:COL_END + HEADS, lo:hi] = jnp.exp(a[:, lo:hi] - m_end)
        pack_ref[COL_DECAY:COL_DECAY + HEADS, lo:hi] = jnp.broadcast_to(
            jnp.exp(m_prev - m_end), (HEADS, L))
    for c in range(t_len // 128):
        col_ref[0, c * 128:(c + 1) * 128, :] = pack_ref[:, c * 128:(c + 1) * 128].T


def _gates(gift, bif, batch, t_len):
    return pl.pallas_call(
        _gates_kernel,
        out_shape=(jax.ShapeDtypeStruct((batch, 2 * HEADS, t_len), F32),
                   jax.ShapeDtypeStruct((batch, t_len, 128), F32),
                   jax.ShapeDtypeStruct((batch, HEADS, 1), F32)),
        grid=(batch,),
        in_specs=[pl.BlockSpec((2 * HEADS, t_len), lambda b: (0, b)), _const_spec((2 * HEADS, 1))],
        out_specs=(pl.BlockSpec((1, 2 * HEADS, t_len), lambda b: (b, 0, 0)),
                   pl.BlockSpec((1, t_len, 128), lambda b: (b, 0, 0)),
                   pl.BlockSpec((1, HEADS, 1), lambda b: (b, 0, 0))),
        scratch_shapes=[pltpu.VMEM((128, t_len), F32)],
        compiler_params=_params(("parallel",)),
        name="mlstm_gates",
    )(gift, bif)


def _mlstm_prompt_body(n_chunks, q_ref, k_ref, v_ref, og_ref, row_ref, col_ref, y_ref, c_ref, n_ref):
    c_idx = pl.program_id(0) % n_chunks
    L = q_ref.shape[1]

    @pl.when(c_idx == 0)
    def _():
        c_ref[...] = jnp.zeros(c_ref.shape, F32)
        n_ref[...] = jnp.zeros(n_ref.shape, F32)

    t_idx = lax.broadcasted_iota(jnp.int32, (L, L), 0)
    s_idx = lax.broadcasted_iota(jnp.int32, (L, L), 1)
    causal = s_idx <= t_idx
    for b in range(q_ref.shape[0]):
        for h in range(HEADS):
            hs = slice(h * DH, (h + 1) * DH)
            qh = q_ref[b, :, hs]
            kh = k_ref[b, :, hs]
            vh = v_ref[b, :, hs]
            a_row = row_ref[b, h:h + 1, :]
            m_col = col_ref[b, :, COL_M + h:COL_M + h + 1]
            w_inter = col_ref[b, :, COL_INTER + h:COL_INTER + h + 1]
            einv = col_ref[b, :, COL_EINV + h:COL_EINV + h + 1]
            w_end = col_ref[b, :, COL_END + h:COL_END + h + 1]
            decay = col_ref[b, 0:1, COL_DECAY + h:COL_DECAY + h + 1]

            dmat = jnp.exp(jnp.where(causal, a_row - m_col, NEG))
            sw = _dot_nt(qh, kh) * dmat
            c_old = c_ref[b, h]
            n_old = n_ref[b, h]
            inter = _dot_nt(qh, c_old.astype(BF16))
            num = _dot(sw.astype(BF16), vh) + w_inter * inter
            nq = jnp.sum(qh.astype(F32) * n_old, axis=-1, keepdims=True)
            den = jnp.sum(sw, axis=-1, keepdims=True) + w_inter * nq
            r = 1.0 / jnp.maximum(jnp.abs(den), einv)
            y_ref[b, :, hs] = (og_ref[b, :, hs] * (num * r)).astype(y_ref.dtype)

            vw = (vh.astype(F32) * w_end).astype(BF16)
            c_ref[b, h] = decay * c_old + _dot_tn(vw, kh)
            n_ref[b, h] = decay * n_old + jnp.sum(kh.astype(F32) * w_end, axis=0, keepdims=True)


def _mlstm_prompt_operands(q, k, v, og, rowp, colp, t_len, b_lo, nb, gb):
    L = CHUNK
    nc = t_len // L
    batch = q.shape[0] // t_len
    g0 = b_lo // gb
    assert b_lo % gb == 0 and nb % gb == 0
    q3, k3, v3, og3 = (a.reshape(batch, t_len, D) for a in (q, k, v, og))
    tok_in = lambda i: (g0 + i // nc, i % nc, 0)
    tok_out = lambda i: (i // nc, i % nc, 0)
    in_specs = [pl.BlockSpec((gb, L, D), tok_in), pl.BlockSpec((gb, L, D), tok_in),
                pl.BlockSpec((gb, L, D), tok_in), pl.BlockSpec((gb, L, D), tok_in),
                pl.BlockSpec((gb, 2 * HEADS, L), lambda i: (g0 + i // nc, 0, i % nc)),
                pl.BlockSpec((gb, L, 128), tok_in)]
    out_shape = (jax.ShapeDtypeStruct((nb, t_len, D), BF16),
                 jax.ShapeDtypeStruct((nb, HEADS, DH, DH), F32),
                 jax.ShapeDtypeStruct((nb, HEADS, 1, DH), F32))
    out_specs = (pl.BlockSpec((gb, L, D), tok_out),
                 pl.BlockSpec((gb, HEADS, DH, DH), lambda i: (i // nc, 0, 0, 0)),
                 pl.BlockSpec((gb, HEADS, 1, DH), lambda i: (i // nc, 0, 0, 0)))
    body = functools.partial(_mlstm_prompt_body, nc)
    return body, (q3, k3, v3, og3, rowp, colp), in_specs, out_shape, out_specs, (nb // gb) * nc


def _mlstm_prompt(q, k, v, og, rowp, colp, t_len, b_lo, nb):
    body, args, in_specs, out_shape, out_specs, steps = _mlstm_prompt_operands(
        q, k, v, og, rowp, colp, t_len, b_lo, nb, MLSTM_GROUP)
    y, c_fin, n_fin = pl.pallas_call(
        body,
        out_shape=out_shape,
        grid=(steps,),
        in_specs=in_specs,
        out_specs=out_specs,
        compiler_params=_params(("arbitrary",)),
        name="mlstm_prompt",
    )(*args)
    return y.reshape(nb * t_len, D), c_fin, n_fin


def _mlstm_decode_body(q_ref, k_ref, v8_ref, og8_ref, gif_ref, bif_ref, c0_ref, n0_ref, m0_ref,
                       y8_ref, c_out_ref, n_out_ref, m_out_ref, vt_ref, ht_ref):
    vrep = jnp.concatenate([v8_ref[0]] * 16, axis=0)
    for r in range(D // 128):
        rs = slice(r * 128, (r + 1) * 128)
        vt_ref[rs, :] = vrep[:, rs].T
    ht_ref[...] = jnp.zeros(ht_ref.shape, F32)
    for j in range(q_ref.shape[0]):
        gi = gif_ref[j] + bif_ref[...]
        m0 = m0_ref[j]
        for h in range(HEADS):
            hs = slice(h * DH, (h + 1) * DH)
            qr = q_ref[j, :, hs]
            kr = k_ref[j, :, hs]
            vc = vt_ref[hs, j:j + 1]
            ig = gi[:, h:h + 1]
            lf = _log_sigmoid(gi[:, HEADS + h:HEADS + h + 1])
            m_old = m0[:, h:h + 1]
            m_new = jnp.maximum(lf + m_old, ig)
            w_i = jnp.exp(ig - m_new)
            w_f = jnp.exp(lf + m_old - m_new)
            c_old = c0_ref[j, h]
            n_old = n0_ref[j, h:h + 1, :]
            cq = jnp.sum(c_old * qr, axis=-1, keepdims=True)
            qk = jnp.sum(qr * kr, axis=-1, keepdims=True)
            nq = jnp.sum(n_old * qr, axis=-1, keepdims=True)
            num = (w_i * qk) * vc + w_f * cq
            den = w_i * qk + w_f * nq
            hcol = num / jnp.maximum(jnp.abs(den), jnp.exp(-m_new))
            ht_ref[hs, j:j + 1] = hcol
            c_out_ref[j, h] = w_f * c_old + (w_i * vc) * kr
            n_out_ref[j, h:h + 1, :] = w_f * n_old + w_i * kr
            m_out_ref[j, :, h:h + 1] = m_new
    for r in range(D // 128):
        rs = slice(r * 128, (r + 1) * 128)
        y8_ref[0, :, rs] = (og8_ref[0, :, rs] * ht_ref[rs, :].T[0:8, :]).astype(y8_ref.dtype)


MLSTM_DEC_SCRATCH = [pltpu.VMEM((D, 128), F32), pltpu.VMEM((D, 128), F32)]


def _mlstm_decode_operands(q, k, v, og, gifc, bif_row, c0, n0, m0, bb):
    n = q.shape[0]
    assert bb <= 8
    r3 = lambda i: (i, 0, 0)
    r4 = lambda i: (i, 0, 0, 0)
    q3, k3 = (a.reshape(n, 1, D) for a in (q, k))
    v8, og8 = (jnp.pad(a.reshape(n // bb, bb, D), ((0, 0), (0, 8 - bb), (0, 0))) for a in (v, og))
    args = (q3, k3, v8, og8, gifc.reshape(n, 1, 2 * HEADS), bif_row, c0, n0,
            m0.reshape(n, 1, HEADS))
    in_specs = [pl.BlockSpec((bb, 1, D), r3), pl.BlockSpec((bb, 1, D), r3),
                pl.BlockSpec((1, 8, D), r3), pl.BlockSpec((1, 8, D), r3),
                pl.BlockSpec((bb, 1, 2 * HEADS), r3), _const_spec((1, 2 * HEADS)),
                pl.BlockSpec((bb, HEADS, DH, DH), r4), pl.BlockSpec((bb, HEADS, DH), r3),
                pl.BlockSpec((bb, 1, HEADS), r3)]
    out_shape = (jax.ShapeDtypeStruct((n // bb, 8, D), BF16),
                 jax.ShapeDtypeStruct((n, HEADS, DH, DH), F32),
                 jax.ShapeDtypeStruct((n, HEADS, DH), F32),
                 jax.ShapeDtypeStruct((n, 1, HEADS), F32))
    out_specs = (pl.BlockSpec((1, 8, D), r3), pl.BlockSpec((bb, HEADS, DH, DH), r4),
                 pl.BlockSpec((bb, HEADS, DH), r3), pl.BlockSpec((bb, 1, HEADS), r3))
    return args, in_specs, out_shape, out_specs


def _mlstm_decode_results(y8, c_new, n_new, m_new):
    n = c_new.shape[0]
    bb = n // y8.shape[0]
    return y8[:, :bb].reshape(n, D), c_new, n_new, m_new.reshape(n, HEADS)


def _xattn_prompt_kernel(xq_ref, mk_ref, mv_ref, y_ref):
    scale = XDH ** -0.5
    for h in range(XH):
        hs = slice(h * XDH, (h + 1) * XDH)
        s = _dot_nt(xq_ref[:, hs], mk_ref[:, hs].astype(BF16)) * scale
        p = jnp.exp(s - jnp.max(s, axis=-1, keepdims=True))
        l = jnp.sum(p, axis=-1, keepdims=True)
        o = _dot(p.astype(BF16), mv_ref[:, hs].astype(BF16)) / l
        y_ref[:, hs] = o.astype(y_ref.dtype)


def _xattn_prompt(xq, mk, mv, batch, t_len):
    tq = TM_PROMPT
    nq = t_len // tq
    return pl.pallas_call(
        _xattn_prompt_kernel,
        out_shape=jax.ShapeDtypeStruct((batch * t_len, XW), BF16),
        grid=(batch, nq),
        in_specs=[pl.BlockSpec((tq, XW), lambda b, i: (b * nq + i, 0)),
                  pl.BlockSpec((N_MEM, XW), lambda b, i: (b, 0)),
                  pl.BlockSpec((N_MEM, XW), lambda b, i: (b, 0))],
        out_specs=pl.BlockSpec((tq, XW), lambda b, i: (b * nq + i, 0)),
        compiler_params=_params(("parallel", "arbitrary")),
        name="xattn_prompt",
    )(xq, mk, mv)


def _xattn_decode_kernel(q_ref, mk_ref, mv_ref, y_ref):
    scale = XDH ** -0.5
    q = q_ref[...][:, None, :, :]
    s = jnp.sum(mk_ref[...] * q, axis=-1, keepdims=True) * scale
    mx = jnp.max(s, axis=1, keepdims=True)
    mx = jnp.maximum(mx, pltpu.roll(mx, XH, axis=2))
    p = jnp.exp(s - mx)
    l = jnp.sum(p, axis=1, keepdims=True)
    l = l + pltpu.roll(l, XH, axis=2)
    o = jnp.sum(p * mv_ref[...], axis=1, keepdims=True)
    o = (o + pltpu.roll(o, XH, axis=2)) / l
    y_ref[...] = o[:, 0, 0:XH, :].astype(y_ref.dtype)


def _xattn_decode_operands(xq, mk, mv, bb):
    n = xq.shape[0]
    xq4 = xq.reshape(n, XH, XDH)
    q2 = jnp.concatenate([xq4, xq4], axis=1)
    mk2, mv2 = (a.reshape(n, N_MEM // 2, 2 * XH, XDH) for a in (mk, mv))
    kv_spec = pl.BlockSpec((bb, N_MEM // 2, 2 * XH, XDH), lambda i: (i, 0, 0, 0))
    in_specs = [pl.BlockSpec((bb, 2 * XH, XDH), lambda i: (i, 0, 0)), kv_spec, kv_spec]
    out_shape = jax.ShapeDtypeStruct((n, XH, XDH), BF16)
    out_spec = pl.BlockSpec((bb, XH, XDH), lambda i: (i, 0, 0))
    return (q2, mk2, mv2), in_specs, out_shape, out_spec


def _with_side_job(host_body, n_host, side_body, n_side, *refs):
    (hi, ho, hs), (si, so, ss) = n_host, n_side
    ins, outs, scr = refs[:hi + si], refs[hi + si:hi + si + ho + so], refs[hi + si + ho + so:]
    assert len(scr) == hs + ss
    side_body(*ins[hi:], *outs[ho:], *scr[hs:])
    host_body(*ins[:hi], *outs[:ho], *scr[:hs])


def _merge_kernel(x_ref, yp_ref, ym_ref, yx_ref, gpre_ref, gpost_ref, wg_ref, wbp_ref, wbm_ref,
                  wbx_ref, wo_ref, o_ref, merged_ref):
    x = x_ref[...]
    h = _rms(x, gpre_ref[...]).astype(BF16)
    yp = yp_ref[...]
    ym = ym_ref[...]
    yx = yx_ref[...]
    nchunk = 256
    for c in range(D // nchunk):
        cs = slice(c * nchunk, (c + 1) * nchunk)
        acc = _sigmoid(_dot_nt(h, wg_ref[c * nchunk:(c + 1) * nchunk, :])) * _dot(yp, wbp_ref[:, cs])
        acc += (_sigmoid(_dot_nt(h, wg_ref[D + c * nchunk:D + (c + 1) * nchunk, :]))
                * _dot(ym, wbm_ref[:, cs]))
        acc += (_sigmoid(_dot_nt(h, wg_ref[2 * D + c * nchunk:2 * D + (c + 1) * nchunk, :]))
                * _dot(yx, wbx_ref[:, cs]))
        merged_ref[:, cs] = acc.astype(BF16)
    o_ref[...] = x + _rms(_dot(merged_ref[...], wo_ref[...]), gpost_ref[...])


def _merge(x2, yp, ym, yx, gpre, gpost, wg, wbp, wbm, wbx, wo, tm, row0=0, side=None):
    n = ym.shape[0]
    steps = n // tm
    blk0 = row0 // tm
    assert row0 % tm == 0
    row = lambda i: (i, 0)
    off = lambda i: (blk0 + i, 0)
    args = (x2, yp, ym, yx, gpre, gpost, wg, wbp, wbm, wbx, wo)
    in_specs = [pl.BlockSpec((tm, D), off), pl.BlockSpec((tm, POOL_W), off),
                pl.BlockSpec((tm, D), row), pl.BlockSpec((tm, XW), off),
                _const_spec((1, D)), _const_spec((1, D)), _const_spec((3 * D, D)),
                _const_spec((POOL_W, D)), _const_spec((D, D)), _const_spec((XW, D)),
                _const_spec((D, D))]
    out_shape = jax.ShapeDtypeStruct((n, D), F32)
    out_specs = pl.BlockSpec((tm, D), row)
    body = _merge_kernel
    scratch = [pltpu.VMEM((tm, D), BF16)]
    if side is not None:
        s_body, s_args, s_in, s_shape, s_out, s_scratch = side
        body = functools.partial(_with_side_job, _merge_kernel, (len(args), 1, 1),
                                 s_body, (len(s_args), len(s_shape), len(s_scratch)))
        args, in_specs = args + tuple(s_args), in_specs + list(s_in)
        out_shape, out_specs = (out_shape,) + tuple(s_shape), (out_specs,) + tuple(s_out)
        scratch = scratch + list(s_scratch)
    return pl.pallas_call(
        body,
        out_shape=out_shape,
        grid=(steps,),
        in_specs=in_specs,
        out_specs=out_specs,
        scratch_shapes=scratch,
        compiler_params=_params(("parallel",)),
        name="merge_out",
    )(*args)


def _ffn_kernel(x_ref, gpre_ref, gpost_ref, w1_ref, w2_ref, o_ref):
    x = x_ref[...]
    h = _rms(x, gpre_ref[...]).astype(BF16)
    fchunk = 1024
    acc = None
    for c in range(D_FF // fchunk):
        a = _dot(h, w1_ref[:, c * fchunk:(c + 1) * fchunk])
        a = jnp.square(jnp.maximum(a, 0.0)).astype(BF16)
        part = _dot(a, w2_ref[c * fchunk:(c + 1) * fchunk, :])
        acc = part if acc is None else acc + part
    o_ref[...] = x + _rms(acc, gpost_ref[...])


def _ffn(x2, gpre, gpost, w1, w2, tm):
    n = x2.shape[0]
    row = lambda i: (i, 0)
    return pl.pallas_call(
        _ffn_kernel,
        out_shape=jax.ShapeDtypeStruct((n, D), F32),
        grid=(n // tm,),
        in_specs=[pl.BlockSpec((tm, D), row), _const_spec((1, D)), _const_spec((1, D)),
                  _const_spec((D, D_FF)), _const_spec((D_FF, D))],
        out_specs=pl.BlockSpec((tm, D), row),
        compiler_params=_params(("parallel",)),
        name="ffn",
    )(x2, gpre, gpost, w1, w2)


def kernel(x_prompt, x_sample, mem_prompt, state_pool_buf, state_mlstm_C, state_mlstm_n, state_mlstm_m, cache_mem_k, cache_mem_v, g_pre_mix, w_in, b_if, w_pool, pool_scale, g_mem, w_mem_kv, w_br_pool, w_br_mlstm, w_br_xattn, w_out, g_post_mix, g_pre_mlp, w_ff1, w_ff2, g_post_mlp):
    batch, t_len, _ = x_prompt.shape
    n_dec = x_sample.shape[0]
    assert w_in.shape[0] == 1, "single layer"

    w_pool_b = w_pool[0].astype(BF16)
    w_kv_b = w_mem_kv[0].astype(BF16)
    g_mix, g_pm, g_mlp, g_pmlp, g_m = (g[0].reshape(1, D) for g in
                                       (g_pre_mix, g_post_mix, g_pre_mlp, g_post_mlp, g_mem))
    scale = pool_scale[0].reshape(1, POOL_W)
    bif = b_if[0]

    w_in_t = jnp.transpose(w_in[0]).astype(BF16)
    w_gif_t = jnp.pad(w_in_t[O_GIF:O_XQ], ((0, 128 - 2 * HEADS), (0, 0)))
    w_xq_t = w_in_t[O_XQ:O_GATE]
    w_in_parts = (w_in_t, w_xq_t, w_gif_t)
    mk_p, mv_p, mk4_p, mv4_p = _mem_kv(mem_prompt.reshape(batch * N_MEM, D), g_m, w_kv_b)

    xs = x_sample.reshape(n_dec, D)
    u_s, q_s, k_s, v_s, og_s, xq_s, _, gifc_s = _in_proj(xs, g_mix, *w_in_parts, n_dec, F32)
    buf_t = jnp.transpose(state_pool_buf[0], (1, 0, 2))
    ypool_s, buf_s_t = _pool_decode(u_s, buf_t, w_pool_b, scale)
    dec_xattn = (xq_s, cache_mem_k[0], cache_mem_v[0])
    dec_mlstm = (q_s, k_s, v_s, og_s, gifc_s, bif.reshape(1, 2 * HEADS),
                 state_mlstm_C[0], state_mlstm_n[0], state_mlstm_m[0])

    xp = x_prompt.reshape(batch * t_len, D)
    later_weights = [(w[0], 0, w.shape[1]) for w in
                     (w_ff1, w_ff2, w_out, w_br_pool, w_br_mlstm, w_br_xattn)]
    later_weights.append((jnp.transpose(w_in[0]), O_GATE, 3 * D))
    u, q, k, v, og, xq, gift, _, yx_s, w1, w2, wo, wbp, wbm, wbx, w_gate = _in_proj(
        xp, g_mix, *w_in_parts, TM_INPROJ, BF16, dec=dec_xattn, convert=later_weights)
    yx_s = yx_s.reshape(n_dec, XW)
    y_pool, buf_p = _pool_prompt(u, w_pool_b, scale, batch, t_len)
    rowp, colp, m_last = _gates(gift, bif.reshape(2 * HEADS, 1), batch, t_len)
    y_x = _xattn_prompt(xq, mk_p, mv_p, batch, t_len)
    merge_w = (g_mix, g_pm, w_gate, wbp, wbm, wbx, wo)
    y_ml, c_p, n_p = _mlstm_prompt(q, k, v, og, rowp, colp, t_len, 0, batch)
    d_args, d_in, d_shape, d_out = _mlstm_decode_operands(
        *dec_mlstm, n_dec // (batch * t_len // TM_MERGE))
    side = (_mlstm_decode_body, d_args, d_in, d_shape, d_out, MLSTM_DEC_SCRATCH)
    x1, *dec_out = _merge(xp, y_pool, y_ml, y_x, *merge_w, TM_MERGE, 0, side)
    yp = _ffn(x1, g_mlp, g_pmlp, w1, w2, TM_PROMPT)

    yml_s, c_s, n_s, m_s = _mlstm_decode_results(*dec_out)
    x1_s = _merge(xs, ypool_s, yml_s, yx_s, *merge_w, n_dec)
    ys = _ffn(x1_s, g_mlp, g_pmlp, w1, w2, n_dec)

    return (yp.reshape(batch, t_len, D), ys.reshape(n_dec, 1, D),
            buf_p[None], c_p[None], n_p.reshape(1, batch, HEADS, DH),
            m_last.reshape(1, batch, HEADS),
            mk4_p.reshape(1, batch, N_MEM, XH, XDH), mv4_p.reshape(1, batch, N_MEM, XH, XDH),
            jnp.transpose(buf_s_t, (1, 0, 2))[None], c_s[None], n_s[None], m_s[None])
```

```python
import functools

import jax
import jax.numpy as jnp
from jax import lax
from jax.experimental import pallas as pl
from jax.experimental.pallas import tpu as pltpu

F32 = jnp.float32
BF16 = jnp.bfloat16

D = 1024
POOL_W = 512
POOL_G = 128
POOL_WINDOWS = (2, 4, 8, 16)
POOL_BUF = 15
HEADS = 4
DH = 256
XH = 4
XDH = 128
XW = 512
N_MEM = 256
D_FF = 4096
EPS = 1e-6
NEG = -1e30
PAST_LEN = 16384

TM_PROMPT = 1024
TM_INPROJ = 512
TM_MERGE = 512
CHUNK = 256
MLSTM_GROUP = 4

VMEM_LIMIT = 56 * 1024 * 1024

O_QKVO = POOL_W
O_GIF = O_QKVO + 4 * D
O_XQ = O_GIF + 2 * HEADS
O_GATE = O_XQ + XW


def _params(sem):
    return pltpu.CompilerParams(dimension_semantics=sem, vmem_limit_bytes=VMEM_LIMIT)


def _const_spec(shape):
    nd = len(shape)
    return pl.BlockSpec(shape, lambda *_: (0,) * nd, pipeline_mode=pl.Buffered(1))


def _rms(x, g):
    ms = jnp.mean(x * x, axis=-1, keepdims=True)
    return x * lax.rsqrt(ms + EPS) * g


def _log_sigmoid(x):
    return jnp.minimum(x, 0.0) - jnp.log(1.0 + jnp.exp(-jnp.abs(x)))


def _sigmoid(x):
    return 1.0 / (1.0 + jnp.exp(-x))


def _dot(a, b):
    return jnp.dot(a, b, preferred_element_type=F32)


def _dot_nt(a, b):
    return lax.dot_general(a, b, (((1,), (1,)), ((), ())), preferred_element_type=F32)


def _dot_tn(a, b):
    return lax.dot_general(a, b, (((0,), (0,)), ((), ())), preferred_element_type=F32)


N_INPROJ_IN = 5
N_INPROJ_OUT = 8


def _in_proj_dec_kernel(n_conv, *refs):
    n_in = N_INPROJ_IN + 3 + n_conv
    host_in = refs[:N_INPROJ_IN]
    dec_in = refs[N_INPROJ_IN:N_INPROJ_IN + 3]
    conv_in = refs[N_INPROJ_IN + 3:n_in]
    host_out = refs[n_in:n_in + N_INPROJ_OUT]
    dec_out = refs[n_in + N_INPROJ_OUT]
    conv_out = refs[n_in + N_INPROJ_OUT + 1:]
    for src, dst in zip(conv_in, conv_out):
        dst[...] = src[...].astype(dst.dtype)
    _xattn_decode_kernel(*dec_in, dec_out)
    _in_proj_kernel(*host_in, *host_out)


def _in_proj_kernel(x_ref, g_ref, w_ref, wxq_ref, wgif_ref, u_ref, q_ref, k_ref, v_ref, og_ref,
                    xq_ref, gift_ref, gifc_ref):
    h = _rms(x_ref[...], g_ref[...]).astype(BF16)

    def seg(lo, n):
        return _dot_nt(h, w_ref[lo:lo + n, :])

    u_ref[...] = seg(0, POOL_W)
    q_ref[...] = seg(POOL_W, D).astype(q_ref.dtype)
    k_ref[...] = (seg(POOL_W + D, D) * (DH ** -0.5)).astype(k_ref.dtype)
    v_ref[...] = seg(POOL_W + 2 * D, D).astype(v_ref.dtype)
    og_ref[...] = _sigmoid(seg(POOL_W + 3 * D, D))
    xq_ref[...] = _dot_nt(h, wxq_ref[...]).astype(xq_ref.dtype)
    gif = _dot_nt(h, wgif_ref[...])
    gifc_ref[...] = gif[:, 0:2 * HEADS]
    for r in range(gif.shape[0] // 128):
        gift_ref[:, r * 128:(r + 1) * 128] = gif[r * 128:(r + 1) * 128, :].T[0:2 * HEADS, :]


def _in_proj(x2, g, w_main_t, w_xq_t, w_gif_t, tm, qkv_dtype, dec=None, convert=()):
    n = x2.shape[0]
    steps = n // tm
    row = lambda i: (i, 0)
    out_shape = (
        jax.ShapeDtypeStruct((n, POOL_W), F32),
        jax.ShapeDtypeStruct((n, D), qkv_dtype),
        jax.ShapeDtypeStruct((n, D), qkv_dtype),
        jax.ShapeDtypeStruct((n, D), qkv_dtype),
        jax.ShapeDtypeStruct((n, D), F32),
        jax.ShapeDtypeStruct((n, XW), qkv_dtype),
        jax.ShapeDtypeStruct((2 * HEADS, n), F32),
        jax.ShapeDtypeStruct((n, 2 * HEADS), F32),
    )
    args = (x2, g, w_main_t, w_xq_t, w_gif_t)
    in_specs = [
        pl.BlockSpec((tm, D), row),
        _const_spec((1, D)),
        _const_spec((O_GIF, D)),
        _const_spec(w_xq_t.shape),
        _const_spec(w_gif_t.shape),
    ]
    out_specs = (
        pl.BlockSpec((tm, POOL_W), row),
        pl.BlockSpec((tm, D), row),
        pl.BlockSpec((tm, D), row),
        pl.BlockSpec((tm, D), row),
        pl.BlockSpec((tm, D), row),
        pl.BlockSpec((tm, XW), row),
        pl.BlockSpec((2 * HEADS, tm), lambda i: (0, i)),
        pl.BlockSpec((tm, 2 * HEADS), row),
    )
    assert len(args) == N_INPROJ_IN and len(out_shape) == N_INPROJ_OUT
    body = _in_proj_kernel
    if dec is not None:
        d_args, d_in, d_shape, d_out = _xattn_decode_operands(*dec, dec[0].shape[0] // steps)
        args, in_specs = args + d_args, in_specs + d_in
        out_shape, out_specs = out_shape + (d_shape,), out_specs + (d_out,)
        for w, row0, nrows in convert:
            per = nrows // steps
            spec = pl.BlockSpec((per, w.shape[1]), row)
            src = spec if row0 == 0 and nrows == w.shape[0] else pl.BlockSpec(
                (pl.Element(per), pl.Element(w.shape[1])), lambda i, r=row0, p=per: (pl.multiple_of(r + i * p, 8), 0))
            args, in_specs = args + (w,), in_specs + [src]
            out_shape += (jax.ShapeDtypeStruct((nrows, w.shape[1]), BF16),)
            out_specs += (spec,)
        body = functools.partial(_in_proj_dec_kernel, len(convert))
    return pl.pallas_call(
        body,
        out_shape=out_shape,
        grid=(steps,),
        in_specs=in_specs,
        out_specs=out_specs,
        compiler_params=_params(("parallel",)),
        name="in_proj",
    )(*args)


MEM_KV_STEPS = 16


def _mem_kv_kernel(mem_ref, g_ref, w_ref, wt_ref, k_ref, v_ref, k4_ref, v4_ref, wtb_ref):
    wtb_ref[...] = wt_ref[...].astype(BF16)
    h = _rms(mem_ref[...], g_ref[...]).astype(BF16)
    kv = _dot(h, w_ref[...])
    k_ref[...] = kv[:, :XW]
    v_ref[...] = kv[:, XW:]
    for hd in range(XH):
        k4_ref[:, hd, :] = kv[:, hd * XDH:(hd + 1) * XDH]
        v4_ref[:, hd, :] = kv[:, XW + hd * XDH:XW + (hd + 1) * XDH]


def _mem_kv(mem2, g, w, w_in_t, n_rows):
    n = mem2.shape[0]
    tm = n // MEM_KV_STEPS
    wr = -(-n_rows // (MEM_KV_STEPS * 16)) * 16
    row = lambda i: (i, 0)
    row3 = lambda i: (i, 0, 0)
    return pl.pallas_call(
        _mem_kv_kernel,
        out_shape=(jax.ShapeDtypeStruct((n, XW), F32), jax.ShapeDtypeStruct((n, XW), F32),
                   jax.ShapeDtypeStruct((n, XH, XDH), F32), jax.ShapeDtypeStruct((n, XH, XDH), F32),
                   jax.ShapeDtypeStruct((n_rows, D), BF16)),
        grid=(MEM_KV_STEPS,),
        in_specs=[pl.BlockSpec((tm, D), row), _const_spec((1, D)), _const_spec((D, 2 * XW)),
                  pl.BlockSpec((wr, D), row)],
        out_specs=(pl.BlockSpec((tm, XW), row), pl.BlockSpec((tm, XW), row),
                   pl.BlockSpec((tm, XH, XDH), row3), pl.BlockSpec((tm, XH, XDH), row3),
                   pl.BlockSpec((wr, D), row)),
        compiler_params=_params(("parallel",)),
        name="mem_kv",
    )(mem2, g, w, w_in_t)


def _pool_prompt_kernel(u_ref, w_ref, s_ref, y_ref, nb_ref):
    t_len = u_ref.shape[0]
    t_idx = lax.broadcasted_iota(jnp.int32, (t_len, POOL_G), 0)
    for g, win in enumerate(POOL_WINDOWS):
        cols = slice(g * POOL_G, (g + 1) * POOL_G)
        u = u_ref[:, cols]
        acc = u
        span = 1
        while span < win:
            shifted = pltpu.roll(acc, span, axis=0)
            acc = acc + jnp.where(t_idx >= span, shifted, 0.0)
            span *= 2
        cnt = jnp.minimum(t_idx + 1, win).astype(F32)
        d = (acc / cnt - u).astype(BF16)
        y = _dot(d, w_ref[g]) * s_ref[:, cols]
        y_ref[:, cols] = y.astype(y_ref.dtype)
    nb_ref[0] = u_ref[t_len - POOL_BUF:, :]


def _pool_prompt(u, w_pool, scale, batch, t_len):
    return pl.pallas_call(
        _pool_prompt_kernel,
        out_shape=(jax.ShapeDtypeStruct((batch * t_len, POOL_W), BF16),
                   jax.ShapeDtypeStruct((batch, POOL_BUF, POOL_W), F32)),
        grid=(batch,),
        in_specs=[pl.BlockSpec((t_len, POOL_W), lambda b: (b, 0)),
                  _const_spec((len(POOL_WINDOWS), POOL_G, POOL_G)),
                  _const_spec((1, POOL_W))],
        out_specs=(pl.BlockSpec((t_len, POOL_W), lambda b: (b, 0)),
                   pl.BlockSpec((1, POOL_BUF, POOL_W), lambda b: (b, 0, 0))),
        compiler_params=_params(("parallel",)),
        name="pool_prompt",
    )(u, w_pool, scale)


def _pool_decode_kernel(u_ref, buf_ref, w_ref, s_ref, y_ref, nb_ref):
    u_all = u_ref[...]
    for g, win in enumerate(POOL_WINDOWS):
        cols = slice(g * POOL_G, (g + 1) * POOL_G)
        u = u_all[:, cols]
        acc = u
        for j in range(POOL_BUF - (win - 1), POOL_BUF):
            acc = acc + buf_ref[j, :, cols]
        cnt = float(min(win, PAST_LEN + 1))
        d = (acc / cnt - u).astype(BF16)
        y = _dot(d, w_ref[g]) * s_ref[:, cols]
        y_ref[:, cols] = y.astype(y_ref.dtype)
    for j in range(POOL_BUF - 1):
        nb_ref[j] = buf_ref[j + 1]
    nb_ref[POOL_BUF - 1] = u_all


def _pool_decode(u, buf_t, w_pool, scale):
    n = u.shape[0]
    return pl.pallas_call(
        _pool_decode_kernel,
        out_shape=(jax.ShapeDtypeStruct((n, POOL_W), BF16),
                   jax.ShapeDtypeStruct((POOL_BUF, n, POOL_W), F32)),
        grid=(1,),
        in_specs=[_const_spec((n, POOL_W)), _const_spec((POOL_BUF, n, POOL_W)),
                  _const_spec((len(POOL_WINDOWS), POOL_G, POOL_G)), _const_spec((1, POOL_W))],
        out_specs=(pl.BlockSpec((n, POOL_W), lambda i: (0, 0)),
                   pl.BlockSpec((POOL_BUF, n, POOL_W), lambda i: (0, 0, 0))),
        compiler_params=_params(("arbitrary",)),
        name="pool_decode",
    )(u, buf_t, w_pool, scale)


COL_M, COL_INTER, COL_EINV, COL_END, COL_DECAY = (i * HEADS for i in range(5))


def _scan_lanes(x, op, fill):
    t_len = x.shape[-1]
    lane = lax.broadcasted_iota(jnp.int32, x.shape, 1)
    k = 1
    while k < t_len:
        shifted = pltpu.roll(x, k, axis=1)
        x = op(x, jnp.where(lane >= k, shifted, fill))
        k *= 2
    return x


def _gates_kernel(gift_ref, bif_ref, row_ref, col_ref, mlast_ref, pack_ref):
    t_len = gift_ref.shape[1]
    L = CHUNK
    g = gift_ref[...] + bif_ref[...]
    ig = g[0:HEADS, :]
    lf = _log_sigmoid(g[HEADS:2 * HEADS, :])
    b_cum = _scan_lanes(lf, jnp.add, 0.0)
    a = ig - b_cum
    m_run = jnp.maximum(_scan_lanes(a, jnp.maximum, NEG), 0.0)
    m_tot = b_cum + m_run
    row_ref[0, 0:HEADS, :] = a
    row_ref[0, HEADS:2 * HEADS, :] = m_run
    mlast_ref[0] = m_tot[:, t_len - 1:t_len]
    pack_ref[...] = jnp.zeros(pack_ref.shape, F32)
    pack_ref[COL_M:COL_M + HEADS, :] = m_run
    pack_ref[COL_EINV:COL_EINV + HEADS, :] = jnp.exp(-m_tot)
    for c in range(t_len // L):
        lo, hi = c * L, (c + 1) * L
        m_prev = jnp.zeros((HEADS, 1), F32) if c == 0 else m_run[:, lo - 1:lo]
        m_end = m_run[:, hi - 1:hi]
        pack_ref[COL_INTER:COL_INTER + HEADS, lo:hi] = jnp.exp(m_prev - m_run[:, lo:hi])
        pack_ref[COL_END:COL_END + HEADS, lo:hi] = jnp.exp(a[:, lo:hi] - m_end)
        pack_ref[COL_DECAY:COL_DECAY + HEADS, lo:hi] = jnp.broadcast_to(
            jnp.exp(m_prev - m_end), (HEADS, L))
    for c in range(t_len // 128):
        col_ref[0, c * 128:(c + 1) * 128, :] = pack_ref[:, c * 128:(c + 1) * 128].T


def _gates(gift, bif, batch, t_len):
    return pl.pallas_call(
        _gates_kernel,
        out_shape=(jax.ShapeDtypeStruct((batch, 2 * HEADS, t_len), F32),
                   jax.ShapeDtypeStruct((batch, t_len, 128), F32),
                   jax.ShapeDtypeStruct((batch, HEADS, 1), F32)),
        grid=(batch,),
        in_specs=[pl.BlockSpec((2 * HEADS, t_len), lambda b: (0, b)), _const_spec((2 * HEADS, 1))],
        out_specs=(pl.BlockSpec((1, 2 * HEADS, t_len), lambda b: (b, 0, 0)),
                   pl.BlockSpec((1, t_len, 128), lambda b: (b, 0, 0)),
                   pl.BlockSpec((1, HEADS, 1), lambda b: (b, 0, 0))),
        scratch_shapes=[pltpu.VMEM((128, t_len), F32)],
        compiler_params=_params(("parallel",)),
        name="mlstm_gates",
    )(gift, bif)


def _mlstm_prompt_body(n_chunks, q_ref, k_ref, v_ref, og_ref, row_ref, col_ref, y_ref, c_ref, n_ref):
    c_idx = pl.program_id(0) % n_chunks
    L = q_ref.shape[1]

    @pl.when(c_idx == 0)
    def _():
        c_ref[...] = jnp.zeros(c_ref.shape, F32)
        n_ref[...] = jnp.zeros(n_ref.shape, F32)

    t_idx = lax.broadcasted_iota(jnp.int32, (L, L), 0)
    s_idx = lax.broadcasted_iota(jnp.int32, (L, L), 1)
    causal = s_idx <= t_idx
    for b in range(q_ref.shape[0]):
        for h in range(HEADS):
            hs = slice(h * DH, (h + 1) * DH)
            qh = q_ref[b, :, hs]
            kh = k_ref[b, :, hs]
            vh = v_ref[b, :, hs]
            a_row = row_ref[b, h:h + 1, :]
            m_col = col_ref[b, :, COL_M + h:COL_M + h + 1]
            w_inter = col_ref[b, :, COL_INTER + h:COL_INTER + h + 1]
            einv = col_ref[b, :, COL_EINV + h:COL_EINV + h + 1]
            w_end = col_ref[b, :, COL_END + h:COL_END + h + 1]
            decay = col_ref[b, 0:1, COL_DECAY + h:COL_DECAY + h + 1]

            dmat = jnp.exp(jnp.where(causal, a_row - m_col, NEG))
            sw = _dot_nt(qh, kh) * dmat
            c_old = c_ref[b, h]
            n_old = n_ref[b, h]
            inter = _dot_nt(qh, c_old.astype(BF16))
            num = _dot(sw.astype(BF16), vh) + w_inter * inter
            nq = jnp.sum(qh.astype(F32) * n_old, axis=-1, keepdims=True)
            den = jnp.sum(sw, axis=-1, keepdims=True) + w_inter * nq
            r = 1.0 / jnp.maximum(jnp.abs(den), einv)
            y_ref[b, :, hs] = (og_ref[b, :, hs] * (num * r)).astype(y_ref.dtype)

            vw = (vh.astype(F32) * w_end).astype(BF16)
            c_ref[b, h] = decay * c_old + _dot_tn(vw, kh)
            n_ref[b, h] = decay * n_old + jnp.sum(kh.astype(F32) * w_end, axis=0, keepdims=True)


def _mlstm_prompt_operands(q, k, v, og, rowp, colp, t_len, b_lo, nb, gb):
    L = CHUNK
    nc = t_len // L
    batch = q.shape[0] // t_len
    g0 = b_lo // gb
    assert b_lo % gb == 0 and nb % gb == 0
    q3, k3, v3, og3 = (a.reshape(batch, t_len, D) for a in (q, k, v, og))
    tok_in = lambda i: (g0 + i // nc, i % nc, 0)
    tok_out = lambda i: (i // nc, i % nc, 0)
    in_specs = [pl.BlockSpec((gb, L, D), tok_in), pl.BlockSpec((gb, L, D), tok_in),
                pl.BlockSpec((gb, L, D), tok_in), pl.BlockSpec((gb, L, D), tok_in),
                pl.BlockSpec((gb, 2 * HEADS, L), lambda i: (g0 + i // nc, 0, i % nc)),
                pl.BlockSpec((gb, L, 128), tok_in)]
    out_shape = (jax.ShapeDtypeStruct((nb, t_len, D), BF16),
                 jax.ShapeDtypeStruct((nb, HEADS, DH, DH), F32),
                 jax.ShapeDtypeStruct((nb, HEADS, 1, DH), F32))
    out_specs = (pl.BlockSpec((gb, L, D), tok_out),
                 pl.BlockSpec((gb, HEADS, DH, DH), lambda i: (i // nc, 0, 0, 0)),
                 pl.BlockSpec((gb, HEADS, 1, DH), lambda i: (i // nc, 0, 0, 0)))
    body = functools.partial(_mlstm_prompt_body, nc)
    return body, (q3, k3, v3, og3, rowp, colp), in_specs, out_shape, out_specs, (nb // gb) * nc


def _mlstm_prompt(q, k, v, og, rowp, colp, t_len, b_lo, nb):
    body, args, in_specs, out_shape, out_specs, steps = _mlstm_prompt_operands(
        q, k, v, og, rowp, colp, t_len, b_lo, nb, MLSTM_GROUP)
    y, c_fin, n_fin = pl.pallas_call(
        body,
        out_shape=out_shape,
        grid=(steps,),
        in_specs=in_specs,
        out_specs=out_specs,
        compiler_params=_params(("arbitrary",)),
        name="mlstm_prompt",
    )(*args)
    return y.reshape(nb * t_len, D), c_fin, n_fin


def _mlstm_decode_body(q_ref, k_ref, v8_ref, og8_ref, gif_ref, bif_ref, c0_ref, n0_ref, m0_ref,
                       y8_ref, c_out_ref, n_out_ref, m_out_ref, vt_ref, ht_ref):
    vrep = jnp.concatenate([v8_ref[0]] * 16, axis=0)
    for r in range(D // 128):
        rs = slice(r * 128, (r + 1) * 128)
        vt_ref[rs, :] = vrep[:, rs].T
    ht_ref[...] = jnp.zeros(ht_ref.shape, F32)
    for j in range(q_ref.shape[0]):
        gi = gif_ref[j] + bif_ref[...]
        m0 = m0_ref[j]
        for h in range(HEADS):
            hs = slice(h * DH, (h + 1) * DH)
            qr = q_ref[j, :, hs]
            kr = k_ref[j, :, hs]
            vc = vt_ref[hs, j:j + 1]
            ig = gi[:, h:h + 1]
            lf = _log_sigmoid(gi[:, HEADS + h:HEADS + h + 1])
            m_old = m0[:, h:h + 1]
            m_new = jnp.maximum(lf + m_old, ig)
            w_i = jnp.exp(ig - m_new)
            w_f = jnp.exp(lf + m_old - m_new)
            c_old = c0_ref[j, h]
            n_old = n0_ref[j, h:h + 1, :]
            cq = jnp.sum(c_old * qr, axis=-1, keepdims=True)
            qk = jnp.sum(qr * kr, axis=-1, keepdims=True)
            nq = jnp.sum(n_old * qr, axis=-1, keepdims=True)
            num = (w_i * qk) * vc + w_f * cq
            den = w_i * qk + w_f * nq
            hcol = num / jnp.maximum(jnp.abs(den), jnp.exp(-m_new))
            ht_ref[hs, j:j + 1] = hcol
            c_out_ref[j, h] = w_f * c_old + (w_i * vc) * kr
            n_out_ref[j, h:h + 1, :] = w_f * n_old + w_i * kr
            m_out_ref[j, :, h:h + 1] = m_new
    for r in range(D // 128):
        rs = slice(r * 128, (r + 1) * 128)
        y8_ref[0, :, rs] = (og8_ref[0, :, rs] * ht_ref[rs, :].T[0:8, :]).astype(y8_ref.dtype)


MLSTM_DEC_SCRATCH = [pltpu.VMEM((D, 128), F32), pltpu.VMEM((D, 128), F32)]


def _mlstm_decode_operands(q, k, v, og, gifc, bif_row, c0, n0, m0, bb):
    n = q.shape[0]
    assert bb <= 8
    r3 = lambda i: (i, 0, 0)
    r4 = lambda i: (i, 0, 0, 0)
    q3, k3 = (a.reshape(n, 1, D) for a in (q, k))
    v8, og8 = (jnp.pad(a.reshape(n // bb, bb, D), ((0, 0), (0, 8 - bb), (0, 0))) for a in (v, og))
    args = (q3, k3, v8, og8, gifc.reshape(n, 1, 2 * HEADS), bif_row, c0, n0,
            m0.reshape(n, 1, HEADS))
    in_specs = [pl.BlockSpec((bb, 1, D), r3), pl.BlockSpec((bb, 1, D), r3),
                pl.BlockSpec((1, 8, D), r3), pl.BlockSpec((1, 8, D), r3),
                pl.BlockSpec((bb, 1, 2 * HEADS), r3), _const_spec((1, 2 * HEADS)),
                pl.BlockSpec((bb, HEADS, DH, DH), r4), pl.BlockSpec((bb, HEADS, DH), r3),
                pl.BlockSpec((bb, 1, HEADS), r3)]
    out_shape = (jax.ShapeDtypeStruct((n // bb, 8, D), BF16),
                 jax.ShapeDtypeStruct((n, HEADS, DH, DH), F32),
                 jax.ShapeDtypeStruct((n, HEADS, DH), F32),
                 jax.ShapeDtypeStruct((n, 1, HEADS), F32))
    out_specs = (pl.BlockSpec((1, 8, D), r3), pl.BlockSpec((bb, HEADS, DH, DH), r4),
                 pl.BlockSpec((bb, HEADS, DH), r3), pl.BlockSpec((bb, 1, HEADS), r3))
    return args, in_specs, out_shape, out_specs


def _mlstm_decode_results(y8, c_new, n_new, m_new):
    n = c_new.shape[0]
    bb = n // y8.shape[0]
    return y8[:, :bb].reshape(n, D), c_new, n_new, m_new.reshape(n, HEADS)


def _xattn_prompt_kernel(xq_ref, mk_ref, mv_ref, y_ref):
    scale = XDH ** -0.5
    for h in range(XH):
        hs = slice(h * XDH, (h + 1) * XDH)
        s = _dot_nt(xq_ref[:, hs], mk_ref[:, hs].astype(BF16)) * scale
        p = jnp.exp(s - jnp.max(s, axis=-1, keepdims=True))
        l = jnp.sum(p, axis=-1, keepdims=True)
        o = _dot(p.astype(BF16), mv_ref[:, hs].astype(BF16)) / l
        y_ref[:, hs] = o.astype(y_ref.dtype)


def _xattn_prompt(xq, mk, mv, batch, t_len):
    tq = TM_PROMPT
    nq = t_len // tq
    return pl.pallas_call(
        _xattn_prompt_kernel,
        out_shape=jax.ShapeDtypeStruct((batch * t_len, XW), BF16),
        grid=(batch, nq),
        in_specs=[pl.BlockSpec((tq, XW), lambda b, i: (b * nq + i, 0)),
                  pl.BlockSpec((N_MEM, XW), lambda b, i: (b, 0)),
                  pl.BlockSpec((N_MEM, XW), lambda b, i: (b, 0))],
        out_specs=pl.BlockSpec((tq, XW), lambda b, i: (b * nq + i, 0)),
        compiler_params=_params(("parallel", "arbitrary")),
        name="xattn_prompt",
    )(xq, mk, mv)


def _xattn_decode_kernel(q_ref, mk_ref, mv_ref, y_ref):
    scale = XDH ** -0.5
    q = q_ref[...][:, None, :, :]
    s = jnp.sum(mk_ref[...] * q, axis=-1, keepdims=True) * scale
    mx = jnp.max(s, axis=1, keepdims=True)
    mx = jnp.maximum(mx, pltpu.roll(mx, XH, axis=2))
    p = jnp.exp(s - mx)
    l = jnp.sum(p, axis=1, keepdims=True)
    l = l + pltpu.roll(l, XH, axis=2)
    o = jnp.sum(p * mv_ref[...], axis=1, keepdims=True)
    o = (o + pltpu.roll(o, XH, axis=2)) / l
    y_ref[...] = o[:, 0, 0:XH, :].astype(y_ref.dtype)


def _xattn_decode_operands(xq, mk, mv, bb):
    n = xq.shape[0]
    xq4 = xq.reshape(n, XH, XDH)
    q2 = jnp.concatenate([xq4, xq4], axis=1)
    mk2, mv2 = (a.reshape(n, N_MEM // 2, 2 * XH, XDH) for a in (mk, mv))
    kv_spec = pl.BlockSpec((bb, N_MEM // 2, 2 * XH, XDH), lambda i: (i, 0, 0, 0))
    in_specs = [pl.BlockSpec((bb, 2 * XH, XDH), lambda i: (i, 0, 0)), kv_spec, kv_spec]
    out_shape = jax.ShapeDtypeStruct((n, XH, XDH), BF16)
    out_spec = pl.BlockSpec((bb, XH, XDH), lambda i: (i, 0, 0))
    return (q2, mk2, mv2), in_specs, out_shape, out_spec


def _with_side_job(host_body, n_host, side_body, n_side, *refs):
    (hi, ho, hs), (si, so, ss) = n_host, n_side
    ins, outs, scr = refs[:hi + si], refs[hi + si:hi + si + ho + so], refs[hi + si + ho + so:]
    assert len(scr) == hs + ss
    side_body(*ins[hi:], *outs[ho:], *scr[hs:])
    host_body(*ins[:hi], *outs[:ho], *scr[:hs])


def _merge_kernel(x_ref, yp_ref, ym_ref, yx_ref, gpre_ref, gpost_ref, wg_ref, wbp_ref, wbm_ref,
                  wbx_ref, wo_ref, o_ref, merged_ref):
    x = x_ref[...]
    h = _rms(x, gpre_ref[...]).astype(BF16)
    yp = yp_ref[...]
    ym = ym_ref[...]
    yx = yx_ref[...]
    nchunk = 256
    for c in range(D // nchunk):
        cs = slice(c * nchunk, (c + 1) * nchunk)
        acc = _sigmoid(_dot_nt(h, wg_ref[c * nchunk:(c + 1) * nchunk, :])) * _dot(yp, wbp_ref[:, cs])
        acc += (_sigmoid(_dot_nt(h, wg_ref[D + c * nchunk:D + (c + 1) * nchunk, :]))
                * _dot(ym, wbm_ref[:, cs]))
        acc += (_sigmoid(_dot_nt(h, wg_ref[2 * D + c * nchunk:2 * D + (c + 1) * nchunk, :]))
                * _dot(yx, wbx_ref[:, cs]))
        merged_ref[:, cs] = acc.astype(BF16)
    o_ref[...] = x + _rms(_dot(merged_ref[...], wo_ref[...]), gpost_ref[...])


def _merge(x2, yp, ym, yx, gpre, gpost, wg, wbp, wbm, wbx, wo, tm, row0=0, side=None):
    n = ym.shape[0]
    steps = n // tm
    blk0 = row0 // tm
    assert row0 % tm == 0
    row = lambda i: (i, 0)
    off = lambda i: (blk0 + i, 0)
    args = (x2, yp, ym, yx, gpre, gpost, wg, wbp, wbm, wbx, wo)
    in_specs = [pl.BlockSpec((tm, D), off), pl.BlockSpec((tm, POOL_W), off),
                pl.BlockSpec((tm, D), row), pl.BlockSpec((tm, XW), off),
                _const_spec((1, D)), _const_spec((1, D)), _const_spec((3 * D, D)),
                _const_spec((POOL_W, D)), _const_spec((D, D)), _const_spec((XW, D)),
                _const_spec((D, D))]
    out_shape = jax.ShapeDtypeStruct((n, D), F32)
    out_specs = pl.BlockSpec((tm, D), row)
    body = _merge_kernel
    scratch = [pltpu.VMEM((tm, D), BF16)]
    if side is not None:
        s_body, s_args, s_in, s_shape, s_out, s_scratch = side
        body = functools.partial(_with_side_job, _merge_kernel, (len(args), 1, 1),
                                 s_body, (len(s_args), len(s_shape), len(s_scratch)))
        args, in_specs = args + tuple(s_args), in_specs + list(s_in)
        out_shape, out_specs = (out_shape,) + tuple(s_shape), (out_specs,) + tuple(s_out)
        scratch = scratch + list(s_scratch)
    return pl.pallas_call(
        body,
        out_shape=out_shape,
        grid=(steps,),
        in_specs=in_specs,
        out_specs=out_specs,
        scratch_shapes=scratch,
        compiler_params=_params(("parallel",)),
        name="merge_out",
    )(*args)


def _ffn_kernel(x_ref, gpre_ref, gpost_ref, w1_ref, w2_ref, o_ref):
    x = x_ref[...]
    h = _rms(x, gpre_ref[...]).astype(BF16)
    fchunk = 1024
    acc = None
    for c in range(D_FF // fchunk):
        a = _dot(h, w1_ref[:, c * fchunk:(c + 1) * fchunk])
        a = jnp.square(jnp.maximum(a, 0.0)).astype(BF16)
        part = _dot(a, w2_ref[c * fchunk:(c + 1) * fchunk, :])
        acc = part if acc is None else acc + part
    o_ref[...] = x + _rms(acc, gpost_ref[...])


def _ffn(x2, gpre, gpost, w1, w2, tm):
    n = x2.shape[0]
    row = lambda i: (i, 0)
    return pl.pallas_call(
        _ffn_kernel,
        out_shape=jax.ShapeDtypeStruct((n, D), F32),
        grid=(n // tm,),
        in_specs=[pl.BlockSpec((tm, D), row), _const_spec((1, D)), _const_spec((1, D)),
                  _const_spec((D, D_FF)), _const_spec((D_FF, D))],
        out_specs=pl.BlockSpec((tm, D), row),
        compiler_params=_params(("parallel",)),
        name="ffn",
    )(x2, gpre, gpost, w1, w2)


def kernel(x_prompt, x_sample, mem_prompt, state_pool_buf, state_mlstm_C, state_mlstm_n, state_mlstm_m, cache_mem_k, cache_mem_v, g_pre_mix, w_in, b_if, w_pool, pool_scale, g_mem, w_mem_kv, w_br_pool, w_br_mlstm, w_br_xattn, w_out, g_post_mix, g_pre_mlp, w_ff1, w_ff2, g_post_mlp):
    batch, t_len, _ = x_prompt.shape
    n_dec = x_sample.shape[0]
    assert w_in.shape[0] == 1, "single layer"

    w_pool_b = w_pool[0].astype(BF16)
    w_kv_b = w_mem_kv[0].astype(BF16)
    g_mix, g_pm, g_mlp, g_pmlp, g_m = (g[0].reshape(1, D) for g in
                                       (g_pre_mix, g_post_mix, g_pre_mlp, g_post_mlp, g_mem))
    scale = pool_scale[0].reshape(1, POOL_W)
    bif = b_if[0]

    w_in_tf = jnp.transpose(w_in[0])
    mk_p, mv_p, mk4_p, mv4_p, w_in_t = _mem_kv(mem_prompt.reshape(batch * N_MEM, D), g_m, w_kv_b,
                                               w_in_tf, O_GATE)
    w_gif_t = jnp.pad(w_in_t[O_GIF:O_XQ], ((0, 128 - 2 * HEADS), (0, 0)))
    w_xq_t = w_in_t[O_XQ:O_GATE]
    w_in_parts = (w_in_t, w_xq_t, w_gif_t)

    xs = x_sample.reshape(n_dec, D)
    u_s, q_s, k_s, v_s, og_s, xq_s, _, gifc_s = _in_proj(xs, g_mix, *w_in_parts, n_dec, F32)
    buf_t = jnp.transpose(state_pool_buf[0], (1, 0, 2))
    ypool_s, buf_s_t = _pool_decode(u_s, buf_t, w_pool_b, scale)
    dec_xattn = (xq_s, cache_mem_k[0], cache_mem_v[0])
    dec_mlstm = (q_s, k_s, v_s, og_s, gifc_s, bif.reshape(1, 2 * HEADS),
                 state_mlstm_C[0], state_mlstm_n[0], state_mlstm_m[0])

    xp = x_prompt.reshape(batch * t_len, D)
    later_weights = [(w[0], 0, w.shape[1]) for w in
                     (w_ff1, w_ff2, w_out, w_br_pool, w_br_mlstm, w_br_xattn)]
    later_weights.append((w_in_tf, O_GATE, 3 * D))
    u, q, k, v, og, xq, gift, _, yx_s, w1, w2, wo, wbp, wbm, wbx, w_gate = _in_proj(
        xp, g_mix, *w_in_parts, TM_INPROJ, BF16, dec=dec_xattn, convert=later_weights)
    yx_s = yx_s.reshape(n_dec, XW)
    y_pool, buf_p = _pool_prompt(u, w_pool_b, scale, batch, t_len)
    rowp, colp, m_last = _gates(gift, bif.reshape(2 * HEADS, 1), batch, t_len)
    y_x = _xattn_prompt(xq, mk_p, mv_p, batch, t_len)
    merge_w = (g_mix, g_pm, w_gate, wbp, wbm, wbx, wo)
    y_ml, c_p, n_p = _mlstm_prompt(q, k, v, og, rowp, colp, t_len, 0, batch)
    d_args, d_in, d_shape, d_out = _mlstm_decode_operands(
        *dec_mlstm, n_dec // (batch * t_len // TM_MERGE))
    side = (_mlstm_decode_body, d_args, d_in, d_shape, d_out, MLSTM_DEC_SCRATCH)
    x1, *dec_out = _merge(xp, y_pool, y_ml, y_x, *merge_w, TM_MERGE, 0, side)
    yp = _ffn(x1, g_mlp, g_pmlp, w1, w2, TM_PROMPT)

    yml_s, c_s, n_s, m_s = _mlstm_decode_results(*dec_out)
    x1_s = _merge(xs, ypool_s, yml_s, yx_s, *merge_w, n_dec)
    ys = _ffn(x1_s, g_mlp, g_pmlp, w1, w2, n_dec)

    return (yp.reshape(batch, t_len, D), ys.reshape(n_dec, 1, D),
            buf_p[None], c_p[None], n_p.reshape(1, batch, HEADS, DH),
            m_last.reshape(1, batch, HEADS),
            mk4_p.reshape(1, batch, N_MEM, XH, XDH), mv4_p.reshape(1, batch, N_MEM, XH, XDH),
            jnp.transpose(buf_s_t, (1, 0, 2))[None], c_s[None], n_s[None], m_s[None])
```

```python
import functools

import jax
import jax.numpy as jnp
from jax import lax
from jax.experimental import pallas as pl
from jax.experimental.pallas import tpu as pltpu

F32 = jnp.float32
BF16 = jnp.bfloat16

D = 1024
POOL_W = 512
POOL_G = 128
POOL_WINDOWS = (2, 4, 8, 16)
POOL_BUF = 15
HEADS = 4
DH = 256
XH = 4
XDH = 128
XW = 512
N_MEM = 256
D_FF = 4096
EPS = 1e-6
NEG = -1e30
PAST_LEN = 16384

TM_PROMPT = 1024
TM_INPROJ = 512
TM_MERGE = 512
TM_FFN = 512
CHUNK = 256
MLSTM_GROUP = 4

VMEM_LIMIT = 56 * 1024 * 1024

O_QKVO = POOL_W
O_GIF = O_QKVO + 4 * D
O_XQ = O_GIF + 2 * HEADS
O_GATE = O_XQ + XW


def _params(sem):
    return pltpu.CompilerParams(dimension_semantics=sem, vmem_limit_bytes=VMEM_LIMIT)


def _const_spec(shape):
    nd = len(shape)
    return pl.BlockSpec(shape, lambda *_: (0,) * nd, pipeline_mode=pl.Buffered(1))


def _rms(x, g):
    ms = jnp.mean(x * x, axis=-1, keepdims=True)
    return x * lax.rsqrt(ms + EPS) * g


def _log_sigmoid(x):
    return jnp.minimum(x, 0.0) - jnp.log(1.0 + jnp.exp(-jnp.abs(x)))


def _sigmoid(x):
    return 1.0 / (1.0 + jnp.exp(-x))


def _dot(a, b):
    return jnp.dot(a, b, preferred_element_type=F32)


def _dot_nt(a, b):
    return lax.dot_general(a, b, (((1,), (1,)), ((), ())), preferred_element_type=F32)


def _dot_tn(a, b):
    return lax.dot_general(a, b, (((0,), (0,)), ((), ())), preferred_element_type=F32)


def _with_side_job(host_body, n_host, side_body, n_side, *refs):
    (hi, ho, hs), (si, so, ss) = n_host, n_side
    ins, outs, scr = refs[:hi + si], refs[hi + si:hi + si + ho + so], refs[hi + si + ho + so:]
    assert len(scr) == hs + ss
    side_body(*ins[hi:], *outs[ho:], *scr[hs:])
    host_body(*ins[:hi], *outs[:ho], *scr[:hs])


def _attach_side(body, args, in_specs, out_shape, out_specs, scratch, side):
    if side is None:
        return body, args, in_specs, out_shape, out_specs, scratch
    s_body, s_args, s_in, s_shape, s_out, s_scratch = side
    body = functools.partial(_with_side_job, body, (len(args), len(out_shape), len(scratch)),
                             s_body, (len(s_args), len(s_shape), len(s_scratch)))
    return (body, tuple(args) + tuple(s_args), list(in_specs) + list(s_in),
            tuple(out_shape) + tuple(s_shape), tuple(out_specs) + tuple(s_out),
            list(scratch) + list(s_scratch))


def _join_sides(a, b):
    na, nb = (len(a[1]), len(a[3]), len(a[5])), (len(b[1]), len(b[3]), len(b[5]))

    def body(*refs):
        ins, outs = refs[:na[0] + nb[0]], refs[na[0] + nb[0]:na[0] + nb[0] + na[1] + nb[1]]
        scr = refs[na[0] + nb[0] + na[1] + nb[1]:]
        a[0](*ins[:na[0]], *outs[:na[1]], *scr[:na[2]])
        b[0](*ins[na[0]:], *outs[na[1]:], *scr[na[2]:])

    return (body, tuple(a[1]) + tuple(b[1]), list(a[2]) + list(b[2]), tuple(a[3]) + tuple(b[3]),
            tuple(a[4]) + tuple(b[4]), list(a[5]) + list(b[5]))


def _convert_body(*refs):
    half = len(refs) // 2
    for src, dst in zip(refs[:half], refs[half:]):
        dst[...] = src[...].astype(dst.dtype)


def _convert_side(weights, steps):
    row = lambda i: (i, 0)
    args, in_specs, out_shape, out_specs = (), [], (), ()
    for w, row0, nrows in weights:
        per = nrows // steps
        spec = pl.BlockSpec((per, w.shape[1]), row)
        src = spec if row0 == 0 and nrows == w.shape[0] else pl.BlockSpec(
            (pl.Element(per), pl.Element(w.shape[1])),
            lambda i, r=row0, p=per: (pl.multiple_of(r + i * p, 8), 0))
        args, in_specs = args + (w,), in_specs + [src]
        out_shape += (jax.ShapeDtypeStruct((nrows, w.shape[1]), BF16),)
        out_specs += (spec,)
    return _convert_body, args, in_specs, out_shape, out_specs, []


def _in_proj_kernel(x_ref, g_ref, w_ref, wxq_ref, wgif_ref, u_ref, q_ref, k_ref, v_ref, og_ref,
                    xq_ref, gift_ref, gifc_ref):
    h = _rms(x_ref[...], g_ref[...]).astype(BF16)

    def seg(lo, n):
        return _dot_nt(h, w_ref[lo:lo + n, :])

    u_ref[...] = seg(0, POOL_W)
    q_ref[...] = seg(POOL_W, D).astype(q_ref.dtype)
    k_ref[...] = (seg(POOL_W + D, D) * (DH ** -0.5)).astype(k_ref.dtype)
    v_ref[...] = seg(POOL_W + 2 * D, D).astype(v_ref.dtype)
    og_ref[...] = _sigmoid(seg(POOL_W + 3 * D, D))
    xq_ref[...] = _dot_nt(h, wxq_ref[...]).astype(xq_ref.dtype)
    gif = _dot_nt(h, wgif_ref[...])
    gifc_ref[...] = gif[:, 0:2 * HEADS]
    for r in range(gif.shape[0] // 128):
        gift_ref[:, r * 128:(r + 1) * 128] = gif[r * 128:(r + 1) * 128, :].T[0:2 * HEADS, :]


def _in_proj(x2, g, w_main_t, w_xq_t, w_gif_t, tm, qkv_dtype, side=None):
    n = x2.shape[0]
    steps = n // tm
    row = lambda i: (i, 0)
    out_shape = (
        jax.ShapeDtypeStruct((n, POOL_W), F32),
        jax.ShapeDtypeStruct((n, D), qkv_dtype),
        jax.ShapeDtypeStruct((n, D), qkv_dtype),
        jax.ShapeDtypeStruct((n, D), qkv_dtype),
        jax.ShapeDtypeStruct((n, D), F32),
        jax.ShapeDtypeStruct((n, XW), qkv_dtype),
        jax.ShapeDtypeStruct((2 * HEADS, n), F32),
        jax.ShapeDtypeStruct((n, 2 * HEADS), F32),
    )
    args = (x2, g, w_main_t, w_xq_t, w_gif_t)
    in_specs = [
        pl.BlockSpec((tm, D), row),
        _const_spec((1, D)),
        _const_spec((O_GIF, D)),
        _const_spec(w_xq_t.shape),
        _const_spec(w_gif_t.shape),
    ]
    out_specs = (
        pl.BlockSpec((tm, POOL_W), row),
        pl.BlockSpec((tm, D), row),
        pl.BlockSpec((tm, D), row),
        pl.BlockSpec((tm, D), row),
        pl.BlockSpec((tm, D), row),
        pl.BlockSpec((tm, XW), row),
        pl.BlockSpec((2 * HEADS, tm), lambda i: (0, i)),
        pl.BlockSpec((tm, 2 * HEADS), row),
    )
    body, args, in_specs, out_shape, out_specs, scratch = _attach_side(
        _in_proj_kernel, args, in_specs, out_shape, out_specs, [], side)
    return pl.pallas_call(
        body,
        out_shape=out_shape,
        grid=(steps,),
        in_specs=in_specs,
        out_specs=out_specs,
        scratch_shapes=scratch,
        compiler_params=_params(("parallel",)),
        name="in_proj",
    )(*args)


MEM_KV_STEPS = 16


def _mem_kv_kernel(mem_ref, g_ref, w_ref, wt_ref, k_ref, v_ref, k4_ref, v4_ref, wtb_ref):
    wtb_ref[...] = wt_ref[...].astype(BF16)
    h = _rms(mem_ref[...], g_ref[...]).astype(BF16)
    kv = _dot(h, w_ref[...])
    k_ref[...] = kv[:, :XW]
    v_ref[...] = kv[:, XW:]
    for hd in range(XH):
        k4_ref[:, hd, :] = kv[:, hd * XDH:(hd + 1) * XDH]
        v4_ref[:, hd, :] = kv[:, XW + hd * XDH:XW + (hd + 1) * XDH]


def _mem_kv(mem2, g, w, w_in_t, n_rows):
    n = mem2.shape[0]
    tm = n // MEM_KV_STEPS
    wr = -(-n_rows // (MEM_KV_STEPS * 16)) * 16
    row = lambda i: (i, 0)
    row3 = lambda i: (i, 0, 0)
    return pl.pallas_call(
        _mem_kv_kernel,
        out_shape=(jax.ShapeDtypeStruct((n, XW), F32), jax.ShapeDtypeStruct((n, XW), F32),
                   jax.ShapeDtypeStruct((n, XH, XDH), F32), jax.ShapeDtypeStruct((n, XH, XDH), F32),
                   jax.ShapeDtypeStruct((n_rows, D), BF16)),
        grid=(MEM_KV_STEPS,),
        in_specs=[pl.BlockSpec((tm, D), row), _const_spec((1, D)), _const_spec((D, 2 * XW)),
                  pl.BlockSpec((wr, D), row)],
        out_specs=(pl.BlockSpec((tm, XW), row), pl.BlockSpec((tm, XW), row),
                   pl.BlockSpec((tm, XH, XDH), row3), pl.BlockSpec((tm, XH, XDH), row3),
                   pl.BlockSpec((wr, D), row)),
        compiler_params=_params(("parallel",)),
        name="mem_kv",
    )(mem2, g, w, w_in_t)


def _pool_prompt_kernel(u_ref, w_ref, s_ref, y_ref, nb_ref):
    t_len = u_ref.shape[0]
    t_idx = lax.broadcasted_iota(jnp.int32, (t_len, POOL_G), 0)
    for g, win in enumerate(POOL_WINDOWS):
        cols = slice(g * POOL_G, (g + 1) * POOL_G)
        u = u_ref[:, cols]
        acc = u
        span = 1
        while span < win:
            shifted = pltpu.roll(acc, span, axis=0)
            acc = acc + jnp.where(t_idx >= span, shifted, 0.0)
            span *= 2
        cnt = jnp.minimum(t_idx + 1, win).astype(F32)
        d = (acc / cnt - u).astype(BF16)
        y = _dot(d, w_ref[g]) * s_ref[:, cols]
        y_ref[:, cols] = y.astype(y_ref.dtype)
    nb_ref[0] = u_ref[t_len - POOL_BUF:, :]


def _pool_prompt(u, w_pool, scale, batch, t_len):
    return pl.pallas_call(
        _pool_prompt_kernel,
        out_shape=(jax.ShapeDtypeStruct((batch * t_len, POOL_W), BF16),
                   jax.ShapeDtypeStruct((batch, POOL_BUF, POOL_W), F32)),
        grid=(batch,),
        in_specs=[pl.BlockSpec((t_len, POOL_W), lambda b: (b, 0)),
                  _const_spec((len(POOL_WINDOWS), POOL_G, POOL_G)),
                  _const_spec((1, POOL_W))],
        out_specs=(pl.BlockSpec((t_len, POOL_W), lambda b: (b, 0)),
                   pl.BlockSpec((1, POOL_BUF, POOL_W), lambda b: (b, 0, 0))),
        compiler_params=_params(("parallel",)),
        name="pool_prompt",
    )(u, w_pool, scale)


def _pool_decode_kernel(u_ref, buf_ref, w_ref, s_ref, y_ref, nb_ref):
    u_all = u_ref[...]
    for g, win in enumerate(POOL_WINDOWS):
        cols = slice(g * POOL_G, (g + 1) * POOL_G)
        u = u_all[:, cols]
        acc = u
        for j in range(POOL_BUF - (win - 1), POOL_BUF):
            acc = acc + buf_ref[j, :, cols]
        cnt = float(min(win, PAST_LEN + 1))
        d = (acc / cnt - u).astype(BF16)
        y = _dot(d, w_ref[g]) * s_ref[:, cols]
        y_ref[:, cols] = y.astype(y_ref.dtype)
    for j in range(POOL_BUF - 1):
        nb_ref[j] = buf_ref[j + 1]
    nb_ref[POOL_BUF - 1] = u_all


def _pool_decode(u, buf_t, w_pool, scale):
    n = u.shape[0]
    return pl.pallas_call(
        _pool_decode_kernel,
        out_shape=(jax.ShapeDtypeStruct((n, POOL_W), BF16),
                   jax.ShapeDtypeStruct((POOL_BUF, n, POOL_W), F32)),
        grid=(1,),
        in_specs=[_const_spec((n, POOL_W)), _const_spec((POOL_BUF, n, POOL_W)),
                  _const_spec((len(POOL_WINDOWS), POOL_G, POOL_G)), _const_spec((1, POOL_W))],
        out_specs=(pl.BlockSpec((n, POOL_W), lambda i: (0, 0)),
                   pl.BlockSpec((POOL_BUF, n, POOL_W), lambda i: (0, 0, 0))),
        compiler_params=_params(("arbitrary",)),
        name="pool_decode",
    )(u, buf_t, w_pool, scale)


COL_M, COL_INTER, COL_EINV, COL_END, COL_DECAY = (i * HEADS for i in range(5))


def _scan_lanes(x, op, fill):
    t_len = x.shape[-1]
    lane = lax.broadcasted_iota(jnp.int32, x.shape, 1)
    k = 1
    while k < t_len:
        shifted = pltpu.roll(x, k, axis=1)
        x = op(x, jnp.where(lane >= k, shifted, fill))
        k *= 2
    return x


def _gates_kernel(gift_ref, bif_ref, row_ref, col_ref, mlast_ref, pack_ref):
    t_len = gift_ref.shape[1]
    L = CHUNK
    g = gift_ref[...] + bif_ref[...]
    ig = g[0:HEADS, :]
    lf = _log_sigmoid(g[HEADS:2 * HEADS, :])
    b_cum = _scan_lanes(lf, jnp.add, 0.0)
    a = ig - b_cum
    m_run = jnp.maximum(_scan_lanes(a, jnp.maximum, NEG), 0.0)
    m_tot = b_cum + m_run
    row_ref[0, 0:HEADS, :] = a
    row_ref[0, HEADS:2 * HEADS, :] = m_run
    mlast_ref[0] = m_tot[:, t_len - 1:t_len]
    pack_ref[...] = jnp.zeros(pack_ref.shape, F32)
    pack_ref[COL_M:COL_M + HEADS, :] = m_run
    pack_ref[COL_EINV:COL_EINV + HEADS, :] = jnp.exp(-m_tot)
    for c in range(t_len // L):
        lo, hi = c * L, (c + 1) * L
        m_prev = jnp.zeros((HEADS, 1), F32) if c == 0 else m_run[:, lo - 1:lo]
        m_end = m_run[:, hi - 1:hi]
        pack_ref[COL_INTER:COL_INTER + HEADS, lo:hi] = jnp.exp(m_prev - m_run[:, lo:hi])
        pack_ref[COL_END:COL_END + HEADS, lo:hi] = jnp.exp(a[:, lo:hi] - m_end)
        pack_ref[COL_DECAY:COL_DECAY + HEADS, lo:hi] = jnp.broadcast_to(
            jnp.exp(m_prev - m_end), (HEADS, L))
    for c in range(t_len // 128):
        col_ref[0, c * 128:(c + 1) * 128, :] = pack_ref[:, c * 128:(c + 1) * 128].T


def _gates(gift, bif, batch, t_len):
    return pl.pallas_call(
        _gates_kernel,
        out_shape=(jax.ShapeDtypeStruct((batch, 2 * HEADS, t_len), F32),
                   jax.ShapeDtypeStruct((batch, t_len, 128), F32),
                   jax.ShapeDtypeStruct((batch, HEADS, 1), F32)),
        grid=(batch,),
        in_specs=[pl.BlockSpec((2 * HEADS, t_len), lambda b: (0, b)), _const_spec((2 * HEADS, 1))],
        out_specs=(pl.BlockSpec((1, 2 * HEADS, t_len), lambda b: (b, 0, 0)),
                   pl.BlockSpec((1, t_len, 128), lambda b: (b, 0, 0)),
                   pl.BlockSpec((1, HEADS, 1), lambda b: (b, 0, 0))),
        scratch_shapes=[pltpu.VMEM((128, t_len), F32)],
        compiler_params=_params(("parallel",)),
        name="mlstm_gates",
    )(gift, bif)


def _mlstm_prompt_body(n_chunks, q_ref, k_ref, v_ref, og_ref, row_ref, col_ref, y_ref, c_ref, n_ref):
    c_idx = pl.program_id(0) % n_chunks
    L = q_ref.shape[1]

    @pl.when(c_idx == 0)
    def _():
        c_ref[...] = jnp.zeros(c_ref.shape, F32)
        n_ref[...] = jnp.zeros(n_ref.shape, F32)

    t_idx = lax.broadcasted_iota(jnp.int32, (L, L), 0)
    s_idx = lax.broadcasted_iota(jnp.int32, (L, L), 1)
    causal = s_idx <= t_idx
    for b in range(q_ref.shape[0]):
        for h in range(HEADS):
            hs = slice(h * DH, (h + 1) * DH)
            qh = q_ref[b, :, hs]
            kh = k_ref[b, :, hs]
            vh = v_ref[b, :, hs]
            a_row = row_ref[b, h:h + 1, :]
            m_col = col_ref[b, :, COL_M + h:COL_M + h + 1]
            w_inter = col_ref[b, :, COL_INTER + h:COL_INTER + h + 1]
            einv = col_ref[b, :, COL_EINV + h:COL_EINV + h + 1]
            w_end = col_ref[b, :, COL_END + h:COL_END + h + 1]
            decay = col_ref[b, 0:1, COL_DECAY + h:COL_DECAY + h + 1]

            dmat = jnp.exp(jnp.where(causal, a_row - m_col, NEG))
            sw = _dot_nt(qh, kh) * dmat
            c_old = c_ref[b, h]
            n_old = n_ref[b, h]
            inter = _dot_nt(qh, c_old.astype(BF16))
            num = _dot(sw.astype(BF16), vh) + w_inter * inter
            nq = jnp.sum(qh.astype(F32) * n_old, axis=-1, keepdims=True)
            den = jnp.sum(sw, axis=-1, keepdims=True) + w_inter * nq
            r = 1.0 / jnp.maximum(jnp.abs(den), einv)
            y_ref[b, :, hs] = (og_ref[b, :, hs] * (num * r)).astype(y_ref.dtype)

            vw = (vh.astype(F32) * w_end).astype(BF16)
            c_ref[b, h] = decay * c_old + _dot_tn(vw, kh)
            n_ref[b, h] = decay * n_old + jnp.sum(kh.astype(F32) * w_end, axis=0, keepdims=True)


def _mlstm_prompt_operands(q, k, v, og, rowp, colp, t_len, b_lo, nb, gb):
    L = CHUNK
    nc = t_len // L
    batch = q.shape[0] // t_len
    g0 = b_lo // gb
    assert b_lo % gb == 0 and nb % gb == 0
    q3, k3, v3, og3 = (a.reshape(batch, t_len, D) for a in (q, k, v, og))
    tok_in = lambda i: (g0 + i // nc, i % nc, 0)
    tok_out = lambda i: (i // nc, i % nc, 0)
    in_specs = [pl.BlockSpec((gb, L, D), tok_in), pl.BlockSpec((gb, L, D), tok_in),
                pl.BlockSpec((gb, L, D), tok_in), pl.BlockSpec((gb, L, D), tok_in),
                pl.BlockSpec((gb, 2 * HEADS, L), lambda i: (g0 + i // nc, 0, i % nc)),
                pl.BlockSpec((gb, L, 128), tok_in)]
    out_shape = (jax.ShapeDtypeStruct((nb, t_len, D), BF16),
                 jax.ShapeDtypeStruct((nb, HEADS, DH, DH), F32),
                 jax.ShapeDtypeStruct((nb, HEADS, 1, DH), F32))
    out_specs = (pl.BlockSpec((gb, L, D), tok_out),
                 pl.BlockSpec((gb, HEADS, DH, DH), lambda i: (i // nc, 0, 0, 0)),
                 pl.BlockSpec((gb, HEADS, 1, DH), lambda i: (i // nc, 0, 0, 0)))
    body = functools.partial(_mlstm_prompt_body, nc)
    return body, (q3, k3, v3, og3, rowp, colp), in_specs, out_shape, out_specs, (nb // gb) * nc


def _mlstm_prompt(q, k, v, og, rowp, colp, t_len, b_lo, nb):
    body, args, in_specs, out_shape, out_specs, steps = _mlstm_prompt_operands(
        q, k, v, og, rowp, colp, t_len, b_lo, nb, MLSTM_GROUP)
    y, c_fin, n_fin = pl.pallas_call(
        body,
        out_shape=out_shape,
        grid=(steps,),
        in_specs=in_specs,
        out_specs=out_specs,
        compiler_params=_params(("arbitrary",)),
        name="mlstm_prompt",
    )(*args)
    return y.reshape(nb * t_len, D), c_fin, n_fin


def _mlstm_decode_body(q_ref, k_ref, v8_ref, og8_ref, gif_ref, bif_ref, c0_ref, n0_ref, m0_ref,
                       y8_ref, c_out_ref, n_out_ref, m_out_ref, vt_ref, ht_ref):
    vrep = jnp.concatenate([v8_ref[0]] * 16, axis=0)
    for r in range(D // 128):
        rs = slice(r * 128, (r + 1) * 128)
        vt_ref[rs, :] = vrep[:, rs].T
    ht_ref[...] = jnp.zeros(ht_ref.shape, F32)
    for j in range(q_ref.shape[0]):
        gi = gif_ref[j] + bif_ref[...]
        m0 = m0_ref[j]
        for h in range(HEADS):
            hs = slice(h * DH, (h + 1) * DH)
            qr = q_ref[j, :, hs]
            kr = k_ref[j, :, hs]
            vc = vt_ref[hs, j:j + 1]
            ig = gi[:, h:h + 1]
            lf = _log_sigmoid(gi[:, HEADS + h:HEADS + h + 1])
            m_old = m0[:, h:h + 1]
            m_new = jnp.maximum(lf + m_old, ig)
            w_i = jnp.exp(ig - m_new)
            w_f = jnp.exp(lf + m_old - m_new)
            c_old = c0_ref[j, h]
            n_old = n0_ref[j, h:h + 1, :]
            cq = jnp.sum(c_old * qr, axis=-1, keepdims=True)
            qk = jnp.sum(qr * kr, axis=-1, keepdims=True)
            nq = jnp.sum(n_old * qr, axis=-1, keepdims=True)
            num = (w_i * qk) * vc + w_f * cq
            den = w_i * qk + w_f * nq
            hcol = num / jnp.maximum(jnp.abs(den), jnp.exp(-m_new))
            ht_ref[hs, j:j + 1] = hcol
            c_out_ref[j, h] = w_f * c_old + (w_i * vc) * kr
            n_out_ref[j, h:h + 1, :] = w_f * n_old + w_i * kr
            m_out_ref[j, :, h:h + 1] = m_new
    for r in range(D // 128):
        rs = slice(r * 128, (r + 1) * 128)
        y8_ref[0, :, rs] = (og8_ref[0, :, rs] * ht_ref[rs, :].T[0:8, :]).astype(y8_ref.dtype)


MLSTM_DEC_SCRATCH = [pltpu.VMEM((D, 128), F32), pltpu.VMEM((D, 128), F32)]


def _mlstm_decode_operands(q, k, v, og, gifc, bif_row, c0, n0, m0, bb):
    n = q.shape[0]
    assert bb <= 8
    r3 = lambda i: (i, 0, 0)
    r4 = lambda i: (i, 0, 0, 0)
    q3, k3 = (a.reshape(n, 1, D) for a in (q, k))
    v8, og8 = (jnp.pad(a.reshape(n // bb, bb, D), ((0, 0), (0, 8 - bb), (0, 0))) for a in (v, og))
    args = (q3, k3, v8, og8, gifc.reshape(n, 1, 2 * HEADS), bif_row, c0, n0,
            m0.reshape(n, 1, HEADS))
    in_specs = [pl.BlockSpec((bb, 1, D), r3), pl.BlockSpec((bb, 1, D), r3),
                pl.BlockSpec((1, 8, D), r3), pl.BlockSpec((1, 8, D), r3),
                pl.BlockSpec((bb, 1, 2 * HEADS), r3), _const_spec((1, 2 * HEADS)),
                pl.BlockSpec((bb, HEADS, DH, DH), r4), pl.BlockSpec((bb, HEADS, DH), r3),
                pl.BlockSpec((bb, 1, HEADS), r3)]
    out_shape = (jax.ShapeDtypeStruct((n // bb, 8, D), BF16),
                 jax.ShapeDtypeStruct((n, HEADS, DH, DH), F32),
                 jax.ShapeDtypeStruct((n, HEADS, DH), F32),
                 jax.ShapeDtypeStruct((n, 1, HEADS), F32))
    out_specs = (pl.BlockSpec((1, 8, D), r3), pl.BlockSpec((bb, HEADS, DH, DH), r4),
                 pl.BlockSpec((bb, HEADS, DH), r3), pl.BlockSpec((bb, 1, HEADS), r3))
    return args, in_specs, out_shape, out_specs


def _mlstm_decode_results(y8, c_new, n_new, m_new):
    n = c_new.shape[0]
    bb = n // y8.shape[0]
    return y8[:, :bb].reshape(n, D), c_new, n_new, m_new.reshape(n, HEADS)


def _xattn_prompt_kernel(xq_ref, mk_ref, mv_ref, y_ref):
    scale = XDH ** -0.5
    for h in range(XH):
        hs = slice(h * XDH, (h + 1) * XDH)
        s = _dot_nt(xq_ref[:, hs], mk_ref[:, hs].astype(BF16)) * scale
        p = jnp.exp(s - jnp.max(s, axis=-1, keepdims=True))
        l = jnp.sum(p, axis=-1, keepdims=True)
        o = _dot(p.astype(BF16), mv_ref[:, hs].astype(BF16)) / l
        y_ref[:, hs] = o.astype(y_ref.dtype)


def _xattn_prompt(xq, mk, mv, batch, t_len):
    tq = TM_PROMPT
    nq = t_len // tq
    return pl.pallas_call(
        _xattn_prompt_kernel,
        out_shape=jax.ShapeDtypeStruct((batch * t_len, XW), BF16),
        grid=(batch, nq),
        in_specs=[pl.BlockSpec((tq, XW), lambda b, i: (b * nq + i, 0)),
                  pl.BlockSpec((N_MEM, XW), lambda b, i: (b, 0)),
                  pl.BlockSpec((N_MEM, XW), lambda b, i: (b, 0))],
        out_specs=pl.BlockSpec((tq, XW), lambda b, i: (b * nq + i, 0)),
        compiler_params=_params(("parallel", "arbitrary")),
        name="xattn_prompt",
    )(xq, mk, mv)


def _xattn_decode_kernel(q_ref, mk_ref, mv_ref, y_ref):
    scale = XDH ** -0.5
    q = q_ref[...][:, None, :, :]
    s = jnp.sum(mk_ref[...] * q, axis=-1, keepdims=True) * scale
    mx = jnp.max(s, axis=1, keepdims=True)
    mx = jnp.maximum(mx, pltpu.roll(mx, XH, axis=2))
    p = jnp.exp(s - mx)
    l = jnp.sum(p, axis=1, keepdims=True)
    l = l + pltpu.roll(l, XH, axis=2)
    o = jnp.sum(p * mv_ref[...], axis=1, keepdims=True)
    o = (o + pltpu.roll(o, XH, axis=2)) / l
    y_ref[...] = o[:, 0, 0:XH, :].astype(y_ref.dtype)


def _xattn_decode_operands(xq, mk, mv, bb):
    n = xq.shape[0]
    xq4 = xq.reshape(n, XH, XDH)
    q2 = jnp.concatenate([xq4, xq4], axis=1)
    mk2, mv2 = (a.reshape(n, N_MEM // 2, 2 * XH, XDH) for a in (mk, mv))
    kv_spec = pl.BlockSpec((bb, N_MEM // 2, 2 * XH, XDH), lambda i: (i, 0, 0, 0))
    in_specs = [pl.BlockSpec((bb, 2 * XH, XDH), lambda i: (i, 0, 0)), kv_spec, kv_spec]
    out_shape = jax.ShapeDtypeStruct((n, XH, XDH), BF16)
    out_spec = pl.BlockSpec((bb, XH, XDH), lambda i: (i, 0, 0))
    return (q2, mk2, mv2), in_specs, out_shape, out_spec


def _merge_kernel(x_ref, yp_ref, ym_ref, yx_ref, gpre_ref, gpost_ref, wg_ref, wbp_ref, wbm_ref,
                  wbx_ref, wo_ref, o_ref, merged_ref):
    x = x_ref[...]
    h = _rms(x, gpre_ref[...]).astype(BF16)
    yp = yp_ref[...]
    ym = ym_ref[...]
    yx = yx_ref[...]
    nchunk = 256
    for c in range(D // nchunk):
        cs = slice(c * nchunk, (c + 1) * nchunk)
        acc = _sigmoid(_dot_nt(h, wg_ref[c * nchunk:(c + 1) * nchunk, :])) * _dot(yp, wbp_ref[:, cs])
        acc += (_sigmoid(_dot_nt(h, wg_ref[D + c * nchunk:D + (c + 1) * nchunk, :]))
                * _dot(ym, wbm_ref[:, cs]))
        acc += (_sigmoid(_dot_nt(h, wg_ref[2 * D + c * nchunk:2 * D + (c + 1) * nchunk, :]))
                * _dot(yx, wbx_ref[:, cs]))
        merged_ref[:, cs] = acc.astype(BF16)
    o_ref[...] = x + _rms(_dot(merged_ref[...], wo_ref[...]), gpost_ref[...])


def _merge(x2, yp, ym, yx, gpre, gpost, wg, wbp, wbm, wbx, wo, tm, row0=0, side=None):
    n = ym.shape[0]
    steps = n // tm
    blk0 = row0 // tm
    assert row0 % tm == 0
    row = lambda i: (i, 0)
    off = lambda i: (blk0 + i, 0)
    args = (x2, yp, ym, yx, gpre, gpost, wg, wbp, wbm, wbx, wo)
    in_specs = [pl.BlockSpec((tm, D), off), pl.BlockSpec((tm, POOL_W), off),
                pl.BlockSpec((tm, D), row), pl.BlockSpec((tm, XW), off),
                _const_spec((1, D)), _const_spec((1, D)), _const_spec((3 * D, D)),
                _const_spec((POOL_W, D)), _const_spec((D, D)), _const_spec((XW, D)),
                _const_spec((D, D))]
    out_shape = jax.ShapeDtypeStruct((n, D), F32)
    out_specs = pl.BlockSpec((tm, D), row)
    body = _merge_kernel
    scratch = [pltpu.VMEM((tm, D), BF16)]
    if side is not None:
        s_body, s_args, s_in, s_shape, s_out, s_scratch = side
        body = functools.partial(_with_side_job, _merge_kernel, (len(args), 1, 1),
                                 s_body, (len(s_args), len(s_shape), len(s_scratch)))
        args, in_specs = args + tuple(s_args), in_specs + list(s_in)
        out_shape, out_specs = (out_shape,) + tuple(s_shape), (out_specs,) + tuple(s_out)
        scratch = scratch + list(s_scratch)
    return pl.pallas_call(
        body,
        out_shape=out_shape,
        grid=(steps,),
        in_specs=in_specs,
        out_specs=out_specs,
        scratch_shapes=scratch,
        compiler_params=_params(("parallel",)),
        name="merge_out",
    )(*args)


def _ffn_kernel(x_ref, gpre_ref, gpost_ref, w1_ref, w2_ref, o_ref):
    x = x_ref[...]
    h = _rms(x, gpre_ref[...]).astype(BF16)
    fchunk = 1024
    acc = None
    for c in range(D_FF // fchunk):
        a = _dot(h, w1_ref[:, c * fchunk:(c + 1) * fchunk])
        a = jnp.square(jnp.maximum(a, 0.0)).astype(BF16)
        part = _dot(a, w2_ref[c * fchunk:(c + 1) * fchunk, :])
        acc = part if acc is None else acc + part
    o_ref[...] = x + _rms(acc, gpost_ref[...])


def _ffn(x2, gpre, gpost, w1, w2, tm, side=None):
    n = x2.shape[0]
    row = lambda i: (i, 0)
    body, args, in_specs, out_shape, out_specs, scratch = _attach_side(
        _ffn_kernel, (x2, gpre, gpost, w1, w2),
        [pl.BlockSpec((tm, D), row), _const_spec((1, D)), _const_spec((1, D)),
         _const_spec((D, D_FF)), _const_spec((D_FF, D))],
        (jax.ShapeDtypeStruct((n, D), F32),), (pl.BlockSpec((tm, D), row),), [], side)
    out = pl.pallas_call(
        body,
        out_shape=out_shape,
        grid=(n // tm,),
        in_specs=in_specs,
        out_specs=out_specs,
        scratch_shapes=scratch,
        compiler_params=_params(("parallel",)),
        name="ffn",
    )(*args)
    return out[0] if side is None else out


def kernel(x_prompt, x_sample, mem_prompt, state_pool_buf, state_mlstm_C, state_mlstm_n, state_mlstm_m, cache_mem_k, cache_mem_v, g_pre_mix, w_in, b_if, w_pool, pool_scale, g_mem, w_mem_kv, w_br_pool, w_br_mlstm, w_br_xattn, w_out, g_post_mix, g_pre_mlp, w_ff1, w_ff2, g_post_mlp):
    batch, t_len, _ = x_prompt.shape
    n_dec = x_sample.shape[0]
    assert w_in.shape[0] == 1, "single layer"

    w_pool_b = w_pool[0].astype(BF16)
    w_kv_b = w_mem_kv[0].astype(BF16)
    g_mix, g_pm, g_mlp, g_pmlp, g_m = (g[0].reshape(1, D) for g in
                                       (g_pre_mix, g_post_mix, g_pre_mlp, g_post_mlp, g_mem))
    scale = pool_scale[0].reshape(1, POOL_W)
    bif = b_if[0]

    w_in_tf = jnp.transpose(w_in[0])
    mk_p, mv_p, mk4_p, mv4_p, w_in_t = _mem_kv(mem_prompt.reshape(batch * N_MEM, D), g_m, w_kv_b,
                                               w_in_tf, O_GATE)
    w_gif_t = jnp.pad(w_in_t[O_GIF:O_XQ], ((0, 128 - 2 * HEADS), (0, 0)))
    w_xq_t = w_in_t[O_XQ:O_GATE]
    w_in_parts = (w_in_t, w_xq_t, w_gif_t)

    xs = x_sample.reshape(n_dec, D)
    u_s, q_s, k_s, v_s, og_s, xq_s, _, gifc_s = _in_proj(xs, g_mix, *w_in_parts, n_dec, F32)
    buf_t = jnp.transpose(state_pool_buf[0], (1, 0, 2))
    ypool_s, buf_s_t = _pool_decode(u_s, buf_t, w_pool_b, scale)
    dec_xattn = (xq_s, cache_mem_k[0], cache_mem_v[0])
    dec_mlstm = (q_s, k_s, v_s, og_s, gifc_s, bif.reshape(1, 2 * HEADS),
                 state_mlstm_C[0], state_mlstm_n[0], state_mlstm_m[0])

    xp = x_prompt.reshape(batch * t_len, D)
    n_tok = batch * t_len
    whole = lambda w: (w[0], 0, w.shape[1])
    conv_merge = _convert_side([whole(w_out), whole(w_br_pool), whole(w_br_mlstm), whole(w_br_xattn),
                                (w_in_tf, O_GATE, 3 * D)], n_tok // TM_INPROJ)
    u, q, k, v, og, xq, gift, _, wo, wbp, wbm, wbx, w_gate = _in_proj(
        xp, g_mix, *w_in_parts, TM_INPROJ, BF16, side=conv_merge)
    y_pool, buf_p = _pool_prompt(u, w_pool_b, scale, batch, t_len)
    rowp, colp, m_last = _gates(gift, bif.reshape(2 * HEADS, 1), batch, t_len)
    y_x = _xattn_prompt(xq, mk_p, mv_p, batch, t_len)
    merge_w = (g_mix, g_pm, w_gate, wbp, wbm, wbx, wo)
    y_ml, c_p, n_p = _mlstm_prompt(q, k, v, og, rowp, colp, t_len, 0, batch)
    d_args, d_in, d_shape, d_out = _mlstm_decode_operands(*dec_mlstm, n_dec // (n_tok // TM_MERGE))
    side = _join_sides((_mlstm_decode_body, d_args, d_in, d_shape, d_out, MLSTM_DEC_SCRATCH),
                       _convert_side([whole(w_ff1), whole(w_ff2)], n_tok // TM_MERGE))
    x1, *dec_out, w1, w2 = _merge(xp, y_pool, y_ml, y_x, *merge_w, TM_MERGE, 0, side)
    x_args, x_in, x_shape, x_out = _xattn_decode_operands(*dec_xattn, n_dec // (n_tok // TM_FFN))
    yp, yx_s = _ffn(x1, g_mlp, g_pmlp, w1, w2, TM_FFN,
                    side=(_xattn_decode_kernel, x_args, x_in, (x_shape,), (x_out,), []))
    yx_s = yx_s.reshape(n_dec, XW)

    yml_s, c_s, n_s, m_s = _mlstm_decode_results(*dec_out)
    x1_s = _merge(xs, ypool_s, yml_s, yx_s, *merge_w, n_dec)
    ys = _ffn(x1_s, g_mlp, g_pmlp, w1, w2, n_dec)

    return (yp.reshape(batch, t_len, D), ys.reshape(n_dec, 1, D),
            buf_p[None], c_p[None], n_p.reshape(1, batch, HEADS, DH),
            m_last.reshape(1, batch, HEADS),
            mk4_p.reshape(1, batch, N_MEM, XH, XDH), mv4_p.reshape(1, batch, N_MEM, XH, XDH),
            jnp.transpose(buf_s_t, (1, 0, 2))[None], c_s[None], n_s[None], m_s[None])
```

```python
import functools

import jax
import jax.numpy as jnp
from jax import lax
from jax.experimental import pallas as pl
from jax.experimental.pallas import tpu as pltpu

F32 = jnp.float32
BF16 = jnp.bfloat16

D = 1024
POOL_W = 512
POOL_G = 128
POOL_WINDOWS = (2, 4, 8, 16)
POOL_BUF = 15
HEADS = 4
DH = 256
XH = 4
XDH = 128
XW = 512
N_MEM = 256
D_FF = 4096
EPS = 1e-6
NEG = -1e30
PAST_LEN = 16384

TM_PROMPT = 1024
TM_INPROJ = 512
TM_MERGE = 512
TM_FFN = 512
CHUNK = 256
MLSTM_GROUP = 4

VMEM_LIMIT = 56 * 1024 * 1024

O_QKVO = POOL_W
O_GIF = O_QKVO + 4 * D
O_XQ = O_GIF + 2 * HEADS
O_GATE = O_XQ + XW


def _params(sem):
    return pltpu.CompilerParams(dimension_semantics=sem, vmem_limit_bytes=VMEM_LIMIT)


def _const_spec(shape):
    nd = len(shape)
    return pl.BlockSpec(shape, lambda *_: (0,) * nd, pipeline_mode=pl.Buffered(1))


def _rms(x, g):
    ms = jnp.mean(x * x, axis=-1, keepdims=True)
    return x * lax.rsqrt(ms + EPS) * g


def _log_sigmoid(x):
    return jnp.minimum(x, 0.0) - jnp.log(1.0 + jnp.exp(-jnp.abs(x)))


def _sigmoid(x):
    return 1.0 / (1.0 + jnp.exp(-x))


def _dot(a, b):
    return jnp.dot(a, b, preferred_element_type=F32)


def _dot_nt(a, b):
    return lax.dot_general(a, b, (((1,), (1,)), ((), ())), preferred_element_type=F32)


def _dot_tn(a, b):
    return lax.dot_general(a, b, (((0,), (0,)), ((), ())), preferred_element_type=F32)


def _with_side_job(host_body, n_host, side_body, n_side, *refs):
    (hi, ho, hs), (si, so, ss) = n_host, n_side
    ins, outs, scr = refs[:hi + si], refs[hi + si:hi + si + ho + so], refs[hi + si + ho + so:]
    assert len(scr) == hs + ss
    side_body(*ins[hi:], *outs[ho:], *scr[hs:])
    host_body(*ins[:hi], *outs[:ho], *scr[:hs])


def _attach_side(body, args, in_specs, out_shape, out_specs, scratch, side):
    if side is None:
        return body, args, in_specs, out_shape, out_specs, scratch
    s_body, s_args, s_in, s_shape, s_out, s_scratch = side
    body = functools.partial(_with_side_job, body, (len(args), len(out_shape), len(scratch)),
                             s_body, (len(s_args), len(s_shape), len(s_scratch)))
    return (body, tuple(args) + tuple(s_args), list(in_specs) + list(s_in),
            tuple(out_shape) + tuple(s_shape), tuple(out_specs) + tuple(s_out),
            list(scratch) + list(s_scratch))


def _join_sides(a, b):
    na, nb = (len(a[1]), len(a[3]), len(a[5])), (len(b[1]), len(b[3]), len(b[5]))

    def body(*refs):
        ins, outs = refs[:na[0] + nb[0]], refs[na[0] + nb[0]:na[0] + nb[0] + na[1] + nb[1]]
        scr = refs[na[0] + nb[0] + na[1] + nb[1]:]
        a[0](*ins[:na[0]], *outs[:na[1]], *scr[:na[2]])
        b[0](*ins[na[0]:], *outs[na[1]:], *scr[na[2]:])

    return (body, tuple(a[1]) + tuple(b[1]), list(a[2]) + list(b[2]), tuple(a[3]) + tuple(b[3]),
            tuple(a[4]) + tuple(b[4]), list(a[5]) + list(b[5]))


def _convert_body(*refs):
    half = len(refs) // 2
    for src, dst in zip(refs[:half], refs[half:]):
        dst[...] = src[...].astype(dst.dtype)


def _convert_side(weights, steps):
    row = lambda i: (i, 0)
    args, in_specs, out_shape, out_specs = (), [], (), ()
    for w, row0, nrows in weights:
        per = nrows // steps
        spec = pl.BlockSpec((per, w.shape[1]), row)
        src = spec if row0 == 0 and nrows == w.shape[0] else pl.BlockSpec(
            (pl.Element(per), pl.Element(w.shape[1])),
            lambda i, r=row0, p=per: (pl.multiple_of(r + i * p, 8), 0))
        args, in_specs = args + (w,), in_specs + [src]
        out_shape += (jax.ShapeDtypeStruct((nrows, w.shape[1]), BF16),)
        out_specs += (spec,)
    return _convert_body, args, in_specs, out_shape, out_specs, []


def _in_proj_kernel(x_ref, g_ref, w_ref, wxq_ref, wgif_ref, u_ref, q_ref, k_ref, v_ref, og_ref,
                    xq_ref, gift_ref, gifc_ref):
    h = _rms(x_ref[...], g_ref[...]).astype(BF16)

    def seg(lo, n):
        return _dot_nt(h, w_ref[lo:lo + n, :])

    u_ref[...] = seg(0, POOL_W)
    q_ref[...] = seg(POOL_W, D).astype(q_ref.dtype)
    k_ref[...] = (seg(POOL_W + D, D) * (DH ** -0.5)).astype(k_ref.dtype)
    v_ref[...] = seg(POOL_W + 2 * D, D).astype(v_ref.dtype)
    og_ref[...] = _sigmoid(seg(POOL_W + 3 * D, D))
    xq_ref[...] = _dot_nt(h, wxq_ref[...]).astype(xq_ref.dtype)
    gif = _dot_nt(h, wgif_ref[...])
    gifc_ref[...] = gif[:, 0:2 * HEADS]
    for r in range(gif.shape[0] // 128):
        gift_ref[:, r * 128:(r + 1) * 128] = gif[r * 128:(r + 1) * 128, :].T[0:2 * HEADS, :]


def _in_proj(x2, g, w_main_t, w_xq_t, w_gif_t, tm, qkv_dtype, side=None):
    n = x2.shape[0]
    steps = n // tm
    row = lambda i: (i, 0)
    out_shape = (
        jax.ShapeDtypeStruct((n, POOL_W), F32),
        jax.ShapeDtypeStruct((n, D), qkv_dtype),
        jax.ShapeDtypeStruct((n, D), qkv_dtype),
        jax.ShapeDtypeStruct((n, D), qkv_dtype),
        jax.ShapeDtypeStruct((n, D), F32),
        jax.ShapeDtypeStruct((n, XW), qkv_dtype),
        jax.ShapeDtypeStruct((2 * HEADS, n), F32),
        jax.ShapeDtypeStruct((n, 2 * HEADS), F32),
    )
    args = (x2, g, w_main_t, w_xq_t, w_gif_t)
    in_specs = [
        pl.BlockSpec((tm, D), row),
        _const_spec((1, D)),
        _const_spec((O_GIF, D)),
        _const_spec(w_xq_t.shape),
        _const_spec(w_gif_t.shape),
    ]
    out_specs = (
        pl.BlockSpec((tm, POOL_W), row),
        pl.BlockSpec((tm, D), row),
        pl.BlockSpec((tm, D), row),
        pl.BlockSpec((tm, D), row),
        pl.BlockSpec((tm, D), row),
        pl.BlockSpec((tm, XW), row),
        pl.BlockSpec((2 * HEADS, tm), lambda i: (0, i)),
        pl.BlockSpec((tm, 2 * HEADS), row),
    )
    body, args, in_specs, out_shape, out_specs, scratch = _attach_side(
        _in_proj_kernel, args, in_specs, out_shape, out_specs, [], side)
    return pl.pallas_call(
        body,
        out_shape=out_shape,
        grid=(steps,),
        in_specs=in_specs,
        out_specs=out_specs,
        scratch_shapes=scratch,
        compiler_params=_params(("parallel",)),
        name="in_proj",
    )(*args)


MEM_KV_STEPS = 16


def _mem_kv_kernel(mem_ref, g_ref, w_ref, wt_ref, k_ref, v_ref, k4_ref, v4_ref, wtb_ref):
    wtb_ref[...] = wt_ref[...].astype(BF16)
    h = _rms(mem_ref[...], g_ref[...]).astype(BF16)
    kv = _dot(h, w_ref[...])
    k_ref[...] = kv[:, :XW]
    v_ref[...] = kv[:, XW:]
    for hd in range(XH):
        k4_ref[:, hd, :] = kv[:, hd * XDH:(hd + 1) * XDH]
        v4_ref[:, hd, :] = kv[:, XW + hd * XDH:XW + (hd + 1) * XDH]


def _mem_kv(mem2, g, w, w_in_t, n_rows):
    n = mem2.shape[0]
    tm = n // MEM_KV_STEPS
    wr = -(-n_rows // (MEM_KV_STEPS * 16)) * 16
    row = lambda i: (i, 0)
    row3 = lambda i: (i, 0, 0)
    return pl.pallas_call(
        _mem_kv_kernel,
        out_shape=(jax.ShapeDtypeStruct((n, XW), F32), jax.ShapeDtypeStruct((n, XW), F32),
                   jax.ShapeDtypeStruct((n, XH, XDH), F32), jax.ShapeDtypeStruct((n, XH, XDH), F32),
                   jax.ShapeDtypeStruct((n_rows, D), BF16)),
        grid=(MEM_KV_STEPS,),
        in_specs=[pl.BlockSpec((tm, D), row), _const_spec((1, D)), _const_spec((D, 2 * XW)),
                  pl.BlockSpec((wr, D), row)],
        out_specs=(pl.BlockSpec((tm, XW), row), pl.BlockSpec((tm, XW), row),
                   pl.BlockSpec((tm, XH, XDH), row3), pl.BlockSpec((tm, XH, XDH), row3),
                   pl.BlockSpec((wr, D), row)),
        compiler_params=_params(("parallel",)),
        name="mem_kv",
    )(mem2, g, w, w_in_t)


def _pool_prompt_kernel(u_ref, w_ref, s_ref, y_ref, nb_ref):
    t_len = u_ref.shape[0]
    t_idx = lax.broadcasted_iota(jnp.int32, (t_len, POOL_G), 0)
    for g, win in enumerate(POOL_WINDOWS):
        cols = slice(g * POOL_G, (g + 1) * POOL_G)
        u = u_ref[:, cols]
        acc = u
        span = 1
        while span < win:
            shifted = pltpu.roll(acc, span, axis=0)
            acc = acc + jnp.where(t_idx >= span, shifted, 0.0)
            span *= 2
        cnt = jnp.minimum(t_idx + 1, win).astype(F32)
        d = (acc / cnt - u).astype(BF16)
        y = _dot(d, w_ref[g]) * s_ref[:, cols]
        y_ref[:, cols] = y.astype(y_ref.dtype)
    nb_ref[0] = u_ref[t_len - POOL_BUF:, :]


def _pool_decode_kernel(u_ref, buf_ref, w_ref, s_ref, y_ref, nb_ref):
    u_all = u_ref[...]
    for g, win in enumerate(POOL_WINDOWS):
        cols = slice(g * POOL_G, (g + 1) * POOL_G)
        u = u_all[:, cols]
        acc = u
        for j in range(POOL_BUF - (win - 1), POOL_BUF):
            acc = acc + buf_ref[j, :, cols]
        cnt = float(min(win, PAST_LEN + 1))
        d = (acc / cnt - u).astype(BF16)
        y = _dot(d, w_ref[g]) * s_ref[:, cols]
        y_ref[:, cols] = y.astype(y_ref.dtype)
    for j in range(POOL_BUF - 1):
        nb_ref[j] = buf_ref[j + 1]
    nb_ref[POOL_BUF - 1] = u_all


def _pool_decode(u, buf_t, w_pool, scale):
    n = u.shape[0]
    return pl.pallas_call(
        _pool_decode_kernel,
        out_shape=(jax.ShapeDtypeStruct((n, POOL_W), BF16),
                   jax.ShapeDtypeStruct((POOL_BUF, n, POOL_W), F32)),
        grid=(1,),
        in_specs=[_const_spec((n, POOL_W)), _const_spec((POOL_BUF, n, POOL_W)),
                  _const_spec((len(POOL_WINDOWS), POOL_G, POOL_G)), _const_spec((1, POOL_W))],
        out_specs=(pl.BlockSpec((n, POOL_W), lambda i: (0, 0)),
                   pl.BlockSpec((POOL_BUF, n, POOL_W), lambda i: (0, 0, 0))),
        compiler_params=_params(("arbitrary",)),
        name="pool_decode",
    )(u, buf_t, w_pool, scale)


COL_M, COL_INTER, COL_EINV, COL_END, COL_DECAY = (i * HEADS for i in range(5))


def _scan_lanes(x, op, fill):
    t_len = x.shape[-1]
    lane = lax.broadcasted_iota(jnp.int32, x.shape, 1)
    k = 1
    while k < t_len:
        shifted = pltpu.roll(x, k, axis=1)
        x = op(x, jnp.where(lane >= k, shifted, fill))
        k *= 2
    return x


def _gates_kernel(gift_ref, bif_ref, row_ref, col_ref, mlast_ref, pack_ref):
    t_len = gift_ref.shape[1]
    L = CHUNK
    g = gift_ref[...] + bif_ref[...]
    ig = g[0:HEADS, :]
    lf = _log_sigmoid(g[HEADS:2 * HEADS, :])
    b_cum = _scan_lanes(lf, jnp.add, 0.0)
    a = ig - b_cum
    m_run = jnp.maximum(_scan_lanes(a, jnp.maximum, NEG), 0.0)
    m_tot = b_cum + m_run
    row_ref[0, 0:HEADS, :] = a
    row_ref[0, HEADS:2 * HEADS, :] = m_run
    mlast_ref[0] = m_tot[:, t_len - 1:t_len]
    pack_ref[...] = jnp.zeros(pack_ref.shape, F32)
    pack_ref[COL_M:COL_M + HEADS, :] = m_run
    pack_ref[COL_EINV:COL_EINV + HEADS, :] = jnp.exp(-m_tot)
    for c in range(t_len // L):
        lo, hi = c * L, (c + 1) * L
        m_prev = jnp.zeros((HEADS, 1), F32) if c == 0 else m_run[:, lo - 1:lo]
        m_end = m_run[:, hi - 1:hi]
        pack_ref[COL_INTER:COL_INTER + HEADS, lo:hi] = jnp.exp(m_prev - m_run[:, lo:hi])
        pack_ref[COL_END:COL_END + HEADS, lo:hi] = jnp.exp(a[:, lo:hi] - m_end)
        pack_ref[COL_DECAY:COL_DECAY + HEADS, lo:hi] = jnp.broadcast_to(
            jnp.exp(m_prev - m_end), (HEADS, L))
    for c in range(t_len // 128):
        col_ref[0, c * 128:(c + 1) * 128, :] = pack_ref[:, c * 128:(c + 1) * 128].T


def _mlstm_prompt_body(n_chunks, q_ref, k_ref, v_ref, og_ref, row_ref, col_ref, y_ref, c_ref, n_ref):
    c_idx = pl.program_id(0) % n_chunks
    L = q_ref.shape[1]

    @pl.when(c_idx == 0)
    def _():
        c_ref[...] = jnp.zeros(c_ref.shape, F32)
        n_ref[...] = jnp.zeros(n_ref.shape, F32)

    t_idx = lax.broadcasted_iota(jnp.int32, (L, L), 0)
    s_idx = lax.broadcasted_iota(jnp.int32, (L, L), 1)
    causal = s_idx <= t_idx
    for b in range(q_ref.shape[0]):
        for h in range(HEADS):
            hs = slice(h * DH, (h + 1) * DH)
            qh = q_ref[b, :, hs]
            kh = k_ref[b, :, hs]
            vh = v_ref[b, :, hs]
            a_row = row_ref[b, h:h + 1, :]
            m_col = col_ref[b, :, COL_M + h:COL_M + h + 1]
            w_inter = col_ref[b, :, COL_INTER + h:COL_INTER + h + 1]
            einv = col_ref[b, :, COL_EINV + h:COL_EINV + h + 1]
            w_end = col_ref[b, :, COL_END + h:COL_END + h + 1]
            decay = col_ref[b, 0:1, COL_DECAY + h:COL_DECAY + h + 1]

            dmat = jnp.exp(jnp.where(causal, a_row - m_col, NEG))
            sw = _dot_nt(qh, kh) * dmat
            c_old = c_ref[b, h]
            n_old = n_ref[b, h]
            inter = _dot_nt(qh, c_old.astype(BF16))
            num = _dot(sw.astype(BF16), vh) + w_inter * inter
            nq = jnp.sum(qh.astype(F32) * n_old, axis=-1, keepdims=True)
            den = jnp.sum(sw, axis=-1, keepdims=True) + w_inter * nq
            r = 1.0 / jnp.maximum(jnp.abs(den), einv)
            y_ref[b, :, hs] = (og_ref[b, :, hs] * (num * r)).astype(y_ref.dtype)

            vw = (vh.astype(F32) * w_end).astype(BF16)
            c_ref[b, h] = decay * c_old + _dot_tn(vw, kh)
            n_ref[b, h] = decay * n_old + jnp.sum(kh.astype(F32) * w_end, axis=0, keepdims=True)


def _mlstm_prompt_operands(q, k, v, og, rowp, colp, t_len, b_lo, nb, gb):
    L = CHUNK
    nc = t_len // L
    batch = q.shape[0] // t_len
    g0 = b_lo // gb
    assert b_lo % gb == 0 and nb % gb == 0
    q3, k3, v3, og3 = (a.reshape(batch, t_len, D) for a in (q, k, v, og))
    tok_in = lambda i: (g0 + i // nc, i % nc, 0)
    tok_out = lambda i: (i // nc, i % nc, 0)
    in_specs = [pl.BlockSpec((gb, L, D), tok_in), pl.BlockSpec((gb, L, D), tok_in),
                pl.BlockSpec((gb, L, D), tok_in), pl.BlockSpec((gb, L, D), tok_in),
                pl.BlockSpec((gb, 2 * HEADS, L), lambda i: (g0 + i // nc, 0, i % nc)),
                pl.BlockSpec((gb, L, 128), tok_in)]
    out_shape = (jax.ShapeDtypeStruct((nb, t_len, D), BF16),
                 jax.ShapeDtypeStruct((nb, HEADS, DH, DH), F32),
                 jax.ShapeDtypeStruct((nb, HEADS, 1, DH), F32))
    out_specs = (pl.BlockSpec((gb, L, D), tok_out),
                 pl.BlockSpec((gb, HEADS, DH, DH), lambda i: (i // nc, 0, 0, 0)),
                 pl.BlockSpec((gb, HEADS, 1, DH), lambda i: (i // nc, 0, 0, 0)))
    body = functools.partial(_mlstm_prompt_body, nc)
    return body, (q3, k3, v3, og3, rowp, colp), in_specs, out_shape, out_specs, (nb // gb) * nc


def _mlstm_prompt(q, k, v, og, rowp, colp, t_len, b_lo, nb):
    body, args, in_specs, out_shape, out_specs, steps = _mlstm_prompt_operands(
        q, k, v, og, rowp, colp, t_len, b_lo, nb, MLSTM_GROUP)
    y, c_fin, n_fin = pl.pallas_call(
        body,
        out_shape=out_shape,
        grid=(steps,),
        in_specs=in_specs,
        out_specs=out_specs,
        compiler_params=_params(("arbitrary",)),
        name="mlstm_prompt",
    )(*args)
    return y.reshape(nb * t_len, D), c_fin, n_fin


def _mlstm_decode_body(q_ref, k_ref, v8_ref, og8_ref, gif_ref, bif_ref, c0_ref, n0_ref, m0_ref,
                       y8_ref, c_out_ref, n_out_ref, m_out_ref, vt_ref, ht_ref):
    vrep = jnp.concatenate([v8_ref[0]] * 16, axis=0)
    for r in range(D // 128):
        rs = slice(r * 128, (r + 1) * 128)
        vt_ref[rs, :] = vrep[:, rs].T
    ht_ref[...] = jnp.zeros(ht_ref.shape, F32)
    for j in range(q_ref.shape[0]):
        gi = gif_ref[j] + bif_ref[...]
        m0 = m0_ref[j]
        for h in range(HEADS):
            hs = slice(h * DH, (h + 1) * DH)
            qr = q_ref[j, :, hs]
            kr = k_ref[j, :, hs]
            vc = vt_ref[hs, j:j + 1]
            ig = gi[:, h:h + 1]
            lf = _log_sigmoid(gi[:, HEADS + h:HEADS + h + 1])
            m_old = m0[:, h:h + 1]
            m_new = jnp.maximum(lf + m_old, ig)
            w_i = jnp.exp(ig - m_new)
            w_f = jnp.exp(lf + m_old - m_new)
            c_old = c0_ref[j, h]
            n_old = n0_ref[j, h:h + 1, :]
            cq = jnp.sum(c_old * qr, axis=-1, keepdims=True)
            qk = jnp.sum(qr * kr, axis=-1, keepdims=True)
            nq = jnp.sum(n_old * qr, axis=-1, keepdims=True)
            num = (w_i * qk) * vc + w_f * cq
            den = w_i * qk + w_f * nq
            hcol = num / jnp.maximum(jnp.abs(den), jnp.exp(-m_new))
            ht_ref[hs, j:j + 1] = hcol
            c_out_ref[j, h] = w_f * c_old + (w_i * vc) * kr
            n_out_ref[j, h:h + 1, :] = w_f * n_old + w_i * kr
            m_out_ref[j, :, h:h + 1] = m_new
    for r in range(D // 128):
        rs = slice(r * 128, (r + 1) * 128)
        y8_ref[0, :, rs] = (og8_ref[0, :, rs] * ht_ref[rs, :].T[0:8, :]).astype(y8_ref.dtype)


MLSTM_DEC_SCRATCH = [pltpu.VMEM((D, 128), F32), pltpu.VMEM((D, 128), F32)]


def _mlstm_decode_operands(q, k, v, og, gifc, bif_row, c0, n0, m0, bb):
    n = q.shape[0]
    assert bb <= 8
    r3 = lambda i: (i, 0, 0)
    r4 = lambda i: (i, 0, 0, 0)
    q3, k3 = (a.reshape(n, 1, D) for a in (q, k))
    v8, og8 = (jnp.pad(a.reshape(n // bb, bb, D), ((0, 0), (0, 8 - bb), (0, 0))) for a in (v, og))
    args = (q3, k3, v8, og8, gifc.reshape(n, 1, 2 * HEADS), bif_row, c0, n0,
            m0.reshape(n, 1, HEADS))
    in_specs = [pl.BlockSpec((bb, 1, D), r3), pl.BlockSpec((bb, 1, D), r3),
                pl.BlockSpec((1, 8, D), r3), pl.BlockSpec((1, 8, D), r3),
                pl.BlockSpec((bb, 1, 2 * HEADS), r3), _const_spec((1, 2 * HEADS)),
                pl.BlockSpec((bb, HEADS, DH, DH), r4), pl.BlockSpec((bb, HEADS, DH), r3),
                pl.BlockSpec((bb, 1, HEADS), r3)]
    out_shape = (jax.ShapeDtypeStruct((n // bb, 8, D), BF16),
                 jax.ShapeDtypeStruct((n, HEADS, DH, DH), F32),
                 jax.ShapeDtypeStruct((n, HEADS, DH), F32),
                 jax.ShapeDtypeStruct((n, 1, HEADS), F32))
    out_specs = (pl.BlockSpec((1, 8, D), r3), pl.BlockSpec((bb, HEADS, DH, DH), r4),
                 pl.BlockSpec((bb, HEADS, DH), r3), pl.BlockSpec((bb, 1, HEADS), r3))
    return args, in_specs, out_shape, out_specs


def _mlstm_decode_results(y8, c_new, n_new, m_new):
    n = c_new.shape[0]
    bb = n // y8.shape[0]
    return y8[:, :bb].reshape(n, D), c_new, n_new, m_new.reshape(n, HEADS)


def _xattn_prompt_kernel(xq_ref, mk_ref, mv_ref, y_ref):
    scale = XDH ** -0.5
    for h in range(XH):
        hs = slice(h * XDH, (h + 1) * XDH)
        s = _dot_nt(xq_ref[:, hs], mk_ref[:, hs].astype(BF16)) * scale
        p = jnp.exp(s - jnp.max(s, axis=-1, keepdims=True))
        l = jnp.sum(p, axis=-1, keepdims=True)
        o = _dot(p.astype(BF16), mv_ref[:, hs].astype(BF16)) / l
        y_ref[:, hs] = o.astype(y_ref.dtype)


def _prompt_branches(xq, mk, mv, u, w_pool, scale, gift, bif, batch, t_len):
    tq = TM_PROMPT
    nq = t_len // tq

    def attention(xq_ref, mk_ref, mv_ref, y_ref):
        for r in range(nq):
            rows = slice(r * tq, (r + 1) * tq)
            _xattn_prompt_kernel(xq_ref.at[rows], mk_ref, mv_ref, y_ref.at[rows])

    seq = lambda b: (b, 0)
    seq3 = lambda b: (b, 0, 0)
    pool = (_pool_prompt_kernel, (u, w_pool, scale),
            [pl.BlockSpec((t_len, POOL_W), seq), _const_spec((len(POOL_WINDOWS), POOL_G, POOL_G)),
             _const_spec((1, POOL_W))],
            (jax.ShapeDtypeStruct((batch * t_len, POOL_W), BF16),
             jax.ShapeDtypeStruct((batch, POOL_BUF, POOL_W), F32)),
            (pl.BlockSpec((t_len, POOL_W), seq), pl.BlockSpec((1, POOL_BUF, POOL_W), seq3)), [])
    gates = (_gates_kernel, (gift, bif),
             [pl.BlockSpec((2 * HEADS, t_len), lambda b: (0, b)), _const_spec((2 * HEADS, 1))],
             (jax.ShapeDtypeStruct((batch, 2 * HEADS, t_len), F32),
              jax.ShapeDtypeStruct((batch, t_len, 128), F32),
              jax.ShapeDtypeStruct((batch, HEADS, 1), F32)),
             (pl.BlockSpec((1, 2 * HEADS, t_len), seq3), pl.BlockSpec((1, t_len, 128), seq3),
              pl.BlockSpec((1, HEADS, 1), seq3)),
             [pltpu.VMEM((128, t_len), F32)])
    body, args, in_specs, out_shape, out_specs, scratch = _attach_side(
        attention, (xq, mk, mv),
        [pl.BlockSpec((t_len, XW), seq), pl.BlockSpec((N_MEM, XW), seq),
         pl.BlockSpec((N_MEM, XW), seq)],
        (jax.ShapeDtypeStruct((batch * t_len, XW), BF16),), (pl.BlockSpec((t_len, XW), seq),), [],
        _join_sides(pool, gates))
    return pl.pallas_call(
        body,
        out_shape=out_shape,
        grid=(batch,),
        in_specs=in_specs,
        out_specs=out_specs,
        scratch_shapes=scratch,
        compiler_params=_params(("parallel",)),
        name="prompt_branches",
    )(*args)


def _xattn_decode_kernel(q_ref, mk_ref, mv_ref, y_ref):
    scale = XDH ** -0.5
    q = q_ref[...][:, None, :, :]
    s = jnp.sum(mk_ref[...] * q, axis=-1, keepdims=True) * scale
    mx = jnp.max(s, axis=1, keepdims=True)
    mx = jnp.maximum(mx, pltpu.roll(mx, XH, axis=2))
    p = jnp.exp(s - mx)
    l = jnp.sum(p, axis=1, keepdims=True)
    l = l + pltpu.roll(l, XH, axis=2)
    o = jnp.sum(p * mv_ref[...], axis=1, keepdims=True)
    o = (o + pltpu.roll(o, XH, axis=2)) / l
    y_ref[...] = o[:, 0, 0:XH, :].astype(y_ref.dtype)


def _xattn_decode_operands(xq, mk, mv, bb):
    n = xq.shape[0]
    xq4 = xq.reshape(n, XH, XDH)
    q2 = jnp.concatenate([xq4, xq4], axis=1)
    mk2, mv2 = (a.reshape(n, N_MEM // 2, 2 * XH, XDH) for a in (mk, mv))
    kv_spec = pl.BlockSpec((bb, N_MEM // 2, 2 * XH, XDH), lambda i: (i, 0, 0, 0))
    in_specs = [pl.BlockSpec((bb, 2 * XH, XDH), lambda i: (i, 0, 0)), kv_spec, kv_spec]
    out_shape = jax.ShapeDtypeStruct((n, XH, XDH), BF16)
    out_spec = pl.BlockSpec((bb, XH, XDH), lambda i: (i, 0, 0))
    return (q2, mk2, mv2), in_specs, out_shape, out_spec


def _merge_kernel(x_ref, yp_ref, ym_ref, yx_ref, gpre_ref, gpost_ref, wg_ref, wbp_ref, wbm_ref,
                  wbx_ref, wo_ref, o_ref, merged_ref):
    x = x_ref[...]
    h = _rms(x, gpre_ref[...]).astype(BF16)
    yp = yp_ref[...]
    ym = ym_ref[...]
    yx = yx_ref[...]
    nchunk = 256
    for c in range(D // nchunk):
        cs = slice(c * nchunk, (c + 1) * nchunk)
        acc = _sigmoid(_dot_nt(h, wg_ref[c * nchunk:(c + 1) * nchunk, :])) * _dot(yp, wbp_ref[:, cs])
        acc += (_sigmoid(_dot_nt(h, wg_ref[D + c * nchunk:D + (c + 1) * nchunk, :]))
                * _dot(ym, wbm_ref[:, cs]))
        acc += (_sigmoid(_dot_nt(h, wg_ref[2 * D + c * nchunk:2 * D + (c + 1) * nchunk, :]))
                * _dot(yx, wbx_ref[:, cs]))
        merged_ref[:, cs] = acc.astype(BF16)
    o_ref[...] = x + _rms(_dot(merged_ref[...], wo_ref[...]), gpost_ref[...])


def _merge(x2, yp, ym, yx, gpre, gpost, wg, wbp, wbm, wbx, wo, tm, row0=0, side=None):
    n = ym.shape[0]
    steps = n // tm
    blk0 = row0 // tm
    assert row0 % tm == 0
    row = lambda i: (i, 0)
    off = lambda i: (blk0 + i, 0)
    args = (x2, yp, ym, yx, gpre, gpost, wg, wbp, wbm, wbx, wo)
    in_specs = [pl.BlockSpec((tm, D), off), pl.BlockSpec((tm, POOL_W), off),
                pl.BlockSpec((tm, D), row), pl.BlockSpec((tm, XW), off),
                _const_spec((1, D)), _const_spec((1, D)), _const_spec((3 * D, D)),
                _const_spec((POOL_W, D)), _const_spec((D, D)), _const_spec((XW, D)),
                _const_spec((D, D))]
    out_shape = jax.ShapeDtypeStruct((n, D), F32)
    out_specs = pl.BlockSpec((tm, D), row)
    body = _merge_kernel
    scratch = [pltpu.VMEM((tm, D), BF16)]
    if side is not None:
        s_body, s_args, s_in, s_shape, s_out, s_scratch = side
        body = functools.partial(_with_side_job, _merge_kernel, (len(args), 1, 1),
                                 s_body, (len(s_args), len(s_shape), len(s_scratch)))
        args, in_specs = args + tuple(s_args), in_specs + list(s_in)
        out_shape, out_specs = (out_shape,) + tuple(s_shape), (out_specs,) + tuple(s_out)
        scratch = scratch + list(s_scratch)
    return pl.pallas_call(
        body,
        out_shape=out_shape,
        grid=(steps,),
        in_specs=in_specs,
        out_specs=out_specs,
        scratch_shapes=scratch,
        compiler_params=_params(("parallel",)),
        name="merge_out",
    )(*args)


def _ffn_kernel(x_ref, gpre_ref, gpost_ref, w1_ref, w2_ref, o_ref):
    x = x_ref[...]
    h = _rms(x, gpre_ref[...]).astype(BF16)
    fchunk = 1024
    acc = None
    for c in range(D_FF // fchunk):
        a = _dot(h, w1_ref[:, c * fchunk:(c + 1) * fchunk])
        a = jnp.square(jnp.maximum(a, 0.0)).astype(BF16)
        part = _dot(a, w2_ref[c * fchunk:(c + 1) * fchunk, :])
        acc = part if acc is None else acc + part
    o_ref[...] = x + _rms(acc, gpost_ref[...])


def _ffn(x2, gpre, gpost, w1, w2, tm, side=None):
    n = x2.shape[0]
    row = lambda i: (i, 0)
    body, args, in_specs, out_shape, out_specs, scratch = _attach_side(
        _ffn_kernel, (x2, gpre, gpost, w1, w2),
        [pl.BlockSpec((tm, D), row), _const_spec((1, D)), _const_spec((1, D)),
         _const_spec((D, D_FF)), _const_spec((D_FF, D))],
        (jax.ShapeDtypeStruct((n, D), F32),), (pl.BlockSpec((tm, D), row),), [], side)
    out = pl.pallas_call(
        body,
        out_shape=out_shape,
        grid=(n // tm,),
        in_specs=in_specs,
        out_specs=out_specs,
        scratch_shapes=scratch,
        compiler_params=_params(("parallel",)),
        name="ffn",
    )(*args)
    return out[0] if side is None else out


def kernel(x_prompt, x_sample, mem_prompt, state_pool_buf, state_mlstm_C, state_mlstm_n, state_mlstm_m, cache_mem_k, cache_mem_v, g_pre_mix, w_in, b_if, w_pool, pool_scale, g_mem, w_mem_kv, w_br_pool, w_br_mlstm, w_br_xattn, w_out, g_post_mix, g_pre_mlp, w_ff1, w_ff2, g_post_mlp):
    batch, t_len, _ = x_prompt.shape
    n_dec = x_sample.shape[0]
    assert w_in.shape[0] == 1, "single layer"

    w_pool_b = w_pool[0].astype(BF16)
    w_kv_b = w_mem_kv[0].astype(BF16)
    g_mix, g_pm, g_mlp, g_pmlp, g_m = (g[0].reshape(1, D) for g in
                                       (g_pre_mix, g_post_mix, g_pre_mlp, g_post_mlp, g_mem))
    scale = pool_scale[0].reshape(1, POOL_W)
    bif = b_if[0]

    w_in_tf = jnp.transpose(w_in[0])
    mk_p, mv_p, mk4_p, mv4_p, w_in_t = _mem_kv(mem_prompt.reshape(batch * N_MEM, D), g_m, w_kv_b,
                                               w_in_tf, O_GATE)
    w_gif_t = jnp.pad(w_in_t[O_GIF:O_XQ], ((0, 128 - 2 * HEADS), (0, 0)))
    w_xq_t = w_in_t[O_XQ:O_GATE]
    w_in_parts = (w_in_t, w_xq_t, w_gif_t)

    xs = x_sample.reshape(n_dec, D)
    u_s, q_s, k_s, v_s, og_s, xq_s, _, gifc_s = _in_proj(xs, g_mix, *w_in_parts, n_dec, F32)
    buf_t = jnp.transpose(state_pool_buf[0], (1, 0, 2))
    ypool_s, buf_s_t = _pool_decode(u_s, buf_t, w_pool_b, scale)
    dec_xattn = (xq_s, cache_mem_k[0], cache_mem_v[0])
    dec_mlstm = (q_s, k_s, v_s, og_s, gifc_s, bif.reshape(1, 2 * HEADS),
                 state_mlstm_C[0], state_mlstm_n[0], state_mlstm_m[0])

    xp = x_prompt.reshape(batch * t_len, D)
    n_tok = batch * t_len
    whole = lambda w: (w[0], 0, w.shape[1])
    conv_merge = _convert_side([whole(w_out), whole(w_br_pool), whole(w_br_mlstm), whole(w_br_xattn),
                                (w_in_tf, O_GATE, 3 * D)], n_tok // TM_INPROJ)
    u, q, k, v, og, xq, gift, _, wo, wbp, wbm, wbx, w_gate = _in_proj(
        xp, g_mix, *w_in_parts, TM_INPROJ, BF16, side=conv_merge)
    y_x, y_pool, buf_p, rowp, colp, m_last = _prompt_branches(
        xq, mk_p, mv_p, u, w_pool_b, scale, gift, bif.reshape(2 * HEADS, 1), batch, t_len)
    merge_w = (g_mix, g_pm, w_gate, wbp, wbm, wbx, wo)
    y_ml, c_p, n_p = _mlstm_prompt(q, k, v, og, rowp, colp, t_len, 0, batch)
    d_args, d_in, d_shape, d_out = _mlstm_decode_operands(*dec_mlstm, n_dec // (n_tok // TM_MERGE))
    side = _join_sides((_mlstm_decode_body, d_args, d_in, d_shape, d_out, MLSTM_DEC_SCRATCH),
                       _convert_side([whole(w_ff1), whole(w_ff2)], n_tok // TM_MERGE))
    x1, *dec_out, w1, w2 = _merge(xp, y_pool, y_ml, y_x, *merge_w, TM_MERGE, 0, side)
    x_args, x_in, x_shape, x_out = _xattn_decode_operands(*dec_xattn, n_dec // (n_tok // TM_FFN))
    yp, yx_s = _ffn(x1, g_mlp, g_pmlp, w1, w2, TM_FFN,
                    side=(_xattn_decode_kernel, x_args, x_in, (x_shape,), (x_out,), []))
    yx_s = yx_s.reshape(n_dec, XW)

    yml_s, c_s, n_s, m_s = _mlstm_decode_results(*dec_out)
    x1_s = _merge(xs, ypool_s, yml_s, yx_s, *merge_w, n_dec)
    ys = _ffn(x1_s, g_mlp, g_pmlp, w1, w2, n_dec)

    return (yp.reshape(batch, t_len, D), ys.reshape(n_dec, 1, D),
            buf_p[None], c_p[None], n_p.reshape(1, batch, HEADS, DH),
            m_last.reshape(1, batch, HEADS),
            mk4_p.reshape(1, batch, N_MEM, XH, XDH), mv4_p.reshape(1, batch, N_MEM, XH, XDH),
            jnp.transpose(buf_s_t, (1, 0, 2))[None], c_s[None], n_s[None], m_s[None])
```

```python
import functools

import jax
import jax.numpy as jnp
from jax import lax
from jax.experimental import pallas as pl
from jax.experimental.pallas import tpu as pltpu

F32 = jnp.float32
BF16 = jnp.bfloat16

D = 1024
POOL_W = 512
POOL_G = 128
POOL_WINDOWS = (2, 4, 8, 16)
POOL_BUF = 15
HEADS = 4
DH = 256
XH = 4
XDH = 128
XW = 512
N_MEM = 256
D_FF = 4096
EPS = 1e-6
NEG = -1e30
PAST_LEN = 16384

TM_PROMPT = 1024
TM_INPROJ = 512
TM_MERGE = 512
TM_FFN = 512
CHUNK = 256
MLSTM_GROUP = 4

VMEM_LIMIT = 56 * 1024 * 1024

O_QKVO = POOL_W
O_GIF = O_QKVO + 4 * D
O_XQ = O_GIF + 2 * HEADS
O_GATE = O_XQ + XW


def _params(sem):
    return pltpu.CompilerParams(dimension_semantics=sem, vmem_limit_bytes=VMEM_LIMIT)


def _const_spec(shape):
    nd = len(shape)
    return pl.BlockSpec(shape, lambda *_: (0,) * nd, pipeline_mode=pl.Buffered(1))


def _rms(x, g):
    ms = jnp.mean(x * x, axis=-1, keepdims=True)
    return x * lax.rsqrt(ms + EPS) * g


def _log_sigmoid(x):
    return jnp.minimum(x, 0.0) - jnp.log(1.0 + jnp.exp(-jnp.abs(x)))


def _sigmoid(x):
    return 1.0 / (1.0 + jnp.exp(-x))


def _dot(a, b):
    return jnp.dot(a, b, preferred_element_type=F32)


def _dot_nt(a, b):
    return lax.dot_general(a, b, (((1,), (1,)), ((), ())), preferred_element_type=F32)


def _dot_tn(a, b):
    return lax.dot_general(a, b, (((0,), (0,)), ((), ())), preferred_element_type=F32)


def _with_side_job(host_body, n_host, side_body, n_side, *refs):
    (hi, ho, hs), (si, so, ss) = n_host, n_side
    ins, outs, scr = refs[:hi + si], refs[hi + si:hi + si + ho + so], refs[hi + si + ho + so:]
    assert len(scr) == hs + ss
    side_body(*ins[hi:], *outs[ho:], *scr[hs:])
    host_body(*ins[:hi], *outs[:ho], *scr[:hs])


def _attach_side(body, args, in_specs, out_shape, out_specs, scratch, side):
    if side is None:
        return body, args, in_specs, out_shape, out_specs, scratch
    s_body, s_args, s_in, s_shape, s_out, s_scratch = side
    body = functools.partial(_with_side_job, body, (len(args), len(out_shape), len(scratch)),
                             s_body, (len(s_args), len(s_shape), len(s_scratch)))
    return (body, tuple(args) + tuple(s_args), list(in_specs) + list(s_in),
            tuple(out_shape) + tuple(s_shape), tuple(out_specs) + tuple(s_out),
            list(scratch) + list(s_scratch))


def _join_sides(a, b):
    na, nb = (len(a[1]), len(a[3]), len(a[5])), (len(b[1]), len(b[3]), len(b[5]))

    def body(*refs):
        ins, outs = refs[:na[0] + nb[0]], refs[na[0] + nb[0]:na[0] + nb[0] + na[1] + nb[1]]
        scr = refs[na[0] + nb[0] + na[1] + nb[1]:]
        a[0](*ins[:na[0]], *outs[:na[1]], *scr[:na[2]])
        b[0](*ins[na[0]:], *outs[na[1]:], *scr[na[2]:])

    return (body, tuple(a[1]) + tuple(b[1]), list(a[2]) + list(b[2]), tuple(a[3]) + tuple(b[3]),
            tuple(a[4]) + tuple(b[4]), list(a[5]) + list(b[5]))


def _convert_body(*refs):
    half = len(refs) // 2
    for src, dst in zip(refs[:half], refs[half:]):
        dst[...] = src[...].astype(dst.dtype)


def _convert_side(weights, steps):
    row = lambda i: (i, 0)
    args, in_specs, out_shape, out_specs = (), [], (), ()
    for w, row0, nrows in weights:
        per = nrows // steps
        spec = pl.BlockSpec((per, w.shape[1]), row)
        src = spec if row0 == 0 and nrows == w.shape[0] else pl.BlockSpec(
            (pl.Element(per), pl.Element(w.shape[1])),
            lambda i, r=row0, p=per: (pl.multiple_of(r + i * p, 8), 0))
        args, in_specs = args + (w,), in_specs + [src]
        out_shape += (jax.ShapeDtypeStruct((nrows, w.shape[1]), BF16),)
        out_specs += (spec,)
    return _convert_body, args, in_specs, out_shape, out_specs, []


def _in_proj_kernel(x_ref, g_ref, w_ref, wxq_ref, wgif_ref, u_ref, q_ref, k_ref, v_ref, og_ref,
                    xq_ref, gift_ref, gifc_ref):
    h = _rms(x_ref[...], g_ref[...]).astype(BF16)

    def seg(lo, n):
        return _dot_nt(h, w_ref[lo:lo + n, :])

    u_ref[...] = seg(0, POOL_W)
    q_ref[...] = seg(POOL_W, D).astype(q_ref.dtype)
    k_ref[...] = (seg(POOL_W + D, D) * (DH ** -0.5)).astype(k_ref.dtype)
    v_ref[...] = seg(POOL_W + 2 * D, D).astype(v_ref.dtype)
    og_ref[...] = _sigmoid(seg(POOL_W + 3 * D, D))
    xq_ref[...] = _dot_nt(h, wxq_ref[...]).astype(xq_ref.dtype)
    gif = _dot_nt(h, wgif_ref[...])
    gifc_ref[...] = gif[:, 0:2 * HEADS]
    for r in range(gif.shape[0] // 128):
        gift_ref[:, r * 128:(r + 1) * 128] = gif[r * 128:(r + 1) * 128, :].T[0:2 * HEADS, :]


def _in_proj(x2, g, w_main_t, w_xq_t, w_gif_t, tm, qkv_dtype, side=None):
    n = x2.shape[0]
    steps = n // tm
    row = lambda i: (i, 0)
    out_shape = (
        jax.ShapeDtypeStruct((n, POOL_W), F32),
        jax.ShapeDtypeStruct((n, D), qkv_dtype),
        jax.ShapeDtypeStruct((n, D), qkv_dtype),
        jax.ShapeDtypeStruct((n, D), qkv_dtype),
        jax.ShapeDtypeStruct((n, D), F32),
        jax.ShapeDtypeStruct((n, XW), qkv_dtype),
        jax.ShapeDtypeStruct((2 * HEADS, n), F32),
        jax.ShapeDtypeStruct((n, 2 * HEADS), F32),
    )
    args = (x2, g, w_main_t, w_xq_t, w_gif_t)
    in_specs = [
        pl.BlockSpec((tm, D), row),
        _const_spec((1, D)),
        _const_spec((O_GIF, D)),
        _const_spec(w_xq_t.shape),
        _const_spec(w_gif_t.shape),
    ]
    out_specs = (
        pl.BlockSpec((tm, POOL_W), row),
        pl.BlockSpec((tm, D), row),
        pl.BlockSpec((tm, D), row),
        pl.BlockSpec((tm, D), row),
        pl.BlockSpec((tm, D), row),
        pl.BlockSpec((tm, XW), row),
        pl.BlockSpec((2 * HEADS, tm), lambda i: (0, i)),
        pl.BlockSpec((tm, 2 * HEADS), row),
    )
    body, args, in_specs, out_shape, out_specs, scratch = _attach_side(
        _in_proj_kernel, args, in_specs, out_shape, out_specs, [], side)
    return pl.pallas_call(
        body,
        out_shape=out_shape,
        grid=(steps,),
        in_specs=in_specs,
        out_specs=out_specs,
        scratch_shapes=scratch,
        compiler_params=_params(("parallel",)),
        name="in_proj",
    )(*args)


MEM_KV_STEPS = 8


def _mem_kv_kernel(mem_ref, g_ref, w_ref, wt_ref, k_ref, v_ref, k4_ref, v4_ref, wtb_ref):
    wtb_ref[...] = wt_ref[...].astype(BF16)
    h = _rms(mem_ref[...], g_ref[...]).astype(BF16)
    kv = _dot(h, w_ref[...])
    k_ref[...] = kv[:, :XW]
    v_ref[...] = kv[:, XW:]
    for hd in range(XH):
        k4_ref[:, hd, :] = kv[:, hd * XDH:(hd + 1) * XDH]
        v4_ref[:, hd, :] = kv[:, XW + hd * XDH:XW + (hd + 1) * XDH]


def _mem_kv(mem2, g, w, w_in_t, n_rows):
    n = mem2.shape[0]
    tm = n // MEM_KV_STEPS
    wr = -(-n_rows // (MEM_KV_STEPS * 16)) * 16
    row = lambda i: (i, 0)
    row3 = lambda i: (i, 0, 0)
    return pl.pallas_call(
        _mem_kv_kernel,
        out_shape=(jax.ShapeDtypeStruct((n, XW), F32), jax.ShapeDtypeStruct((n, XW), F32),
                   jax.ShapeDtypeStruct((n, XH, XDH), F32), jax.ShapeDtypeStruct((n, XH, XDH), F32),
                   jax.ShapeDtypeStruct((n_rows, D), BF16)),
        grid=(MEM_KV_STEPS,),
        in_specs=[pl.BlockSpec((tm, D), row), _const_spec((1, D)), _const_spec((D, 2 * XW)),
                  pl.BlockSpec((wr, D), row)],
        out_specs=(pl.BlockSpec((tm, XW), row), pl.BlockSpec((tm, XW), row),
                   pl.BlockSpec((tm, XH, XDH), row3), pl.BlockSpec((tm, XH, XDH), row3),
                   pl.BlockSpec((wr, D), row)),
        compiler_params=_params(("parallel",)),
        name="mem_kv",
    )(mem2, g, w, w_in_t)


def _pool_prompt_kernel(u_ref, w_ref, s_ref, y_ref, nb_ref):
    t_len = u_ref.shape[0]
    t_idx = lax.broadcasted_iota(jnp.int32, (t_len, POOL_G), 0)
    for g, win in enumerate(POOL_WINDOWS):
        cols = slice(g * POOL_G, (g + 1) * POOL_G)
        u = u_ref[:, cols]
        acc = u
        span = 1
        while span < win:
            shifted = pltpu.roll(acc, span, axis=0)
            acc = acc + jnp.where(t_idx >= span, shifted, 0.0)
            span *= 2
        cnt = jnp.minimum(t_idx + 1, win).astype(F32)
        d = (acc / cnt - u).astype(BF16)
        y = _dot(d, w_ref[g]) * s_ref[:, cols]
        y_ref[:, cols] = y.astype(y_ref.dtype)
    nb_ref[0] = u_ref[t_len - POOL_BUF:, :]


def _pool_decode_kernel(u_ref, buf_ref, w_ref, s_ref, y_ref, nb_ref):
    u_all = u_ref[...]
    for g, win in enumerate(POOL_WINDOWS):
        cols = slice(g * POOL_G, (g + 1) * POOL_G)
        u = u_all[:, cols]
        acc = u
        for j in range(POOL_BUF - (win - 1), POOL_BUF):
            acc = acc + buf_ref[j, :, cols]
        cnt = float(min(win, PAST_LEN + 1))
        d = (acc / cnt - u).astype(BF16)
        y = _dot(d, w_ref[g]) * s_ref[:, cols]
        y_ref[:, cols] = y.astype(y_ref.dtype)
    for j in range(POOL_BUF - 1):
        nb_ref[j] = buf_ref[j + 1]
    nb_ref[POOL_BUF - 1] = u_all


def _pool_decode(u, buf_t, w_pool, scale):
    n = u.shape[0]
    return pl.pallas_call(
        _pool_decode_kernel,
        out_shape=(jax.ShapeDtypeStruct((n, POOL_W), BF16),
                   jax.ShapeDtypeStruct((POOL_BUF, n, POOL_W), F32)),
        grid=(1,),
        in_specs=[_const_spec((n, POOL_W)), _const_spec((POOL_BUF, n, POOL_W)),
                  _const_spec((len(POOL_WINDOWS), POOL_G, POOL_G)), _const_spec((1, POOL_W))],
        out_specs=(pl.BlockSpec((n, POOL_W), lambda i: (0, 0)),
                   pl.BlockSpec((POOL_BUF, n, POOL_W), lambda i: (0, 0, 0))),
        compiler_params=_params(("arbitrary",)),
        name="pool_decode",
    )(u, buf_t, w_pool, scale)


COL_M, COL_INTER, COL_EINV, COL_END, COL_DECAY = (i * HEADS for i in range(5))


def _scan_lanes(x, op, fill):
    t_len = x.shape[-1]
    lane = lax.broadcasted_iota(jnp.int32, x.shape, 1)
    k = 1
    while k < t_len:
        shifted = pltpu.roll(x, k, axis=1)
        x = op(x, jnp.where(lane >= k, shifted, fill))
        k *= 2
    return x


def _gates_kernel(gift_ref, bif_ref, row_ref, col_ref, mlast_ref, pack_ref):
    t_len = gift_ref.shape[1]
    L = CHUNK
    g = gift_ref[...] + bif_ref[...]
    ig = g[0:HEADS, :]
    lf = _log_sigmoid(g[HEADS:2 * HEADS, :])
    b_cum = _scan_lanes(lf, jnp.add, 0.0)
    a = ig - b_cum
    m_run = jnp.maximum(_scan_lanes(a, jnp.maximum, NEG), 0.0)
    m_tot = b_cum + m_run
    row_ref[0, 0:HEADS, :] = a
    row_ref[0, HEADS:2 * HEADS, :] = m_run
    mlast_ref[0] = m_tot[:, t_len - 1:t_len]
    pack_ref[...] = jnp.zeros(pack_ref.shape, F32)
    pack_ref[COL_M:COL_M + HEADS, :] = m_run
    pack_ref[COL_EINV:COL_EINV + HEADS, :] = jnp.exp(-m_tot)
    for c in range(t_len // L):
        lo, hi = c * L, (c + 1) * L
        m_prev = jnp.zeros((HEADS, 1), F32) if c == 0 else m_run[:, lo - 1:lo]
        m_end = m_run[:, hi - 1:hi]
        pack_ref[COL_INTER:COL_INTER + HEADS, lo:hi] = jnp.exp(m_prev - m_run[:, lo:hi])
        pack_ref[COL_END:COL_END + HEADS, lo:hi] = jnp.exp(a[:, lo:hi] - m_end)
        pack_ref[COL_DECAY:COL_DECAY + HEADS, lo:hi] = jnp.broadcast_to(
            jnp.exp(m_prev - m_end), (HEADS, L))
    for c in range(t_len // 128):
        col_ref[0, c * 128:(c + 1) * 128, :] = pack_ref[:, c * 128:(c + 1) * 128].T


def _mlstm_prompt_body(n_chunks, q_ref, k_ref, v_ref, og_ref, row_ref, col_ref, y_ref, c_ref, n_ref):
    c_idx = pl.program_id(0) % n_chunks
    L = q_ref.shape[1]

    @pl.when(c_idx == 0)
    def _():
        c_ref[...] = jnp.zeros(c_ref.shape, F32)
        n_ref[...] = jnp.zeros(n_ref.shape, F32)

    t_idx = lax.broadcasted_iota(jnp.int32, (L, L), 0)
    s_idx = lax.broadcasted_iota(jnp.int32, (L, L), 1)
    causal = s_idx <= t_idx
    for b in range(q_ref.shape[0]):
        for h in range(HEADS):
            hs = slice(h * DH, (h + 1) * DH)
            qh = q_ref[b, :, hs]
            kh = k_ref[b, :, hs]
            vh = v_ref[b, :, hs]
            a_row = row_ref[b, h:h + 1, :]
            m_col = col_ref[b, :, COL_M + h:COL_M + h + 1]
            w_inter = col_ref[b, :, COL_INTER + h:COL_INTER + h + 1]
            einv = col_ref[b, :, COL_EINV + h:COL_EINV + h + 1]
            w_end = col_ref[b, :, COL_END + h:COL_END + h + 1]
            decay = col_ref[b, 0:1, COL_DECAY + h:COL_DECAY + h + 1]

            dmat = jnp.exp(jnp.where(causal, a_row - m_col, NEG))
            sw = _dot_nt(qh, kh) * dmat
            c_old = c_ref[b, h]
            n_old = n_ref[b, h]
            inter = _dot_nt(qh, c_old.astype(BF16))
            num = _dot(sw.astype(BF16), vh) + w_inter * inter
            nq = jnp.sum(qh.astype(F32) * n_old, axis=-1, keepdims=True)
            den = jnp.sum(sw, axis=-1, keepdims=True) + w_inter * nq
            r = 1.0 / jnp.maximum(jnp.abs(den), einv)
            y_ref[b, :, hs] = (og_ref[b, :, hs] * (num * r)).astype(y_ref.dtype)

            vw = (vh.astype(F32) * w_end).astype(BF16)
            c_ref[b, h] = decay * c_old + _dot_tn(vw, kh)
            n_ref[b, h] = decay * n_old + jnp.sum(kh.astype(F32) * w_end, axis=0, keepdims=True)


def _mlstm_prompt_operands(q, k, v, og, rowp, colp, t_len, b_lo, nb, gb):
    L = CHUNK
    nc = t_len // L
    batch = q.shape[0] // t_len
    g0 = b_lo // gb
    assert b_lo % gb == 0 and nb % gb == 0
    q3, k3, v3, og3 = (a.reshape(batch, t_len, D) for a in (q, k, v, og))
    tok_in = lambda i: (g0 + i // nc, i % nc, 0)
    tok_out = lambda i: (i // nc, i % nc, 0)
    in_specs = [pl.BlockSpec((gb, L, D), tok_in), pl.BlockSpec((gb, L, D), tok_in),
                pl.BlockSpec((gb, L, D), tok_in), pl.BlockSpec((gb, L, D), tok_in),
                pl.BlockSpec((gb, 2 * HEADS, L), lambda i: (g0 + i // nc, 0, i % nc)),
                pl.BlockSpec((gb, L, 128), tok_in)]
    out_shape = (jax.ShapeDtypeStruct((nb, t_len, D), BF16),
                 jax.ShapeDtypeStruct((nb, HEADS, DH, DH), F32),
                 jax.ShapeDtypeStruct((nb, HEADS, 1, DH), F32))
    out_specs = (pl.BlockSpec((gb, L, D), tok_out),
                 pl.BlockSpec((gb, HEADS, DH, DH), lambda i: (i // nc, 0, 0, 0)),
                 pl.BlockSpec((gb, HEADS, 1, DH), lambda i: (i // nc, 0, 0, 0)))
    body = functools.partial(_mlstm_prompt_body, nc)
    return body, (q3, k3, v3, og3, rowp, colp), in_specs, out_shape, out_specs, (nb // gb) * nc


def _mlstm_prompt(q, k, v, og, rowp, colp, t_len, b_lo, nb):
    body, args, in_specs, out_shape, out_specs, steps = _mlstm_prompt_operands(
        q, k, v, og, rowp, colp, t_len, b_lo, nb, MLSTM_GROUP)
    y, c_fin, n_fin = pl.pallas_call(
        body,
        out_shape=out_shape,
        grid=(steps,),
        in_specs=in_specs,
        out_specs=out_specs,
        compiler_params=_params(("arbitrary",)),
        name="mlstm_prompt",
    )(*args)
    return y.reshape(nb * t_len, D), c_fin, n_fin


def _mlstm_decode_body(q_ref, k_ref, v8_ref, og8_ref, gif_ref, bif_ref, c0_ref, n0_ref, m0_ref,
                       y8_ref, c_out_ref, n_out_ref, m_out_ref, vt_ref, ht_ref):
    vrep = jnp.concatenate([v8_ref[0]] * 16, axis=0)
    for r in range(D // 128):
        rs = slice(r * 128, (r + 1) * 128)
        vt_ref[rs, :] = vrep[:, rs].T
    ht_ref[...] = jnp.zeros(ht_ref.shape, F32)
    for j in range(q_ref.shape[0]):
        gi = gif_ref[j] + bif_ref[...]
        m0 = m0_ref[j]
        for h in range(HEADS):
            hs = slice(h * DH, (h + 1) * DH)
            qr = q_ref[j, :, hs]
            kr = k_ref[j, :, hs]
            vc = vt_ref[hs, j:j + 1]
            ig = gi[:, h:h + 1]
            lf = _log_sigmoid(gi[:, HEADS + h:HEADS + h + 1])
            m_old = m0[:, h:h + 1]
            m_new = jnp.maximum(lf + m_old, ig)
            w_i = jnp.exp(ig - m_new)
            w_f = jnp.exp(lf + m_old - m_new)
            c_old = c0_ref[j, h]
            n_old = n0_ref[j, h:h + 1, :]
            cq = jnp.sum(c_old * qr, axis=-1, keepdims=True)
            qk = jnp.sum(qr * kr, axis=-1, keepdims=True)
            nq = jnp.sum(n_old * qr, axis=-1, keepdims=True)
            num = (w_i * qk) * vc + w_f * cq
            den = w_i * qk + w_f * nq
            hcol = num / jnp.maximum(jnp.abs(den), jnp.exp(-m_new))
            ht_ref[hs, j:j + 1] = hcol
            c_out_ref[j, h] = w_f * c_old + (w_i * vc) * kr
            n_out_ref[j, h:h + 1, :] = w_f * n_old + w_i * kr
            m_out_ref[j, :, h:h + 1] = m_new
    for r in range(D // 128):
        rs = slice(r * 128, (r + 1) * 128)
        y8_ref[0, :, rs] = (og8_ref[0, :, rs] * ht_ref[rs, :].T[0:8, :]).astype(y8_ref.dtype)


MLSTM_DEC_SCRATCH = [pltpu.VMEM((D, 128), F32), pltpu.VMEM((D, 128), F32)]


def _mlstm_decode_operands(q, k, v, og, gifc, bif_row, c0, n0, m0, bb):
    n = q.shape[0]
    assert bb <= 8
    r3 = lambda i: (i, 0, 0)
    r4 = lambda i: (i, 0, 0, 0)
    q3, k3 = (a.reshape(n, 1, D) for a in (q, k))
    v8, og8 = (jnp.pad(a.reshape(n // bb, bb, D), ((0, 0), (0, 8 - bb), (0, 0))) for a in (v, og))
    args = (q3, k3, v8, og8, gifc.reshape(n, 1, 2 * HEADS), bif_row, c0, n0,
            m0.reshape(n, 1, HEADS))
    in_specs = [pl.BlockSpec((bb, 1, D), r3), pl.BlockSpec((bb, 1, D), r3),
                pl.BlockSpec((1, 8, D), r3), pl.BlockSpec((1, 8, D), r3),
                pl.BlockSpec((bb, 1, 2 * HEADS), r3), _const_spec((1, 2 * HEADS)),
                pl.BlockSpec((bb, HEADS, DH, DH), r4), pl.BlockSpec((bb, HEADS, DH), r3),
                pl.BlockSpec((bb, 1, HEADS), r3)]
    out_shape = (jax.ShapeDtypeStruct((n // bb, 8, D), BF16),
                 jax.ShapeDtypeStruct((n, HEADS, DH, DH), F32),
                 jax.ShapeDtypeStruct((n, HEADS, DH), F32),
                 jax.ShapeDtypeStruct((n, 1, HEADS), F32))
    out_specs = (pl.BlockSpec((1, 8, D), r3), pl.BlockSpec((bb, HEADS, DH, DH), r4),
                 pl.BlockSpec((bb, HEADS, DH), r3), pl.BlockSpec((bb, 1, HEADS), r3))
    return args, in_specs, out_shape, out_specs


def _mlstm_decode_results(y8, c_new, n_new, m_new):
    n = c_new.shape[0]
    bb = n // y8.shape[0]
    return y8[:, :bb].reshape(n, D), c_new, n_new, m_new.reshape(n, HEADS)


def _xattn_prompt_kernel(xq_ref, mk_ref, mv_ref, y_ref):
    scale = XDH ** -0.5
    for h in range(XH):
        hs = slice(h * XDH, (h + 1) * XDH)
        s = _dot_nt(xq_ref[:, hs], mk_ref[:, hs].astype(BF16)) * scale
        p = jnp.exp(s - jnp.max(s, axis=-1, keepdims=True))
        l = jnp.sum(p, axis=-1, keepdims=True)
        o = _dot(p.astype(BF16), mv_ref[:, hs].astype(BF16)) / l
        y_ref[:, hs] = o.astype(y_ref.dtype)


def _prompt_branches(xq, mk, mv, u, w_pool, scale, gift, bif, batch, t_len):
    tq = TM_PROMPT
    nq = t_len // tq

    def attention(xq_ref, mk_ref, mv_ref, y_ref):
        for r in range(nq):
            rows = slice(r * tq, (r + 1) * tq)
            _xattn_prompt_kernel(xq_ref.at[rows], mk_ref, mv_ref, y_ref.at[rows])

    seq = lambda b: (b, 0)
    seq3 = lambda b: (b, 0, 0)
    pool = (_pool_prompt_kernel, (u, w_pool, scale),
            [pl.BlockSpec((t_len, POOL_W), seq), _const_spec((len(POOL_WINDOWS), POOL_G, POOL_G)),
             _const_spec((1, POOL_W))],
            (jax.ShapeDtypeStruct((batch * t_len, POOL_W), BF16),
             jax.ShapeDtypeStruct((batch, POOL_BUF, POOL_W), F32)),
            (pl.BlockSpec((t_len, POOL_W), seq), pl.BlockSpec((1, POOL_BUF, POOL_W), seq3)), [])
    gates = (_gates_kernel, (gift, bif),
             [pl.BlockSpec((2 * HEADS, t_len), lambda b: (0, b)), _const_spec((2 * HEADS, 1))],
             (jax.ShapeDtypeStruct((batch, 2 * HEADS, t_len), F32),
              jax.ShapeDtypeStruct((batch, t_len, 128), F32),
              jax.ShapeDtypeStruct((batch, HEADS, 1), F32)),
             (pl.BlockSpec((1, 2 * HEADS, t_len), seq3), pl.BlockSpec((1, t_len, 128), seq3),
              pl.BlockSpec((1, HEADS, 1), seq3)),
             [pltpu.VMEM((128, t_len), F32)])
    body, args, in_specs, out_shape, out_specs, scratch = _attach_side(
        attention, (xq, mk, mv),
        [pl.BlockSpec((t_len, XW), seq), pl.BlockSpec((N_MEM, XW), seq),
         pl.BlockSpec((N_MEM, XW), seq)],
        (jax.ShapeDtypeStruct((batch * t_len, XW), BF16),), (pl.BlockSpec((t_len, XW), seq),), [],
        _join_sides(pool, gates))
    return pl.pallas_call(
        body,
        out_shape=out_shape,
        grid=(batch,),
        in_specs=in_specs,
        out_specs=out_specs,
        scratch_shapes=scratch,
        compiler_params=_params(("parallel",)),
        name="prompt_branches",
    )(*args)


def _xattn_decode_kernel(q_ref, mk_ref, mv_ref, y_ref):
    scale = XDH ** -0.5
    q = q_ref[...][:, None, :, :]
    s = jnp.sum(mk_ref[...] * q, axis=-1, keepdims=True) * scale
    mx = jnp.max(s, axis=1, keepdims=True)
    mx = jnp.maximum(mx, pltpu.roll(mx, XH, axis=2))
    p = jnp.exp(s - mx)
    l = jnp.sum(p, axis=1, keepdims=True)
    l = l + pltpu.roll(l, XH, axis=2)
    o = jnp.sum(p * mv_ref[...], axis=1, keepdims=True)
    o = (o + pltpu.roll(o, XH, axis=2)) / l
    y_ref[...] = o[:, 0, 0:XH, :].astype(y_ref.dtype)


def _xattn_decode_operands(xq, mk, mv, bb):
    n = xq.shape[0]
    xq4 = xq.reshape(n, XH, XDH)
    q2 = jnp.concatenate([xq4, xq4], axis=1)
    mk2, mv2 = (a.reshape(n, N_MEM // 2, 2 * XH, XDH) for a in (mk, mv))
    kv_spec = pl.BlockSpec((bb, N_MEM // 2, 2 * XH, XDH), lambda i: (i, 0, 0, 0))
    in_specs = [pl.BlockSpec((bb, 2 * XH, XDH), lambda i: (i, 0, 0)), kv_spec, kv_spec]
    out_shape = jax.ShapeDtypeStruct((n, XH, XDH), BF16)
    out_spec = pl.BlockSpec((bb, XH, XDH), lambda i: (i, 0, 0))
    return (q2, mk2, mv2), in_specs, out_shape, out_spec


def _merge_kernel(x_ref, yp_ref, ym_ref, yx_ref, gpre_ref, gpost_ref, wg_ref, wbp_ref, wbm_ref,
                  wbx_ref, wo_ref, o_ref, merged_ref):
    x = x_ref[...]
    h = _rms(x, gpre_ref[...]).astype(BF16)
    yp = yp_ref[...]
    ym = ym_ref[...]
    yx = yx_ref[...]
    nchunk = 256
    for c in range(D // nchunk):
        cs = slice(c * nchunk, (c + 1) * nchunk)
        acc = _sigmoid(_dot_nt(h, wg_ref[c * nchunk:(c + 1) * nchunk, :])) * _dot(yp, wbp_ref[:, cs])
        acc += (_sigmoid(_dot_nt(h, wg_ref[D + c * nchunk:D + (c + 1) * nchunk, :]))
                * _dot(ym, wbm_ref[:, cs]))
        acc += (_sigmoid(_dot_nt(h, wg_ref[2 * D + c * nchunk:2 * D + (c + 1) * nchunk, :]))
                * _dot(yx, wbx_ref[:, cs]))
        merged_ref[:, cs] = acc.astype(BF16)
    o_ref[...] = x + _rms(_dot(merged_ref[...], wo_ref[...]), gpost_ref[...])


def _merge(x2, yp, ym, yx, gpre, gpost, wg, wbp, wbm, wbx, wo, tm, row0=0, side=None):
    n = ym.shape[0]
    steps = n // tm
    blk0 = row0 // tm
    assert row0 % tm == 0
    row = lambda i: (i, 0)
    off = lambda i: (blk0 + i, 0)
    args = (x2, yp, ym, yx, gpre, gpost, wg, wbp, wbm, wbx, wo)
    in_specs = [pl.BlockSpec((tm, D), off), pl.BlockSpec((tm, POOL_W), off),
                pl.BlockSpec((tm, D), row), pl.BlockSpec((tm, XW), off),
                _const_spec((1, D)), _const_spec((1, D)), _const_spec((3 * D, D)),
                _const_spec((POOL_W, D)), _const_spec((D, D)), _const_spec((XW, D)),
                _const_spec((D, D))]
    out_shape = jax.ShapeDtypeStruct((n, D), F32)
    out_specs = pl.BlockSpec((tm, D), row)
    body = _merge_kernel
    scratch = [pltpu.VMEM((tm, D), BF16)]
    if side is not None:
        s_body, s_args, s_in, s_shape, s_out, s_scratch = side
        body = functools.partial(_with_side_job, _merge_kernel, (len(args), 1, 1),
                                 s_body, (len(s_args), len(s_shape), len(s_scratch)))
        args, in_specs = args + tuple(s_args), in_specs + list(s_in)
        out_shape, out_specs = (out_shape,) + tuple(s_shape), (out_specs,) + tuple(s_out)
        scratch = scratch + list(s_scratch)
    return pl.pallas_call(
        body,
        out_shape=out_shape,
        grid=(steps,),
        in_specs=in_specs,
        out_specs=out_specs,
        scratch_shapes=scratch,
        compiler_params=_params(("parallel",)),
        name="merge_out",
    )(*args)


def _ffn_kernel(x_ref, gpre_ref, gpost_ref, w1_ref, w2_ref, o_ref):
    x = x_ref[...]
    h = _rms(x, gpre_ref[...]).astype(BF16)
    fchunk = 1024
    acc = None
    for c in range(D_FF // fchunk):
        a = _dot(h, w1_ref[:, c * fchunk:(c + 1) * fchunk])
        a = jnp.square(jnp.maximum(a, 0.0)).astype(BF16)
        part = _dot(a, w2_ref[c * fchunk:(c + 1) * fchunk, :])
        acc = part if acc is None else acc + part
    o_ref[...] = x + _rms(acc, gpost_ref[...])


def _ffn(x2, gpre, gpost, w1, w2, tm, side=None):
    n = x2.shape[0]
    row = lambda i: (i, 0)
    body, args, in_specs, out_shape, out_specs, scratch = _attach_side(
        _ffn_kernel, (x2, gpre, gpost, w1, w2),
        [pl.BlockSpec((tm, D), row), _const_spec((1, D)), _const_spec((1, D)),
         _const_spec((D, D_FF)), _const_spec((D_FF, D))],
        (jax.ShapeDtypeStruct((n, D), F32),), (pl.BlockSpec((tm, D), row),), [], side)
    out = pl.pallas_call(
        body,
        out_shape=out_shape,
        grid=(n // tm,),
        in_specs=in_specs,
        out_specs=out_specs,
        scratch_shapes=scratch,
        compiler_params=_params(("parallel",)),
        name="ffn",
    )(*args)
    return out[0] if side is None else out


def kernel(x_prompt, x_sample, mem_prompt, state_pool_buf, state_mlstm_C, state_mlstm_n, state_mlstm_m, cache_mem_k, cache_mem_v, g_pre_mix, w_in, b_if, w_pool, pool_scale, g_mem, w_mem_kv, w_br_pool, w_br_mlstm, w_br_xattn, w_out, g_post_mix, g_pre_mlp, w_ff1, w_ff2, g_post_mlp):
    batch, t_len, _ = x_prompt.shape
    n_dec = x_sample.shape[0]
    assert w_in.shape[0] == 1, "single layer"

    w_pool_b = w_pool[0].astype(BF16)
    w_kv_b = w_mem_kv[0].astype(BF16)
    g_mix, g_pm, g_mlp, g_pmlp, g_m = (g[0].reshape(1, D) for g in
                                       (g_pre_mix, g_post_mix, g_pre_mlp, g_post_mlp, g_mem))
    scale = pool_scale[0].reshape(1, POOL_W)
    bif = b_if[0]

    w_in_tf = jnp.transpose(w_in[0])
    mk_p, mv_p, mk4_p, mv4_p, w_in_t = _mem_kv(mem_prompt.reshape(batch * N_MEM, D), g_m, w_kv_b,
                                               w_in_tf, O_GATE)
    w_gif_t = jnp.pad(w_in_t[O_GIF:O_XQ], ((0, 128 - 2 * HEADS), (0, 0)))
    w_xq_t = w_in_t[O_XQ:O_GATE]
    w_in_parts = (w_in_t, w_xq_t, w_gif_t)

    xs = x_sample.reshape(n_dec, D)
    u_s, q_s, k_s, v_s, og_s, xq_s, _, gifc_s = _in_proj(xs, g_mix, *w_in_parts, n_dec, F32)
    buf_t = jnp.transpose(state_pool_buf[0], (1, 0, 2))
    ypool_s, buf_s_t = _pool_decode(u_s, buf_t, w_pool_b, scale)
    dec_xattn = (xq_s, cache_mem_k[0], cache_mem_v[0])
    dec_mlstm = (q_s, k_s, v_s, og_s, gifc_s, bif.reshape(1, 2 * HEADS),
                 state_mlstm_C[0], state_mlstm_n[0], state_mlstm_m[0])

    xp = x_prompt.reshape(batch * t_len, D)
    n_tok = batch * t_len
    whole = lambda w: (w[0], 0, w.shape[1])
    conv_merge = _convert_side([whole(w_out), whole(w_br_pool), whole(w_br_mlstm), whole(w_br_xattn),
                                (w_in_tf, O_GATE, 3 * D)], n_tok // TM_INPROJ)
    u, q, k, v, og, xq, gift, _, wo, wbp, wbm, wbx, w_gate = _in_proj(
        xp, g_mix, *w_in_parts, TM_INPROJ, BF16, side=conv_merge)
    y_x, y_pool, buf_p, rowp, colp, m_last = _prompt_branches(
        xq, mk_p, mv_p, u, w_pool_b, scale, gift, bif.reshape(2 * HEADS, 1), batch, t_len)
    merge_w = (g_mix, g_pm, w_gate, wbp, wbm, wbx, wo)
    y_ml, c_p, n_p = _mlstm_prompt(q, k, v, og, rowp, colp, t_len, 0, batch)
    d_args, d_in, d_shape, d_out = _mlstm_decode_operands(*dec_mlstm, n_dec // (n_tok // TM_MERGE))
    side = _join_sides((_mlstm_decode_body, d_args, d_in, d_shape, d_out, MLSTM_DEC_SCRATCH),
                       _convert_side([whole(w_ff1), whole(w_ff2)], n_tok // TM_MERGE))
    x1, *dec_out, w1, w2 = _merge(xp, y_pool, y_ml, y_x, *merge_w, TM_MERGE, 0, side)
    x_args, x_in, x_shape, x_out = _xattn_decode_operands(*dec_xattn, n_dec // (n_tok // TM_FFN))
    yp, yx_s = _ffn(x1, g_mlp, g_pmlp, w1, w2, TM_FFN,
                    side=(_xattn_decode_kernel, x_args, x_in, (x_shape,), (x_out,), []))
    yx_s = yx_s.reshape(n_dec, XW)

    yml_s, c_s, n_s, m_s = _mlstm_decode_results(*dec_out)
    x1_s = _merge(xs, ypool_s, yml_s, yx_s, *merge_w, n_dec)
    ys = _ffn(x1_s, g_mlp, g_pmlp, w1, w2, n_dec)

    return (yp.reshape(batch, t_len, D), ys.reshape(n_dec, 1, D),
            buf_p[None], c_p[None], n_p.reshape(1, batch, HEADS, DH),
            m_last.reshape(1, batch, HEADS),
            mk4_p.reshape(1, batch, N_MEM, XH, XDH), mv4_p.reshape(1, batch, N_MEM, XH, XDH),
            jnp.transpose(buf_s_t, (1, 0, 2))[None], c_s[None], n_s[None], m_s[None])
```

```python
import functools

import jax
import jax.numpy as jnp
from jax import lax
from jax.experimental import pallas as pl
from jax.experimental.pallas import tpu as pltpu

F32 = jnp.float32
BF16 = jnp.bfloat16

D = 1024
POOL_W = 512
POOL_G = 128
POOL_WINDOWS = (2, 4, 8, 16)
POOL_BUF = 15
HEADS = 4
DH = 256
XH = 4
XDH = 128
XW = 512
N_MEM = 256
D_FF = 4096
EPS = 1e-6
NEG = -1e30
PAST_LEN = 16384

TM_PROMPT = 1024
TM_INPROJ = 512
TM_MERGE = 512
TM_FFN = 512
CHUNK = 256
MLSTM_GROUP = 4

VMEM_LIMIT = 56 * 1024 * 1024

O_QKVO = POOL_W
O_GIF = O_QKVO + 4 * D
O_XQ = O_GIF + 2 * HEADS
O_GATE = O_XQ + XW


def _params(sem):
    return pltpu.CompilerParams(dimension_semantics=sem, vmem_limit_bytes=VMEM_LIMIT)


def _const_spec(shape):
    nd = len(shape)
    return pl.BlockSpec(shape, lambda *_: (0,) * nd, pipeline_mode=pl.Buffered(1))


def _rms(x, g):
    ms = jnp.mean(x * x, axis=-1, keepdims=True)
    return x * lax.rsqrt(ms + EPS) * g


def _log_sigmoid(x):
    return jnp.minimum(x, 0.0) - jnp.log(1.0 + jnp.exp(-jnp.abs(x)))


def _sigmoid(x):
    return 1.0 / (1.0 + jnp.exp(-x))


def _dot(a, b):
    return jnp.dot(a, b, preferred_element_type=F32)


def _dot_nt(a, b):
    return lax.dot_general(a, b, (((1,), (1,)), ((), ())), preferred_element_type=F32)


def _dot_tn(a, b):
    return lax.dot_general(a, b, (((0,), (0,)), ((), ())), preferred_element_type=F32)


def _with_side_job(host_body, n_host, side_body, n_side, *refs):
    (hi, ho, hs), (si, so, ss) = n_host, n_side
    ins, outs, scr = refs[:hi + si], refs[hi + si:hi + si + ho + so], refs[hi + si + ho + so:]
    assert len(scr) == hs + ss
    side_body(*ins[hi:], *outs[ho:], *scr[hs:])
    host_body(*ins[:hi], *outs[:ho], *scr[:hs])


def _attach_side(body, args, in_specs, out_shape, out_specs, scratch, side):
    if side is None:
        return body, args, in_specs, out_shape, out_specs, scratch
    s_body, s_args, s_in, s_shape, s_out, s_scratch = side
    body = functools.partial(_with_side_job, body, (len(args), len(out_shape), len(scratch)),
                             s_body, (len(s_args), len(s_shape), len(s_scratch)))
    return (body, tuple(args) + tuple(s_args), list(in_specs) + list(s_in),
            tuple(out_shape) + tuple(s_shape), tuple(out_specs) + tuple(s_out),
            list(scratch) + list(s_scratch))


def _join_sides(a, b):
    na, nb = (len(a[1]), len(a[3]), len(a[5])), (len(b[1]), len(b[3]), len(b[5]))

    def body(*refs):
        ins, outs = refs[:na[0] + nb[0]], refs[na[0] + nb[0]:na[0] + nb[0] + na[1] + nb[1]]
        scr = refs[na[0] + nb[0] + na[1] + nb[1]:]
        a[0](*ins[:na[0]], *outs[:na[1]], *scr[:na[2]])
        b[0](*ins[na[0]:], *outs[na[1]:], *scr[na[2]:])

    return (body, tuple(a[1]) + tuple(b[1]), list(a[2]) + list(b[2]), tuple(a[3]) + tuple(b[3]),
            tuple(a[4]) + tuple(b[4]), list(a[5]) + list(b[5]))


def _convert_body(*refs):
    half = len(refs) // 2
    for src, dst in zip(refs[:half], refs[half:]):
        dst[...] = src[...].astype(dst.dtype)


def _convert_side(weights, steps):
    row = lambda i: (i, 0)
    args, in_specs, out_shape, out_specs = (), [], (), ()
    for w, row0, nrows in weights:
        per = nrows // steps
        spec = pl.BlockSpec((per, w.shape[1]), row)
        src = spec if row0 == 0 and nrows == w.shape[0] else pl.BlockSpec(
            (pl.Element(per), pl.Element(w.shape[1])),
            lambda i, r=row0, p=per: (pl.multiple_of(r + i * p, 8), 0))
        args, in_specs = args + (w,), in_specs + [src]
        out_shape += (jax.ShapeDtypeStruct((nrows, w.shape[1]), BF16),)
        out_specs += (spec,)
    return _convert_body, args, in_specs, out_shape, out_specs, []


def _in_proj_kernel(x_ref, g_ref, w_ref, wxq_ref, wgif_ref, u_ref, q_ref, k_ref, v_ref, og_ref,
                    xq_ref, gift_ref, gifc_ref):
    h = _rms(x_ref[...], g_ref[...]).astype(BF16)

    def seg(lo, n):
        return _dot_nt(h, w_ref[lo:lo + n, :])

    u_ref[...] = seg(0, POOL_W)
    q_ref[...] = seg(POOL_W, D).astype(q_ref.dtype)
    k_ref[...] = (seg(POOL_W + D, D) * (DH ** -0.5)).astype(k_ref.dtype)
    v_ref[...] = seg(POOL_W + 2 * D, D).astype(v_ref.dtype)
    og_ref[...] = _sigmoid(seg(POOL_W + 3 * D, D))
    xq_ref[...] = _dot_nt(h, wxq_ref[...]).astype(xq_ref.dtype)
    gif = _dot_nt(h, wgif_ref[...])
    gifc_ref[...] = gif[:, 0:2 * HEADS]
    for r in range(gif.shape[0] // 128):
        gift_ref[:, r * 128:(r + 1) * 128] = gif[r * 128:(r + 1) * 128, :].T[0:2 * HEADS, :]


def _in_proj(x2, g, w_main_t, w_xq_t, w_gif_t, tm, qkv_dtype, side=None):
    n = x2.shape[0]
    steps = n // tm
    row = lambda i: (i, 0)
    out_shape = (
        jax.ShapeDtypeStruct((n, POOL_W), F32),
        jax.ShapeDtypeStruct((n, D), qkv_dtype),
        jax.ShapeDtypeStruct((n, D), qkv_dtype),
        jax.ShapeDtypeStruct((n, D), qkv_dtype),
        jax.ShapeDtypeStruct((n, D), F32),
        jax.ShapeDtypeStruct((n, XW), qkv_dtype),
        jax.ShapeDtypeStruct((2 * HEADS, n), F32),
        jax.ShapeDtypeStruct((n, 2 * HEADS), F32),
    )
    args = (x2, g, w_main_t, w_xq_t, w_gif_t)
    in_specs = [
        pl.BlockSpec((tm, D), row),
        _const_spec((1, D)),
        _const_spec((O_GIF, D)),
        _const_spec(w_xq_t.shape),
        _const_spec(w_gif_t.shape),
    ]
    out_specs = (
        pl.BlockSpec((tm, POOL_W), row),
        pl.BlockSpec((tm, D), row),
        pl.BlockSpec((tm, D), row),
        pl.BlockSpec((tm, D), row),
        pl.BlockSpec((tm, D), row),
        pl.BlockSpec((tm, XW), row),
        pl.BlockSpec((2 * HEADS, tm), lambda i: (0, i)),
        pl.BlockSpec((tm, 2 * HEADS), row),
    )
    body, args, in_specs, out_shape, out_specs, scratch = _attach_side(
        _in_proj_kernel, args, in_specs, out_shape, out_specs, [], side)
    return pl.pallas_call(
        body,
        out_shape=out_shape,
        grid=(steps,),
        in_specs=in_specs,
        out_specs=out_specs,
        scratch_shapes=scratch,
        compiler_params=_params(("parallel",)),
        name="in_proj",
    )(*args)


MEM_KV_STEPS = 4


def _mem_kv_kernel(mem_ref, g_ref, w_ref, wt_ref, k_ref, v_ref, k4_ref, v4_ref, wtb_ref):
    wtb_ref[...] = wt_ref[...].astype(BF16)
    h = _rms(mem_ref[...], g_ref[...]).astype(BF16)
    kv = _dot(h, w_ref[...])
    k_ref[...] = kv[:, :XW]
    v_ref[...] = kv[:, XW:]
    for hd in range(XH):
        k4_ref[:, hd, :] = kv[:, hd * XDH:(hd + 1) * XDH]
        v4_ref[:, hd, :] = kv[:, XW + hd * XDH:XW + (hd + 1) * XDH]


def _mem_kv(mem2, g, w, w_in_t, n_rows):
    n = mem2.shape[0]
    tm = n // MEM_KV_STEPS
    wr = -(-n_rows // (MEM_KV_STEPS * 16)) * 16
    row = lambda i: (i, 0)
    row3 = lambda i: (i, 0, 0)
    return pl.pallas_call(
        _mem_kv_kernel,
        out_shape=(jax.ShapeDtypeStruct((n, XW), F32), jax.ShapeDtypeStruct((n, XW), F32),
                   jax.ShapeDtypeStruct((n, XH, XDH), F32), jax.ShapeDtypeStruct((n, XH, XDH), F32),
                   jax.ShapeDtypeStruct((n_rows, D), BF16)),
        grid=(MEM_KV_STEPS,),
        in_specs=[pl.BlockSpec((tm, D), row), _const_spec((1, D)), _const_spec((D, 2 * XW)),
                  pl.BlockSpec((wr, D), row)],
        out_specs=(pl.BlockSpec((tm, XW), row), pl.BlockSpec((tm, XW), row),
                   pl.BlockSpec((tm, XH, XDH), row3), pl.BlockSpec((tm, XH, XDH), row3),
                   pl.BlockSpec((wr, D), row)),
        compiler_params=_params(("parallel",)),
        name="mem_kv",
    )(mem2, g, w, w_in_t)


def _pool_prompt_kernel(u_ref, w_ref, s_ref, y_ref, nb_ref):
    t_len = u_ref.shape[0]
    t_idx = lax.broadcasted_iota(jnp.int32, (t_len, POOL_G), 0)
    for g, win in enumerate(POOL_WINDOWS):
        cols = slice(g * POOL_G, (g + 1) * POOL_G)
        u = u_ref[:, cols]
        acc = u
        span = 1
        while span < win:
            shifted = pltpu.roll(acc, span, axis=0)
            acc = acc + jnp.where(t_idx >= span, shifted, 0.0)
            span *= 2
        cnt = jnp.minimum(t_idx + 1, win).astype(F32)
        d = (acc / cnt - u).astype(BF16)
        y = _dot(d, w_ref[g]) * s_ref[:, cols]
        y_ref[:, cols] = y.astype(y_ref.dtype)
    nb_ref[0] = u_ref[t_len - POOL_BUF:, :]


def _pool_decode_kernel(u_ref, buf_ref, w_ref, s_ref, y_ref, nb_ref):
    u_all = u_ref[...]
    for g, win in enumerate(POOL_WINDOWS):
        cols = slice(g * POOL_G, (g + 1) * POOL_G)
        u = u_all[:, cols]
        acc = u
        for j in range(POOL_BUF - (win - 1), POOL_BUF):
            acc = acc + buf_ref[j, :, cols]
        cnt = float(min(win, PAST_LEN + 1))
        d = (acc / cnt - u).astype(BF16)
        y = _dot(d, w_ref[g]) * s_ref[:, cols]
        y_ref[:, cols] = y.astype(y_ref.dtype)
    for j in range(POOL_BUF - 1):
        nb_ref[j] = buf_ref[j + 1]
    nb_ref[POOL_BUF - 1] = u_all


def _pool_decode(u, buf_t, w_pool, scale):
    n = u.shape[0]
    return pl.pallas_call(
        _pool_decode_kernel,
        out_shape=(jax.ShapeDtypeStruct((n, POOL_W), BF16),
                   jax.ShapeDtypeStruct((POOL_BUF, n, POOL_W), F32)),
        grid=(1,),
        in_specs=[_const_spec((n, POOL_W)), _const_spec((POOL_BUF, n, POOL_W)),
                  _const_spec((len(POOL_WINDOWS), POOL_G, POOL_G)), _const_spec((1, POOL_W))],
        out_specs=(pl.BlockSpec((n, POOL_W), lambda i: (0, 0)),
                   pl.BlockSpec((POOL_BUF, n, POOL_W), lambda i: (0, 0, 0))),
        compiler_params=_params(("arbitrary",)),
        name="pool_decode",
    )(u, buf_t, w_pool, scale)


COL_M, COL_INTER, COL_EINV, COL_END, COL_DECAY = (i * HEADS for i in range(5))


def _scan_lanes(x, op, fill):
    t_len = x.shape[-1]
    lane = lax.broadcasted_iota(jnp.int32, x.shape, 1)
    k = 1
    while k < t_len:
        shifted = pltpu.roll(x, k, axis=1)
        x = op(x, jnp.where(lane >= k, shifted, fill))
        k *= 2
    return x


def _gates_kernel(gift_ref, bif_ref, row_ref, col_ref, mlast_ref, pack_ref):
    t_len = gift_ref.shape[1]
    L = CHUNK
    g = gift_ref[...] + bif_ref[...]
    ig = g[0:HEADS, :]
    lf = _log_sigmoid(g[HEADS:2 * HEADS, :])
    b_cum = _scan_lanes(lf, jnp.add, 0.0)
    a = ig - b_cum
    m_run = jnp.maximum(_scan_lanes(a, jnp.maximum, NEG), 0.0)
    m_tot = b_cum + m_run
    row_ref[0, 0:HEADS, :] = a
    row_ref[0, HEADS:2 * HEADS, :] = m_run
    mlast_ref[0] = m_tot[:, t_len - 1:t_len]
    pack_ref[...] = jnp.zeros(pack_ref.shape, F32)
    pack_ref[COL_M:COL_M + HEADS, :] = m_run
    pack_ref[COL_EINV:COL_EINV + HEADS, :] = jnp.exp(-m_tot)
    for c in range(t_len // L):
        lo, hi = c * L, (c + 1) * L
        m_prev = jnp.zeros((HEADS, 1), F32) if c == 0 else m_run[:, lo - 1:lo]
        m_end = m_run[:, hi - 1:hi]
        pack_ref[COL_INTER:COL_INTER + HEADS, lo:hi] = jnp.exp(m_prev - m_run[:, lo:hi])
        pack_ref[COL_END:COL_END + HEADS, lo:hi] = jnp.exp(a[:, lo:hi] - m_end)
        pack_ref[COL_DECAY:COL_DECAY + HEADS, lo:hi] = jnp.broadcast_to(
            jnp.exp(m_prev - m_end), (HEADS, L))
    for c in range(t_len // 128):
        col_ref[0, c * 128:(c + 1) * 128, :] = pack_ref[:, c * 128:(c + 1) * 128].T


def _mlstm_prompt_body(n_chunks, q_ref, k_ref, v_ref, og_ref, row_ref, col_ref, y_ref, c_ref, n_ref):
    c_idx = pl.program_id(0) % n_chunks
    L = q_ref.shape[1]

    @pl.when(c_idx == 0)
    def _():
        c_ref[...] = jnp.zeros(c_ref.shape, F32)
        n_ref[...] = jnp.zeros(n_ref.shape, F32)

    t_idx = lax.broadcasted_iota(jnp.int32, (L, L), 0)
    s_idx = lax.broadcasted_iota(jnp.int32, (L, L), 1)
    causal = s_idx <= t_idx
    for b in range(q_ref.shape[0]):
        for h in range(HEADS):
            hs = slice(h * DH, (h + 1) * DH)
            qh = q_ref[b, :, hs]
            kh = k_ref[b, :, hs]
            vh = v_ref[b, :, hs]
            a_row = row_ref[b, h:h + 1, :]
            m_col = col_ref[b, :, COL_M + h:COL_M + h + 1]
            w_inter = col_ref[b, :, COL_INTER + h:COL_INTER + h + 1]
            einv = col_ref[b, :, COL_EINV + h:COL_EINV + h + 1]
            w_end = col_ref[b, :, COL_END + h:COL_END + h + 1]
            decay = col_ref[b, 0:1, COL_DECAY + h:COL_DECAY + h + 1]

            dmat = jnp.exp(jnp.where(causal, a_row - m_col, NEG))
            sw = _dot_nt(qh, kh) * dmat
            c_old = c_ref[b, h]
            n_old = n_ref[b, h]
            inter = _dot_nt(qh, c_old.astype(BF16))
            num = _dot(sw.astype(BF16), vh) + w_inter * inter
            nq = jnp.sum(qh.astype(F32) * n_old, axis=-1, keepdims=True)
            den = jnp.sum(sw, axis=-1, keepdims=True) + w_inter * nq
            r = 1.0 / jnp.maximum(jnp.abs(den), einv)
            y_ref[b, :, hs] = (og_ref[b, :, hs] * (num * r)).astype(y_ref.dtype)

            vw = (vh.astype(F32) * w_end).astype(BF16)
            c_ref[b, h] = decay * c_old + _dot_tn(vw, kh)
            n_ref[b, h] = decay * n_old + jnp.sum(kh.astype(F32) * w_end, axis=0, keepdims=True)


def _mlstm_prompt_operands(q, k, v, og, rowp, colp, t_len, b_lo, nb, gb):
    L = CHUNK
    nc = t_len // L
    batch = q.shape[0] // t_len
    g0 = b_lo // gb
    assert b_lo % gb == 0 and nb % gb == 0
    q3, k3, v3, og3 = (a.reshape(batch, t_len, D) for a in (q, k, v, og))
    tok_in = lambda i: (g0 + i // nc, i % nc, 0)
    tok_out = lambda i: (i // nc, i % nc, 0)
    in_specs = [pl.BlockSpec((gb, L, D), tok_in), pl.BlockSpec((gb, L, D), tok_in),
                pl.BlockSpec((gb, L, D), tok_in), pl.BlockSpec((gb, L, D), tok_in),
                pl.BlockSpec((gb, 2 * HEADS, L), lambda i: (g0 + i // nc, 0, i % nc)),
                pl.BlockSpec((gb, L, 128), tok_in)]
    out_shape = (jax.ShapeDtypeStruct((nb, t_len, D), BF16),
                 jax.ShapeDtypeStruct((nb, HEADS, DH, DH), F32),
                 jax.ShapeDtypeStruct((nb, HEADS, 1, DH), F32))
    out_specs = (pl.BlockSpec((gb, L, D), tok_out),
                 pl.BlockSpec((gb, HEADS, DH, DH), lambda i: (i // nc, 0, 0, 0)),
                 pl.BlockSpec((gb, HEADS, 1, DH), lambda i: (i // nc, 0, 0, 0)))
    body = functools.partial(_mlstm_prompt_body, nc)
    return body, (q3, k3, v3, og3, rowp, colp), in_specs, out_shape, out_specs, (nb // gb) * nc


def _mlstm_prompt(q, k, v, og, rowp, colp, t_len, b_lo, nb):
    body, args, in_specs, out_shape, out_specs, steps = _mlstm_prompt_operands(
        q, k, v, og, rowp, colp, t_len, b_lo, nb, MLSTM_GROUP)
    y, c_fin, n_fin = pl.pallas_call(
        body,
        out_shape=out_shape,
        grid=(steps,),
        in_specs=in_specs,
        out_specs=out_specs,
        compiler_params=_params(("arbitrary",)),
        name="mlstm_prompt",
    )(*args)
    return y.reshape(nb * t_len, D), c_fin, n_fin


def _mlstm_decode_body(q_ref, k_ref, v8_ref, og8_ref, gif_ref, bif_ref, c0_ref, n0_ref, m0_ref,
                       y8_ref, c_out_ref, n_out_ref, m_out_ref, vt_ref, ht_ref):
    vrep = jnp.concatenate([v8_ref[0]] * 16, axis=0)
    for r in range(D // 128):
        rs = slice(r * 128, (r + 1) * 128)
        vt_ref[rs, :] = vrep[:, rs].T
    ht_ref[...] = jnp.zeros(ht_ref.shape, F32)
    for j in range(q_ref.shape[0]):
        gi = gif_ref[j] + bif_ref[...]
        m0 = m0_ref[j]
        for h in range(HEADS):
            hs = slice(h * DH, (h + 1) * DH)
            qr = q_ref[j, :, hs]
            kr = k_ref[j, :, hs]
            vc = vt_ref[hs, j:j + 1]
            ig = gi[:, h:h + 1]
            lf = _log_sigmoid(gi[:, HEADS + h:HEADS + h + 1])
            m_old = m0[:, h:h + 1]
            m_new = jnp.maximum(lf + m_old, ig)
            w_i = jnp.exp(ig - m_new)
            w_f = jnp.exp(lf + m_old - m_new)
            c_old = c0_ref[j, h]
            n_old = n0_ref[j, h:h + 1, :]
            cq = jnp.sum(c_old * qr, axis=-1, keepdims=True)
            qk = jnp.sum(qr * kr, axis=-1, keepdims=True)
            nq = jnp.sum(n_old * qr, axis=-1, keepdims=True)
            num = (w_i * qk) * vc + w_f * cq
            den = w_i * qk + w_f * nq
            hcol = num / jnp.maximum(jnp.abs(den), jnp.exp(-m_new))
            ht_ref[hs, j:j + 1] = hcol
            c_out_ref[j, h] = w_f * c_old + (w_i * vc) * kr
            n_out_ref[j, h:h + 1, :] = w_f * n_old + w_i * kr
            m_out_ref[j, :, h:h + 1] = m_new
    for r in range(D // 128):
        rs = slice(r * 128, (r + 1) * 128)
        y8_ref[0, :, rs] = (og8_ref[0, :, rs] * ht_ref[rs, :].T[0:8, :]).astype(y8_ref.dtype)


MLSTM_DEC_SCRATCH = [pltpu.VMEM((D, 128), F32), pltpu.VMEM((D, 128), F32)]


def _mlstm_decode_operands(q, k, v, og, gifc, bif_row, c0, n0, m0, bb):
    n = q.shape[0]
    assert bb <= 8
    r3 = lambda i: (i, 0, 0)
    r4 = lambda i: (i, 0, 0, 0)
    q3, k3 = (a.reshape(n, 1, D) for a in (q, k))
    v8, og8 = (jnp.pad(a.reshape(n // bb, bb, D), ((0, 0), (0, 8 - bb), (0, 0))) for a in (v, og))
    args = (q3, k3, v8, og8, gifc.reshape(n, 1, 2 * HEADS), bif_row, c0, n0,
            m0.reshape(n, 1, HEADS))
    in_specs = [pl.BlockSpec((bb, 1, D), r3), pl.BlockSpec((bb, 1, D), r3),
                pl.BlockSpec((1, 8, D), r3), pl.BlockSpec((1, 8, D), r3),
                pl.BlockSpec((bb, 1, 2 * HEADS), r3), _const_spec((1, 2 * HEADS)),
                pl.BlockSpec((bb, HEADS, DH, DH), r4), pl.BlockSpec((bb, HEADS, DH), r3),
                pl.BlockSpec((bb, 1, HEADS), r3)]
    out_shape = (jax.ShapeDtypeStruct((n // bb, 8, D), BF16),
                 jax.ShapeDtypeStruct((n, HEADS, DH, DH), F32),
                 jax.ShapeDtypeStruct((n, HEADS, DH), F32),
                 jax.ShapeDtypeStruct((n, 1, HEADS), F32))
    out_specs = (pl.BlockSpec((1, 8, D), r3), pl.BlockSpec((bb, HEADS, DH, DH), r4),
                 pl.BlockSpec((bb, HEADS, DH), r3), pl.BlockSpec((bb, 1, HEADS), r3))
    return args, in_specs, out_shape, out_specs


def _mlstm_decode_results(y8, c_new, n_new, m_new):
    n = c_new.shape[0]
    bb = n // y8.shape[0]
    return y8[:, :bb].reshape(n, D), c_new, n_new, m_new.reshape(n, HEADS)


def _xattn_prompt_kernel(xq_ref, mk_ref, mv_ref, y_ref):
    scale = XDH ** -0.5
    for h in range(XH):
        hs = slice(h * XDH, (h + 1) * XDH)
        s = _dot_nt(xq_ref[:, hs], mk_ref[:, hs].astype(BF16)) * scale
        p = jnp.exp(s - jnp.max(s, axis=-1, keepdims=True))
        l = jnp.sum(p, axis=-1, keepdims=True)
        o = _dot(p.astype(BF16), mv_ref[:, hs].astype(BF16)) / l
        y_ref[:, hs] = o.astype(y_ref.dtype)


def _prompt_branches(xq, mk, mv, u, w_pool, scale, gift, bif, batch, t_len):
    tq = TM_PROMPT
    nq = t_len // tq

    def attention(xq_ref, mk_ref, mv_ref, y_ref):
        for r in range(nq):
            rows = slice(r * tq, (r + 1) * tq)
            _xattn_prompt_kernel(xq_ref.at[rows], mk_ref, mv_ref, y_ref.at[rows])

    seq = lambda b: (b, 0)
    seq3 = lambda b: (b, 0, 0)
    pool = (_pool_prompt_kernel, (u, w_pool, scale),
            [pl.BlockSpec((t_len, POOL_W), seq), _const_spec((len(POOL_WINDOWS), POOL_G, POOL_G)),
             _const_spec((1, POOL_W))],
            (jax.ShapeDtypeStruct((batch * t_len, POOL_W), BF16),
             jax.ShapeDtypeStruct((batch, POOL_BUF, POOL_W), F32)),
            (pl.BlockSpec((t_len, POOL_W), seq), pl.BlockSpec((1, POOL_BUF, POOL_W), seq3)), [])
    gates = (_gates_kernel, (gift, bif),
             [pl.BlockSpec((2 * HEADS, t_len), lambda b: (0, b)), _const_spec((2 * HEADS, 1))],
             (jax.ShapeDtypeStruct((batch, 2 * HEADS, t_len), F32),
              jax.ShapeDtypeStruct((batch, t_len, 128), F32),
              jax.ShapeDtypeStruct((batch, HEADS, 1), F32)),
             (pl.BlockSpec((1, 2 * HEADS, t_len), seq3), pl.BlockSpec((1, t_len, 128), seq3),
              pl.BlockSpec((1, HEADS, 1), seq3)),
             [pltpu.VMEM((128, t_len), F32)])
    body, args, in_specs, out_shape, out_specs, scratch = _attach_side(
        attention, (xq, mk, mv),
        [pl.BlockSpec((t_len, XW), seq), pl.BlockSpec((N_MEM, XW), seq),
         pl.BlockSpec((N_MEM, XW), seq)],
        (jax.ShapeDtypeStruct((batch * t_len, XW), BF16),), (pl.BlockSpec((t_len, XW), seq),), [],
        _join_sides(pool, gates))
    return pl.pallas_call(
        body,
        out_shape=out_shape,
        grid=(batch,),
        in_specs=in_specs,
        out_specs=out_specs,
        scratch_shapes=scratch,
        compiler_params=_params(("parallel",)),
        name="prompt_branches",
    )(*args)


def _xattn_decode_kernel(q_ref, mk_ref, mv_ref, y_ref):
    scale = XDH ** -0.5
    q = q_ref[...][:, None, :, :]
    s = jnp.sum(mk_ref[...] * q, axis=-1, keepdims=True) * scale
    mx = jnp.max(s, axis=1, keepdims=True)
    mx = jnp.maximum(mx, pltpu.roll(mx, XH, axis=2))
    p = jnp.exp(s - mx)
    l = jnp.sum(p, axis=1, keepdims=True)
    l = l + pltpu.roll(l, XH, axis=2)
    o = jnp.sum(p * mv_ref[...], axis=1, keepdims=True)
    o = (o + pltpu.roll(o, XH, axis=2)) / l
    y_ref[...] = o[:, 0, 0:XH, :].astype(y_ref.dtype)


def _xattn_decode_operands(xq, mk, mv, bb):
    n = xq.shape[0]
    xq4 = xq.reshape(n, XH, XDH)
    q2 = jnp.concatenate([xq4, xq4], axis=1)
    mk2, mv2 = (a.reshape(n, N_MEM // 2, 2 * XH, XDH) for a in (mk, mv))
    kv_spec = pl.BlockSpec((bb, N_MEM // 2, 2 * XH, XDH), lambda i: (i, 0, 0, 0))
    in_specs = [pl.BlockSpec((bb, 2 * XH, XDH), lambda i: (i, 0, 0)), kv_spec, kv_spec]
    out_shape = jax.ShapeDtypeStruct((n, XH, XDH), BF16)
    out_spec = pl.BlockSpec((bb, XH, XDH), lambda i: (i, 0, 0))
    return (q2, mk2, mv2), in_specs, out_shape, out_spec


def _merge_kernel(x_ref, yp_ref, ym_ref, yx_ref, gpre_ref, gpost_ref, wg_ref, wbp_ref, wbm_ref,
                  wbx_ref, wo_ref, o_ref, merged_ref):
    x = x_ref[...]
    h = _rms(x, gpre_ref[...]).astype(BF16)
    yp = yp_ref[...]
    ym = ym_ref[...]
    yx = yx_ref[...]
    nchunk = 256
    for c in range(D // nchunk):
        cs = slice(c * nchunk, (c + 1) * nchunk)
        acc = _sigmoid(_dot_nt(h, wg_ref[c * nchunk:(c + 1) * nchunk, :])) * _dot(yp, wbp_ref[:, cs])
        acc += (_sigmoid(_dot_nt(h, wg_ref[D + c * nchunk:D + (c + 1) * nchunk, :]))
                * _dot(ym, wbm_ref[:, cs]))
        acc += (_sigmoid(_dot_nt(h, wg_ref[2 * D + c * nchunk:2 * D + (c + 1) * nchunk, :]))
                * _dot(yx, wbx_ref[:, cs]))
        merged_ref[:, cs] = acc.astype(BF16)
    o_ref[...] = x + _rms(_dot(merged_ref[...], wo_ref[...]), gpost_ref[...])


def _merge(x2, yp, ym, yx, gpre, gpost, wg, wbp, wbm, wbx, wo, tm, row0=0, side=None):
    n = ym.shape[0]
    steps = n // tm
    blk0 = row0 // tm
    assert row0 % tm == 0
    row = lambda i: (i, 0)
    off = lambda i: (blk0 + i, 0)
    args = (x2, yp, ym, yx, gpre, gpost, wg, wbp, wbm, wbx, wo)
    in_specs = [pl.BlockSpec((tm, D), off), pl.BlockSpec((tm, POOL_W), off),
                pl.BlockSpec((tm, D), row), pl.BlockSpec((tm, XW), off),
                _const_spec((1, D)), _const_spec((1, D)), _const_spec((3 * D, D)),
                _const_spec((POOL_W, D)), _const_spec((D, D)), _const_spec((XW, D)),
                _const_spec((D, D))]
    out_shape = jax.ShapeDtypeStruct((n, D), F32)
    out_specs = pl.BlockSpec((tm, D), row)
    body = _merge_kernel
    scratch = [pltpu.VMEM((tm, D), BF16)]
    if side is not None:
        s_body, s_args, s_in, s_shape, s_out, s_scratch = side
        body = functools.partial(_with_side_job, _merge_kernel, (len(args), 1, 1),
                                 s_body, (len(s_args), len(s_shape), len(s_scratch)))
        args, in_specs = args + tuple(s_args), in_specs + list(s_in)
        out_shape, out_specs = (out_shape,) + tuple(s_shape), (out_specs,) + tuple(s_out)
        scratch = scratch + list(s_scratch)
    return pl.pallas_call(
        body,
        out_shape=out_shape,
        grid=(steps,),
        in_specs=in_specs,
        out_specs=out_specs,
        scratch_shapes=scratch,
        compiler_params=_params(("parallel",)),
        name="merge_out",
    )(*args)


def _ffn_kernel(x_ref, gpre_ref, gpost_ref, w1_ref, w2_ref, o_ref):
    x = x_ref[...]
    h = _rms(x, gpre_ref[...]).astype(BF16)
    fchunk = 4096
    acc = None
    for c in range(D_FF // fchunk):
        a = _dot(h, w1_ref[:, c * fchunk:(c + 1) * fchunk])
        a = jnp.square(jnp.maximum(a, 0.0)).astype(BF16)
        part = _dot(a, w2_ref[c * fchunk:(c + 1) * fchunk, :])
        acc = part if acc is None else acc + part
    o_ref[...] = x + _rms(acc, gpost_ref[...])


def _ffn(x2, gpre, gpost, w1, w2, tm, side=None):
    n = x2.shape[0]
    row = lambda i: (i, 0)
    body, args, in_specs, out_shape, out_specs, scratch = _attach_side(
        _ffn_kernel, (x2, gpre, gpost, w1, w2),
        [pl.BlockSpec((tm, D), row), _const_spec((1, D)), _const_spec((1, D)),
         _const_spec((D, D_FF)), _const_spec((D_FF, D))],
        (jax.ShapeDtypeStruct((n, D), F32),), (pl.BlockSpec((tm, D), row),), [], side)
    out = pl.pallas_call(
        body,
        out_shape=out_shape,
        grid=(n // tm,),
        in_specs=in_specs,
        out_specs=out_specs,
        scratch_shapes=scratch,
        compiler_params=_params(("parallel",)),
        name="ffn",
    )(*args)
    return out[0] if side is None else out


def kernel(x_prompt, x_sample, mem_prompt, state_pool_buf, state_mlstm_C, state_mlstm_n, state_mlstm_m, cache_mem_k, cache_mem_v, g_pre_mix, w_in, b_if, w_pool, pool_scale, g_mem, w_mem_kv, w_br_pool, w_br_mlstm, w_br_xattn, w_out, g_post_mix, g_pre_mlp, w_ff1, w_ff2, g_post_mlp):
    batch, t_len, _ = x_prompt.shape
    n_dec = x_sample.shape[0]
    assert w_in.shape[0] == 1, "single layer"

    w_pool_b = w_pool[0].astype(BF16)
    w_kv_b = w_mem_kv[0].astype(BF16)
    g_mix, g_pm, g_mlp, g_pmlp, g_m = (g[0].reshape(1, D) for g in
                                       (g_pre_mix, g_post_mix, g_pre_mlp, g_post_mlp, g_mem))
    scale = pool_scale[0].reshape(1, POOL_W)
    bif = b_if[0]

    w_in_tf = jnp.transpose(w_in[0])
    mk_p, mv_p, mk4_p, mv4_p, w_in_t = _mem_kv(mem_prompt.reshape(batch * N_MEM, D), g_m, w_kv_b,
                                               w_in_tf, O_GATE)
    w_gif_t = jnp.pad(w_in_t[O_GIF:O_XQ], ((0, 128 - 2 * HEADS), (0, 0)))
    w_xq_t = w_in_t[O_XQ:O_GATE]
    w_in_parts = (w_in_t, w_xq_t, w_gif_t)

    xs = x_sample.reshape(n_dec, D)
    u_s, q_s, k_s, v_s, og_s, xq_s, _, gifc_s = _in_proj(xs, g_mix, *w_in_parts, n_dec, F32)
    buf_t = jnp.transpose(state_pool_buf[0], (1, 0, 2))
    ypool_s, buf_s_t = _pool_decode(u_s, buf_t, w_pool_b, scale)
    dec_xattn = (xq_s, cache_mem_k[0], cache_mem_v[0])
    dec_mlstm = (q_s, k_s, v_s, og_s, gifc_s, bif.reshape(1, 2 * HEADS),
                 state_mlstm_C[0], state_mlstm_n[0], state_mlstm_m[0])

    xp = x_prompt.reshape(batch * t_len, D)
    n_tok = batch * t_len
    whole = lambda w: (w[0], 0, w.shape[1])
    conv_merge = _convert_side([whole(w_out), whole(w_br_pool), whole(w_br_mlstm), whole(w_br_xattn),
                                (w_in_tf, O_GATE, 3 * D)], n_tok // TM_INPROJ)
    u, q, k, v, og, xq, gift, _, wo, wbp, wbm, wbx, w_gate = _in_proj(
        xp, g_mix, *w_in_parts, TM_INPROJ, BF16, side=conv_merge)
    y_x, y_pool, buf_p, rowp, colp, m_last = _prompt_branches(
        xq, mk_p, mv_p, u, w_pool_b, scale, gift, bif.reshape(2 * HEADS, 1), batch, t_len)
    merge_w = (g_mix, g_pm, w_gate, wbp, wbm, wbx, wo)
    y_ml, c_p, n_p = _mlstm_prompt(q, k, v, og, rowp, colp, t_len, 0, batch)
    d_args, d_in, d_shape, d_out = _mlstm_decode_operands(*dec_mlstm, n_dec // (n_tok // TM_MERGE))
    side = _join_sides((_mlstm_decode_body, d_args, d_in, d_shape, d_out, MLSTM_DEC_SCRATCH),
                       _convert_side([whole(w_ff1), whole(w_ff2)], n_tok // TM_MERGE))
    x1, *dec_out, w1, w2 = _merge(xp, y_pool, y_ml, y_x, *merge_w, TM_MERGE, 0, side)
    x_args, x_in, x_shape, x_out = _xattn_decode_operands(*dec_xattn, n_dec // (n_tok // TM_FFN))
    yp, yx_s = _ffn(x1, g_mlp, g_pmlp, w1, w2, TM_FFN,
                    side=(_xattn_decode_kernel, x_args, x_in, (x_shape,), (x_out,), []))
    yx_s = yx_s.reshape(n_dec, XW)

    yml_s, c_s, n_s, m_s = _mlstm_decode_results(*dec_out)
    x1_s = _merge(xs, ypool_s, yml_s, yx_s, *merge_w, n_dec)
    ys = _ffn(x1_s, g_mlp, g_pmlp, w1, w2, n_dec)

    return (yp.reshape(batch, t_len, D), ys.reshape(n_dec, 1, D),
            buf_p[None], c_p[None], n_p.reshape(1, batch, HEADS, DH),
            m_last.reshape(1, batch, HEADS),
            mk4_p.reshape(1, batch, N_MEM, XH, XDH), mv4_p.reshape(1, batch, N_MEM, XH, XDH),
            jnp.transpose(buf_s_t, (1, 0, 2))[None], c_s[None], n_s[None], m_s[None])
```
